```python
import math
import jax
import jax.numpy as jnp
from jax import lax
import numpy as np

D_MODEL = 1024
BATCH = 8
SEQ = 2048
DEPTH = 4
DEC_BATCH = 128
DEC_SEQ = 4
PAST_LEN = 16384
PAGE_SIZE = 128

N_META = 16
N_AB = (DEPTH + 1) // 2
N_C = DEPTH // 2

GLA_HEADS = 4
GLA_DK = D_MODEL // 4
GLA_DV = D_MODEL // 2
GLA_HK = GLA_DK // GLA_HEADS
GLA_HV = GLA_DV // GLA_HEADS
GLA_GATE_RANK = 16
GLA_GATE_NORM = 16.0
GLA_CHUNK = 16
GLA_COLS = 2 * GLA_DK + 2 * GLA_DV + GLA_GATE_RANK

RWKV_DIM = D_MODEL // 2
RWKV_HEAD = 64
RWKV_HEADS = RWKV_DIM // RWKV_HEAD
RWKV_DECAY_RANK = 64
RWKV_A_RANK = 64
RWKV_GATE_RANK = 128
RWKV_COLS = 3 * RWKV_DIM + RWKV_DECAY_RANK + RWKV_A_RANK + RWKV_GATE_RANK
RWKV_GN_EPS = 64e-5

AB_COLS = GLA_COLS + RWKV_COLS
AB_OUT = GLA_DV + RWKV_DIM

GDN_HEADS = 8
GDN_HEAD = 128
GDN_DIM = GDN_HEADS * GDN_HEAD
GDN_CONV = 4
GDN_CHUNK = 64
C_COLS = 4 * GDN_DIM + 2 * GDN_HEADS

D_FF = 2816
FFN_CONV = 3

LN_EPS = 1e-5
NORM_EPS = 1e-6
DEEPNORM_ALPHA = (2.0 * DEPTH) ** 0.25
DEEPNORM_BETA = (8.0 * DEPTH) ** -0.25

kernel_name = 'hybrid_gla_rwkv7_gdn_convffn_step'


def _split(t, sizes):
    offs, acc = [], 0
    for s in sizes[:-1]:
        acc += s
        offs.append(acc)
    return jnp.split(t, offs, axis=-1)


def _layer_norm(x, g, b):
    xf = x.astype(jnp.float32)
    mu = jnp.mean(xf, axis=-1, keepdims=True)
    var = jnp.mean(jnp.square(xf - mu), axis=-1, keepdims=True)
    y = (xf - mu) * lax.rsqrt(var + LN_EPS) * g.astype(jnp.float32) + b.astype(jnp.float32)
    return y.astype(x.dtype)


def _rms_norm(x, g):
    xf = x.astype(jnp.float32)
    return xf * lax.rsqrt(jnp.mean(jnp.square(xf), axis=-1, keepdims=True) + NORM_EPS) * g.astype(jnp.float32)


def _l2_normalize(x):
    xf = x.astype(jnp.float32)
    return xf * lax.rsqrt(jnp.sum(xf * xf, axis=-1, keepdims=True) + NORM_EPS)


def _causal_dwconv(u, buf, w):
    width = w.shape[0]
    T = u.shape[1]
    full = jnp.concatenate([buf.astype(u.dtype), u], axis=1)
    out = sum(full[:, j:j + T] * w[j] for j in range(width))
    return out, full[:, T:]


def _to_chunks(t, chunk):
    B, T, H, d = t.shape
    return t.reshape(B, T // chunk, chunk, H, d).transpose(1, 0, 3, 2, 4)


def _from_chunks(t):
    n, B, H, L, d = t.shape
    return t.transpose(1, 0, 3, 2, 4).reshape(B, n * L, H, d)


def _gla_chunked(q, k, v, log_a, s0, chunk):
    f32 = jnp.float32
    qc, kc, vc, gc = (_to_chunks(t.astype(f32), chunk) for t in (q, k, v, log_a))
    causal = jnp.tril(jnp.ones((chunk, chunk), bool))

    def step(s, inp):
        qi, ki, vi, gi = inp
        b = jnp.cumsum(gi, axis=-2)
        qd = qi * jnp.exp(b)
        kd = ki * jnp.exp(-b)
        att = jnp.where(causal, jnp.einsum('bhld,bhmd->bhlm', qd, kd), 0.0)
        o = jnp.einsum('bhld,bhdv->bhlv', qd, s) + jnp.einsum('bhlm,bhmv->bhlv', att, vi)
        b_end = b[..., -1:, :]
        s = s * jnp.exp(b_end)[..., 0, :, None] + jnp.einsum('bhld,bhlv->bhdv', ki * jnp.exp(b_end - b), vi)
        return s, o

    s, o = lax.scan(step, s0.astype(f32), (qc, kc, vc, gc))
    return _from_chunks(o), s


def _gated_delta_chunked(q, k, v, beta, g, s0, chunk):
    f32 = jnp.float32
    dv = v.shape[-1]
    qc, kc, vc = (_to_chunks(t.astype(f32), chunk) for t in (q, k, v))
    bc = _to_chunks(beta.astype(f32)[..., None], chunk)[..., 0]
    gc = _to_chunks(g.astype(f32)[..., None], chunk)[..., 0]
    incl = jnp.tril(jnp.ones((chunk, chunk), bool))
    strict = jnp.tril(jnp.ones((chunk, chunk), bool), -1)
    eye = jnp.eye(chunk, dtype=f32)

    def step(s, inp):
        qi, ki, vi, bi, gi = inp
        gam = jnp.cumsum(gi, axis=-1)
        dec = jnp.exp(jnp.where(incl, gam[..., :, None] - gam[..., None, :], -jnp.inf))
        a_mat = jnp.where(strict, jnp.einsum('bhlk,bhmk->bhlm', ki, ki) * dec, 0.0) * bi[..., :, None] + eye
        rhs = jnp.concatenate([vi * bi[..., None], ki * (bi * jnp.exp(gam))[..., None]], axis=-1)
        sol = lax.linalg.triangular_solve(a_mat, rhs, left_side=True, lower=True, unit_diagonal=True)
        u, w = sol[..., :dv], sol[..., dv:]
        delta = u - jnp.einsum('bhlk,bhkv->bhlv', w, s)
        qk = jnp.einsum('bhlk,bhmk->bhlm', qi, ki) * dec
        o = jnp.einsum('bhlk,bhkv->bhlv', qi * jnp.exp(gam)[..., None], s) + jnp.einsum('bhlm,bhmv->bhlv', qk, delta)
        g_end = gam[..., -1:]
        s = s * jnp.exp(g_end)[..., None] + jnp.einsum('bhlk,bhlv->bhkv', ki * jnp.exp(g_end - gam)[..., None], delta)
        return s, o

    s, o = lax.scan(step, s0.astype(f32), (qc, kc, vc, bc, gc))
    return _from_chunks(o), s


def _rwkv7_scan(r, decay, k, v, a, b, s0):
    def step(s, inp):
        rt, wt, kt, vt, at, bt = inp
        sa = jnp.einsum('bhvk,bhk->bhv', s, at)
        s = s * wt[:, :, None, :] + sa[..., None] * bt[:, :, None, :] + vt[..., None] * kt[:, :, None, :]
        return s, jnp.einsum('bhvk,bhk->bhv', s, rt)

    xs = tuple(jnp.moveaxis(t, 1, 0) for t in (r, decay, k, v, a, b))
    s, y = lax.scan(step, s0.astype(jnp.float32), xs)
    return jnp.moveaxis(y, 0, 1), s


def _ab_mixer(x, x_last, s_gla, s_rwkv, p, i, gla_chunk):
    f32 = jnp.float32
    B, T, _ = x.shape
    w_in = p['w_in_ab'][i]
    z = x @ w_in
    z_gla, z_rwkv = z[..., :GLA_COLS], z[..., GLA_COLS:]
    q, k, v, g_lo, og = _split(z_gla, (GLA_DK, GLA_DK, GLA_DV, GLA_GATE_RANK, GLA_DV))
    log_a = jax.nn.log_sigmoid((g_lo @ p['gla_gate_w2'][i] + p['gla_gate_b'][i]).astype(f32)) / GLA_GATE_NORM
    heads_k = lambda t: t.astype(f32).reshape(B, T, GLA_HEADS, GLA_HK)
    o_gla, s_gla_new = _gla_chunked(heads_k(q) * GLA_HK ** -0.5, heads_k(k),
                                    v.astype(f32).reshape(B, T, GLA_HEADS, GLA_HV), heads_k(log_a),
                                    s_gla, gla_chunk)
    o_gla = _rms_norm(o_gla, p['gla_norm_g'][i].reshape(GLA_HEADS, GLA_HV)).reshape(B, T, GLA_DV) * jax.nn.silu(og.astype(f32))
    prev_first = x_last @ w_in[:, GLA_COLS:]
    z_prev = jnp.concatenate([prev_first[:, None].astype(z.dtype), z_rwkv[:, :-1]], axis=1)
    z_rwkv = z_rwkv + p['rwkv_mu'][i] * (z_prev - z_rwkv)
    r, kr, vr, w_lo, a_lo, g_lo2 = _split(z_rwkv.astype(f32), (RWKV_DIM, RWKV_DIM, RWKV_DIM, RWKV_DECAY_RANK, RWKV_A_RANK, RWKV_GATE_RANK))
    w_raw = p['rwkv_w0'][i] + jnp.tanh(w_lo) @ p['rwkv_w2'][i]
    decay = jnp.exp(-jnp.exp(-jax.nn.softplus(-w_raw) - 0.5))
    a_lr = jax.nn.sigmoid(p['rwkv_a0'][i] + a_lo @ p['rwkv_a2'][i])
    gate = jax.nn.sigmoid(g_lo2) @ p['rwkv_g2'][i]
    heads = lambda t: t.reshape(B, T, RWKV_HEADS, RWKV_HEAD)
    kk = _l2_normalize(heads(kr * p['rwkv_k_k'][i]))
    k_mod = kr * (1.0 + (a_lr - 1.0) * p['rwkv_k_a'][i])
    y, s_rwkv_new = _rwkv7_scan(heads(r), heads(decay), heads(k_mod), heads(vr), -kk, kk * heads(a_lr), s_rwkv)
    mu = jnp.mean(y, axis=-1, keepdims=True)
    var = jnp.mean(jnp.square(y - mu), axis=-1, keepdims=True)
    y = ((y - mu) * lax.rsqrt(var + RWKV_GN_EPS)).reshape(B, T, RWKV_DIM) * p['rwkv_ln_g'][i] + p['rwkv_ln_b'][i]
    bonus = jnp.sum(heads(r) * heads(k_mod) * p['rwkv_r_k'][i], axis=-1, keepdims=True) * heads(vr)
    y = (y + bonus.reshape(B, T, RWKV_DIM)) * gate
    mix = jnp.concatenate([o_gla, y], axis=-1).astype(x.dtype) @ p['w_out_ab'][i]
    return mix, s_gla_new, s_rwkv_new, x[:, -1]


def _c_mixer(x, conv_buf, s, p, i, segments):
    f32 = jnp.float32
    B, T, _ = x.shape
    z = x @ p['w_in_c'][i]
    qkv, zg, b_lo, a_lo = _split(z, (3 * GDN_DIM, GDN_DIM, GDN_HEADS, GDN_HEADS))
    qkv, new_buf = _causal_dwconv(qkv, conv_buf, p['gdn_conv_w'][i])
    qkv = jax.nn.silu(qkv.astype(f32))
    q, k, v = (t.reshape(B, T, GDN_HEADS, GDN_HEAD) for t in jnp.split(qkv, 3, axis=-1))
    q = _l2_normalize(q) * GDN_HEAD ** -0.5
    k = _l2_normalize(k)
    beta = jax.nn.sigmoid(b_lo.astype(f32))
    g = -jnp.exp(p['gdn_A_log'][i].astype(f32)) * jax.nn.softplus(a_lo.astype(f32) + p['gdn_dt_bias'][i])
    outs = []
    for start, end, chunk in segments:
        o, s = _gated_delta_chunked(q[:, start:end], k[:, start:end], v[:, start:end],
                                    beta[:, start:end], g[:, start:end], s, chunk)
        outs.append(o)
    o = jnp.concatenate(outs, axis=1)
    o = _rms_norm(o, p['gdn_norm_g'][i]).reshape(B, T, GDN_DIM) * jax.nn.silu(zg.astype(f32))
    return o.astype(x.dtype) @ p['w_out_c'][i], new_buf, s


def _conv_ffn(x, buf, p, l):
    h = x @ p['w_up'][l]
    gate, up = jnp.split(h, [D_FF], axis=-1)
    gate, new_buf = _causal_dwconv(gate, buf, p['ffn_conv_w'][l])
    return (jax.nn.silu(gate + p['ffn_conv_b'][l]) * up) @ p['w_down'][l], new_buf


def _trunk(x, st_gla, st_rwkv, st_shift, st_gdn, st_gdn_conv, st_ffn_conv, p, n_lead):
    T = x.shape[1]
    if n_lead > 0:
        segments = ((0, n_lead, n_lead), (n_lead, T, math.gcd(T - n_lead, GDN_CHUNK)))
    else:
        segments = ((0, T, math.gcd(T, GDN_CHUNK)),)
    gla_chunk = math.gcd(T, GLA_CHUNK)
    new_gla, new_rwkv, new_shift, new_gdn, new_gdn_conv, new_ffn_conv = [], [], [], [], [], []
    for l in range(DEPTH):
        i = l // 2
        if l % 2 == 0:
            mix, s_g, s_r, s_sh = _ab_mixer(x, st_shift[i], st_gla[i], st_rwkv[i], p, i, gla_chunk)
            new_gla.append(s_g.astype(st_gla.dtype))
            new_rwkv.append(s_r.astype(st_rwkv.dtype))
            new_shift.append(s_sh.astype(st_shift.dtype))
        else:
            mix, c_buf, s_d = _c_mixer(x, st_gdn_conv[i], st_gdn[i], p, i, segments)
            new_gdn_conv.append(c_buf.astype(st_gdn_conv.dtype))
            new_gdn.append(s_d.astype(st_gdn.dtype))
        x = _layer_norm(DEEPNORM_ALPHA * x + mix, p['ln_mix_g'][l], p['ln_mix_b'][l])
        f, f_buf = _conv_ffn(x, st_ffn_conv[l], p, l)
        new_ffn_conv.append(f_buf.astype(st_ffn_conv.dtype))
        x = _layer_norm(DEEPNORM_ALPHA * x + f, p['ln_ffn_g'][l], p['ln_ffn_b'][l])
    return (x, jnp.stack(new_gla), jnp.stack(new_rwkv), jnp.stack(new_shift),
            jnp.stack(new_gdn), jnp.stack(new_gdn_conv), jnp.stack(new_ffn_conv))


def setup_inputs(seed: int = 0) -> dict:
    key = jax.random.key(seed)
    ks = iter(jax.random.split(key, 64))
    f32 = jnp.float32
    nrm = lambda shape, std: std * jax.random.normal(next(ks), shape, f32)
    ones_n = lambda shape: 1.0 + nrm(shape, 0.02)
    beta = DEEPNORM_BETA
    ratio = jnp.linspace(0.0, 1.0, RWKV_DIM, dtype=f32)
    w0 = (-6.0 + 5.0 * ratio ** 1.5 + 0.5)[None] + nrm((N_AB, RWKV_DIM), 0.1)
    dt = jnp.exp(jax.random.uniform(next(ks), (N_C, GDN_HEADS), f32, math.log(1e-3), math.log(1e-1)))
    dt_bias = dt + jnp.log(-jnp.expm1(-dt))
    a_log = jnp.log(jax.random.uniform(next(ks), (N_C, GDN_HEADS), f32, 1.0, 16.0))
    return {
        'x_prompt': nrm((BATCH, SEQ, D_MODEL), 1.0),
        'x_sample': nrm((DEC_BATCH, DEC_SEQ, D_MODEL), 1.0),
        'state_gla': nrm((N_AB, DEC_BATCH, GLA_HEADS, GLA_HK, GLA_HV), 0.5),
        'state_rwkv': nrm((N_AB, DEC_BATCH, RWKV_HEADS, RWKV_HEAD, RWKV_HEAD), 0.5),
        'state_rwkv_shift': nrm((N_AB, DEC_BATCH, D_MODEL), 1.0),
        'state_gdn': nrm((N_C, DEC_BATCH, GDN_HEADS, GDN_HEAD, GDN_HEAD), 0.1),
        'state_gdn_conv': nrm((N_C, DEC_BATCH, GDN_CONV - 1, 3 * GDN_DIM), 1.0),
        'state_ffn_conv': nrm((DEPTH, DEC_BATCH, FFN_CONV - 1, D_FF), 1.0),
        'meta_tokens': nrm((N_META, D_MODEL), 1.0),
        'w_in_ab': nrm((N_AB, D_MODEL, AB_COLS), D_MODEL ** -0.5),
        'gla_gate_w2': nrm((N_AB, GLA_GATE_RANK, GLA_DK), GLA_GATE_RANK ** -0.5),
        'gla_gate_b': nrm((N_AB, GLA_DK), 0.1),
        'gla_norm_g': ones_n((N_AB, GLA_DV)),
        'rwkv_mu': jax.random.uniform(next(ks), (N_AB, RWKV_COLS), f32, 0.0, 1.0),
        'rwkv_w0': w0,
        'rwkv_w2': nrm((N_AB, RWKV_DECAY_RANK, RWKV_DIM), 0.1 * RWKV_DECAY_RANK ** -0.5),
        'rwkv_a0': nrm((N_AB, RWKV_DIM), 0.1),
        'rwkv_a2': nrm((N_AB, RWKV_A_RANK, RWKV_DIM), RWKV_A_RANK ** -0.5),
        'rwkv_g2': nrm((N_AB, RWKV_GATE_RANK, RWKV_DIM), RWKV_GATE_RANK ** -0.5),
        'rwkv_k_k': 0.85 + nrm((N_AB, RWKV_DIM), 0.02),
        'rwkv_k_a': ones_n((N_AB, RWKV_DIM)),
        'rwkv_r_k': nrm((N_AB, RWKV_HEADS, RWKV_HEAD), 0.1),
        'rwkv_ln_g': ones_n((N_AB, RWKV_DIM)),
        'rwkv_ln_b': nrm((N_AB, RWKV_DIM), 0.02),
        'w_out_ab': nrm((N_AB, AB_OUT, D_MODEL), AB_OUT ** -0.5 * beta),
        'w_in_c': nrm((N_C, D_MODEL, C_COLS), D_MODEL ** -0.5),
        'gdn_conv_w': nrm((N_C, GDN_CONV, 3 * GDN_DIM), GDN_CONV ** -0.5),
        'gdn_A_log': a_log,
        'gdn_dt_bias': dt_bias,
        'gdn_norm_g': ones_n((N_C, GDN_HEAD)),
        'w_out_c': nrm((N_C, GDN_DIM, D_MODEL), GDN_DIM ** -0.5 * beta),
        'w_up': nrm((DEPTH, D_MODEL, 2 * D_FF), D_MODEL ** -0.5),
        'ffn_conv_w': nrm((DEPTH, FFN_CONV, D_FF), FFN_CONV ** -0.5),
        'ffn_conv_b': nrm((DEPTH, D_FF), 0.02),
        'w_down': nrm((DEPTH, D_FF, D_MODEL), D_FF ** -0.5 * beta),
        'ln_mix_g': ones_n((DEPTH, D_MODEL)),
        'ln_mix_b': nrm((DEPTH, D_MODEL), 0.02),
        'ln_ffn_g': ones_n((DEPTH, D_MODEL)),
        'ln_ffn_b': nrm((DEPTH, D_MODEL), 0.02),
    }


def reference(x_prompt, x_sample, state_gla, state_rwkv, state_rwkv_shift, state_gdn, state_gdn_conv,
              state_ffn_conv, meta_tokens, w_in_ab, gla_gate_w2, gla_gate_b, gla_norm_g, rwkv_mu, rwkv_w0,
              rwkv_w2, rwkv_a0, rwkv_a2, rwkv_g2, rwkv_k_k, rwkv_k_a, rwkv_r_k, rwkv_ln_g, rwkv_ln_b,
              w_out_ab, w_in_c, gdn_conv_w, gdn_A_log, gdn_dt_bias, gdn_norm_g, w_out_c, w_up,
              ffn_conv_w, ffn_conv_b, w_down, ln_mix_g, ln_mix_b, ln_ffn_g, ln_ffn_b):
    p = {
        'w_in_ab': w_in_ab, 'gla_gate_w2': gla_gate_w2, 'gla_gate_b': gla_gate_b, 'gla_norm_g': gla_norm_g,
        'rwkv_mu': rwkv_mu, 'rwkv_w0': rwkv_w0, 'rwkv_w2': rwkv_w2, 'rwkv_a0': rwkv_a0, 'rwkv_a2': rwkv_a2,
        'rwkv_g2': rwkv_g2, 'rwkv_k_k': rwkv_k_k, 'rwkv_k_a': rwkv_k_a, 'rwkv_r_k': rwkv_r_k,
        'rwkv_ln_g': rwkv_ln_g, 'rwkv_ln_b': rwkv_ln_b, 'w_out_ab': w_out_ab,
        'w_in_c': w_in_c, 'gdn_conv_w': gdn_conv_w, 'gdn_A_log': gdn_A_log, 'gdn_dt_bias': gdn_dt_bias,
        'gdn_norm_g': gdn_norm_g, 'w_out_c': w_out_c,
        'w_up': w_up, 'ffn_conv_w': ffn_conv_w, 'ffn_conv_b': ffn_conv_b, 'w_down': w_down,
        'ln_mix_g': ln_mix_g, 'ln_mix_b': ln_mix_b, 'ln_ffn_g': ln_ffn_g, 'ln_ffn_b': ln_ffn_b,
    }
    B = x_prompt.shape[0]
    dt = x_prompt.dtype
    meta = jnp.broadcast_to(meta_tokens.astype(dt)[None], (B, N_META, D_MODEL))
    xp = jnp.concatenate([meta, x_prompt], axis=1)
    zeros_like_state = lambda s: jnp.zeros((s.shape[0], B) + s.shape[2:], dt)
    (hp, p_gla, p_rwkv, p_shift, p_gdn, p_gdn_conv, p_ffn_conv) = _trunk(
        xp, zeros_like_state(state_gla), zeros_like_state(state_rwkv), zeros_like_state(state_rwkv_shift),
        zeros_like_state(state_gdn), zeros_like_state(state_gdn_conv), zeros_like_state(state_ffn_conv),
        p, N_META)
    y_prompt = hp[:, N_META:]
    (y_sample, s_gla, s_rwkv, s_shift, s_gdn, s_gdn_conv, s_ffn_conv) = _trunk(
        x_sample, state_gla, state_rwkv, state_rwkv_shift, state_gdn, state_gdn_conv, state_ffn_conv, p, 0)
    return (y_prompt, y_sample, p_gla, p_rwkv, p_shift, p_gdn, p_gdn_conv, p_ffn_conv,
            s_gla, s_rwkv, s_shift, s_gdn, s_gdn_conv, s_ffn_conv)
```

```python
import functools
import math

import jax
import jax.numpy as jnp
from jax import lax
from jax.experimental import pallas as pl
from jax.experimental.pallas import tpu as pltpu

F32 = jnp.float32
BF16 = jnp.bfloat16
HIGHEST = lax.Precision.HIGHEST

N_META = 16
GLA_HEADS, GLA_HK, GLA_HV = 4, 64, 128
GLA_DK, GLA_DV = GLA_HEADS * GLA_HK, GLA_HEADS * GLA_HV
GLA_GATE_RANK = 16
GLA_GATE_NORM = 16.0
GLA_SUBCHUNK = 16
RWKV_HEADS, RWKV_HEAD = 8, 64
RWKV_DIM = RWKV_HEADS * RWKV_HEAD
RWKV_COLS = 3 * RWKV_DIM + 64 + 64 + 128
RWKV_GN_EPS = 64e-5
GDN_HEADS, GDN_HEAD = 8, 128
GDN_DIM = GDN_HEADS * GDN_HEAD
GDN_CONV = 4
FFN_CONV = 3
LN_EPS = 1e-5
NORM_EPS = 1e-6
DEPTH = 4
DEEPNORM_ALPHA = (2.0 * DEPTH) ** 0.25

LANE = 128
SUBLANE = 8
PROMPT_CHUNK = 64
VMEM_LIMIT = 56 * 1024 * 1024

GLA_Z = 3 * 512 + LANE
GDN_Z = 4 * GDN_DIM + LANE


def _params(n_axes):
    return pltpu.CompilerParams(dimension_semantics=("arbitrary",) * n_axes,
                                vmem_limit_bytes=VMEM_LIMIT)


def _row_tile(rows, cap):
    t = cap
    while rows % t:
        t //= 2
    return t


def _dot(a, b, precision=HIGHEST):
    return jnp.dot(a, b, preferred_element_type=F32, precision=precision)


def _dot_nt(a, b, precision=HIGHEST):
    return lax.dot_general(a, b, (((1,), (1,)), ((), ())), preferred_element_type=F32, precision=precision)


def _dot_tn(a, b, precision=HIGHEST):
    return lax.dot_general(a, b, (((0,), (0,)), ((), ())), preferred_element_type=F32, precision=precision)


def _bdot(a, w):
    return jnp.dot(a.astype(BF16), w, preferred_element_type=F32)


def _iota2(shape, dim):
    return lax.broadcasted_iota(jnp.int32, shape, dim)


def _tri(n, strict=False):
    r, c = _iota2((n, n), 0), _iota2((n, n), 1)
    return (c < r) if strict else (c <= r)


def _softplus(x):
    return jnp.maximum(x, 0.0) + jnp.log1p(jnp.exp(-jnp.abs(x)))


def _sigmoid(x):
    return 1.0 / (1.0 + jnp.exp(-x))


def _silu(x):
    return x * _sigmoid(x)


def _unit_lower_inverse(m):
    n = m.shape[0]
    eye = (_iota2((n, n), 0) == _iota2((n, n), 1)).astype(F32)
    x = eye + m
    p = m
    for _ in range(int(math.log2(n)) - 1):
        p = _dot(p, p)
        x = x + _dot(x, p)
    return x


def _shift_rows(cur, carry, k):
    rows = cur.shape[0]
    out = pltpu.roll(cur, k, 0)
    r = _iota2((rows, 1), 0)
    for j in range(k):
        out = jnp.where(r == j, carry[SUBLANE - k + j:SUBLANE - k + j + 1, :], out)
    return out


def _valid_rows(row0, rows, seq_len, n_pad):
    if seq_len & (seq_len - 1) == 0:
        t = (row0 + _iota2((rows, 1), 0)) & (seq_len - 1)
    else:
        assert rows <= seq_len
        t = lax.rem(row0, seq_len) + _iota2((rows, 1), 0)
        t = jnp.where(t >= seq_len, t - seq_len, t)
    return t >= n_pad


def _layer_norm_rows(h, g, b):
    mu = jnp.mean(h, axis=-1, keepdims=True)
    d = h - mu
    var = jnp.mean(d * d, axis=-1, keepdims=True)
    return d * lax.rsqrt(var + LN_EPS) * g + b


def _inproj_kernel(x_ref, w_ref, *out_refs, widths):
    z = _bdot(x_ref[...], w_ref[...])
    off = 0
    for o_ref, wd in zip(out_refs, widths):
        o_ref[...] = z[:, off:off + wd]
        off += wd


def _inproj(x, w, widths):
    rows, d = x.shape
    n = w.shape[1]
    tm = _row_tile(rows, 256)
    return pl.pallas_call(
        functools.partial(_inproj_kernel, widths=widths),
        grid=(rows // tm,),
        in_specs=[pl.BlockSpec((tm, d), lambda i: (i, 0)),
                  pl.BlockSpec((d, n), lambda i: (0, 0))],
        out_specs=[pl.BlockSpec((tm, wd), lambda i: (i, 0)) for wd in widths],
        out_shape=[jax.ShapeDtypeStruct((rows, wd), F32) for wd in widths],
        compiler_params=_params(1),
        name="inproj",
    )(x, w)


def _outln_kernel(o_ref, x_ref, w_ref, g_ref, b_ref, y_ref, *, seq_len, n_pad):
    tm = x_ref.shape[0]
    h = DEEPNORM_ALPHA * x_ref[...] + _bdot(o_ref[...], w_ref[...])
    y = _layer_norm_rows(h, g_ref[...], b_ref[...])
    valid = _valid_rows(pl.program_id(0) * tm, tm, seq_len, n_pad)
    y_ref[...] = jnp.where(valid, y, 0.0)


def _outproj_ln(o, x, w, g, b, seq_len, n_pad):
    rows, d = x.shape
    k = o.shape[1]
    tm = _row_tile(rows, 512)
    return pl.pallas_call(
        functools.partial(_outln_kernel, seq_len=seq_len, n_pad=n_pad),
        grid=(rows // tm,),
        in_specs=[pl.BlockSpec((tm, k), lambda i: (i, 0)),
                  pl.BlockSpec((tm, d), lambda i: (i, 0)),
                  pl.BlockSpec((k, d), lambda i: (0, 0)),
                  pl.BlockSpec((1, d), lambda i: (0, 0)),
                  pl.BlockSpec((1, d), lambda i: (0, 0))],
        out_specs=pl.BlockSpec((tm, d), lambda i: (i, 0)),
        out_shape=jax.ShapeDtypeStruct((rows, d), F32),
        compiler_params=_params(1),
        name="outproj_ln",
    )(o, x, w, g.reshape(1, d), b.reshape(1, d))


def _ffn_kernel(*refs, seq_len, n_pad, d_ff, ff_chunk, has_fix, full_gate):
    if has_fix:
        x_ref, wup_ref, wdn_ref, cw_ref, cb_ref, g_ref, b_ref, fix_ref, y_ref, gate_ref, carry_ref = refs
    else:
        x_ref, wup_ref, wdn_ref, cw_ref, cb_ref, g_ref, b_ref, y_ref, gate_ref, carry_ref = refs
    i = pl.program_id(0)
    tm = x_ref.shape[0]

    @pl.when(i == 0)
    def _():
        carry_ref[...] = jnp.zeros_like(carry_ref)

    x = x_ref[...]
    xb = x.astype(BF16)
    acc = jnp.zeros(x.shape, F32)
    for c in range(d_ff // ff_chunk):
        lo = c * ff_chunk
        gate = jnp.dot(xb, wup_ref[:, lo:lo + ff_chunk], preferred_element_type=F32)
        up = jnp.dot(xb, wup_ref[:, d_ff + lo:d_ff + lo + ff_chunk], preferred_element_type=F32)
        if has_fix:
            gate = gate + fix_ref[:, lo:lo + ff_chunk]
        carry = carry_ref[:, lo:lo + ff_chunk]
        conv = (cw_ref[0:1, lo:lo + ff_chunk] * _shift_rows(gate, carry, 2)
                + cw_ref[1:2, lo:lo + ff_chunk] * _shift_rows(gate, carry, 1)
                + cw_ref[2:3, lo:lo + ff_chunk] * gate
                + cb_ref[:, lo:lo + ff_chunk])
        act = _silu(conv) * up
        acc = acc + jnp.dot(act.astype(BF16), wdn_ref[lo:lo + ff_chunk, :], preferred_element_type=F32)
        carry_ref[:, lo:lo + ff_chunk] = gate[tm - SUBLANE:, :]
        if full_gate:
            gate_ref[:, lo:lo + ff_chunk] = gate
        else:
            gate_ref[0, :, lo:lo + ff_chunk] = gate[tm - SUBLANE:, :]
    y = _layer_norm_rows(DEEPNORM_ALPHA * x + acc, g_ref[...], b_ref[...])
    valid = _valid_rows(i * tm, tm, seq_len, n_pad)
    y_ref[...] = jnp.where(valid, y, 0.0)


def _conv_ffn_ln(x, w_up, w_down, conv_w, conv_b, g, b, fix, seq_len, n_pad, tm, full_gate):
    rows, d = x.shape
    d_ff = w_down.shape[0]
    ff_chunk = d_ff // 2
    nt = rows // tm
    in_specs = [pl.BlockSpec((tm, d), lambda i: (i, 0)),
                pl.BlockSpec((d, 2 * d_ff), lambda i: (0, 0)),
                pl.BlockSpec((d_ff, d), lambda i: (0, 0)),
                pl.BlockSpec((FFN_CONV, d_ff), lambda i: (0, 0)),
                pl.BlockSpec((1, d_ff), lambda i: (0, 0)),
                pl.BlockSpec((1, d), lambda i: (0, 0)),
                pl.BlockSpec((1, d), lambda i: (0, 0))]
    args = [x, w_up, w_down, conv_w, conv_b.reshape(1, d_ff), g.reshape(1, d), b.reshape(1, d)]
    if fix is not None:
        in_specs.append(pl.BlockSpec((tm, d_ff), lambda i: (i, 0)))
        args.append(fix)
    if full_gate:
        gate_spec = pl.BlockSpec((tm, d_ff), lambda i: (i, 0))
        gate_shape = jax.ShapeDtypeStruct((rows, d_ff), F32)
    else:
        gate_spec = pl.BlockSpec((1, SUBLANE, d_ff), lambda i: (i, 0, 0))
        gate_shape = jax.ShapeDtypeStruct((nt, SUBLANE, d_ff), F32)
    return pl.pallas_call(
        functools.partial(_ffn_kernel, seq_len=seq_len, n_pad=n_pad, d_ff=d_ff, ff_chunk=ff_chunk,
                          has_fix=fix is not None, full_gate=full_gate),
        grid=(nt,),
        in_specs=in_specs,
        out_specs=[pl.BlockSpec((tm, d), lambda i: (i, 0)), gate_spec],
        out_shape=[jax.ShapeDtypeStruct((rows, d), F32), gate_shape],
        scratch_shapes=[pltpu.VMEM((SUBLANE, d_ff), F32)],
        compiler_params=_params(1),
        name="conv_ffn_ln",
    )(*args)


def _gla_kernel(qk_ref, v_ref, og_ref, glo_ref, w2_ref, gb_ref, ng_ref, s0_ref, o_ref, s_ref, st_ref, *, n_pad, sub):
    c = pl.program_id(1)
    rows = qk_ref.shape[0]

    @pl.when(c == 0)
    def _():
        st_ref[...] = s0_ref[0]

    valid = (c * rows + _iota2((rows, 1), 0)) >= n_pad
    pre = _dot(glo_ref[...], w2_ref[...]) + gb_ref[...]
    log_a = -_softplus(-pre) * (1.0 / GLA_GATE_NORM)
    log_a = jnp.where(valid, log_a, 0.0)
    q = qk_ref[:, :GLA_DK] * (GLA_HK ** -0.5)
    k = jnp.where(valid, qk_ref[:, GLA_DK:], 0.0)
    v = v_ref[...]
    causal = _tri(sub)
    tri_f = causal.astype(F32)
    ones = jnp.ones((sub, GLA_HV), F32)
    outs = []
    for j in range(rows // sub):
        r0 = j * sub
        g = log_a[r0:r0 + sub]
        b = _dot(tri_f, g)
        qd = q[r0:r0 + sub] * jnp.exp(b)
        kd = k[r0:r0 + sub] * jnp.exp(-b)
        kend = k[r0:r0 + sub] * jnp.exp(b[sub - 1:sub, :] - b)
        o_heads = []
        for h in range(GLA_HEADS):
            ks = slice(h * GLA_HK, (h + 1) * GLA_HK)
            vh = v[r0:r0 + sub, h * GLA_HV:(h + 1) * GLA_HV]
            s = st_ref[h]
            att = jnp.where(causal, _dot_nt(qd[:, ks], kd[:, ks]), 0.0)
            o_heads.append(_dot(qd[:, ks], s) + _dot(att, vh))
            b_end = _dot_tn(g[:, ks], ones)
            st_ref[h] = s * jnp.exp(b_end) + _dot_tn(kend[:, ks], vh)
        outs.append(o_heads)
    og = og_ref[...]
    for h in range(GLA_HEADS):
        vs = slice(h * GLA_HV, (h + 1) * GLA_HV)
        o = jnp.concatenate([outs[j][h] for j in range(rows // sub)], axis=0) if rows // sub > 1 else outs[0][h]
        o = o * lax.rsqrt(jnp.mean(o * o, axis=-1, keepdims=True) + NORM_EPS) * ng_ref[:, vs]
        o_ref[:, vs] = o * _silu(og[:, vs])

    @pl.when(c == pl.num_programs(1) - 1)
    def _():
        s_ref[0] = st_ref[...]


def _gla(z, w2, gate_b, norm_g, s0, batch, seq_len, n_pad, chunk):
    nc = seq_len // chunk
    sub = math.gcd(chunk, GLA_SUBCHUNK)
    row = lambda blk: (lambda b, c: (b * nc + c, blk))
    const = lambda b, c: (0, 0)
    st_map = lambda b, c: (b, 0, 0, 0)
    w2p = jnp.zeros((LANE, GLA_DK), F32).at[:GLA_GATE_RANK].set(w2)
    return pl.pallas_call(
        functools.partial(_gla_kernel, n_pad=n_pad, sub=sub),
        grid=(batch, nc),
        in_specs=[pl.BlockSpec((chunk, 2 * GLA_DK), row(0)),
                  pl.BlockSpec((chunk, GLA_DV), row(1)),
                  pl.BlockSpec((chunk, GLA_DV), row(2)),
                  pl.BlockSpec((chunk, LANE), row(3 * 512 // LANE)),
                  pl.BlockSpec((LANE, GLA_DK), const),
                  pl.BlockSpec((1, GLA_DK), const),
                  pl.BlockSpec((1, GLA_DV), const),
                  pl.BlockSpec((1, GLA_HEADS, GLA_HK, GLA_HV), st_map)],
        out_specs=[pl.BlockSpec((chunk, GLA_DV), row(0)),
                   pl.BlockSpec((1, GLA_HEADS, GLA_HK, GLA_HV), st_map)],
        out_shape=[jax.ShapeDtypeStruct((batch * seq_len, GLA_DV), F32),
                   jax.ShapeDtypeStruct((batch, GLA_HEADS, GLA_HK, GLA_HV), F32)],
        scratch_shapes=[pltpu.VMEM((GLA_HEADS, GLA_HK, GLA_HV), F32)],
        compiler_params=_params(2),
        name="gla",
    )(z, z, z, z, w2p, gate_b.reshape(1, GLA_DK), norm_g.reshape(1, GLA_DV), s0)


def _rwkv_kernel(z_ref, mu_ref, w0_ref, w2_ref, a0_ref, a2_ref, g2_ref, kk_ref, ka_ref, rk_ref, lg_ref, lb_ref,
                 bd_ref, s0_ref, y_ref, s_ref, st_ref, carry_ref, *, n_pad):
    c = pl.program_id(1)
    rows = z_ref.shape[0]
    dim = RWKV_DIM

    @pl.when(c == 0)
    def _():
        st_ref[...] = s0_ref[0]
        carry_ref[...] = jnp.zeros_like(carry_ref)

    valid = (c * rows + _iota2((rows, 1), 0)) >= n_pad
    z = z_ref[...]
    z_prev = _shift_rows(z, carry_ref[...], 1)
    carry_ref[...] = z[rows - SUBLANE:, :]
    z = z + mu_ref[...] * (z_prev - z)
    r, kr, vr = z[:, :dim], z[:, dim:2 * dim], z[:, 2 * dim:3 * dim]
    w_lo = z[:, 3 * dim:3 * dim + 64]
    a_lo = z[:, 3 * dim + 64:3 * dim + 128]
    g_lo = z[:, 3 * dim + 128:]
    bd = bd_ref[...]
    w_raw = w0_ref[...] + _dot(jnp.tanh(w_lo), w2_ref[...])
    log_w = -jnp.exp(-_softplus(-w_raw) - 0.5)
    a_lr = _sigmoid(a0_ref[...] + _dot(a_lo, a2_ref[...]))
    gate = _dot(_sigmoid(g_lo), g2_ref[...])
    kx = kr * kk_ref[...]
    kk = kx * lax.rsqrt(_dot(kx * kx, bd) + NORM_EPS)
    k_mod = kr * (1.0 + (a_lr - 1.0) * ka_ref[...])
    bonus = _dot(r * k_mod * rk_ref[...], bd) * vr

    log_w = jnp.where(valid, log_w, 0.0)
    k_in = jnp.where(valid, k_mod, 0.0)
    b_in = jnp.where(valid, kk * a_lr, 0.0)
    incl = _tri(rows)
    strict = _tri(rows, strict=True)
    c_incl = _dot(incl.astype(F32), log_w)
    c_end = c_incl[rows - 1:rows, :]
    e_neg = jnp.exp(-c_incl)
    e_end = jnp.exp(c_end - c_incl)
    a_t = -kk * jnp.exp(c_incl - log_w)
    r_t = r * jnp.exp(c_incl)
    b_t, k_t = b_in * e_neg, k_in * e_neg
    b_e, k_e = b_in * e_end, k_in * e_end
    w_end = jnp.exp(c_end)
    ys = []
    for h in range(RWKV_HEADS):
        hs = slice(h * RWKV_HEAD, (h + 1) * RWKV_HEAD)
        s = st_ref[h]
        vh = vr[:, hs]
        m_ab = jnp.where(strict, _dot_nt(a_t[:, hs], b_t[:, hs]), 0.0)
        m_ak = jnp.where(strict, _dot_nt(a_t[:, hs], k_t[:, hs]), 0.0)
        m_rb = jnp.where(incl, _dot_nt(r_t[:, hs], b_t[:, hs]), 0.0)
        m_rk = jnp.where(incl, _dot_nt(r_t[:, hs], k_t[:, hs]), 0.0)
        rhs = _dot_nt(a_t[:, hs], s) + _dot(m_ak, vh)
        u = _dot(_unit_lower_inverse(m_ab), rhs)
        ys.append(_dot_nt(r_t[:, hs], s) + _dot(m_rb, u) + _dot(m_rk, vh))
        st_ref[h] = s * w_end[:, hs] + _dot_tn(u, b_e[:, hs]) + _dot_tn(vh, k_e[:, hs])
    y = jnp.concatenate(ys, axis=1)
    mean = _dot(y, bd) * (1.0 / RWKV_HEAD)
    d = y - mean
    var = _dot(d * d, bd) * (1.0 / RWKV_HEAD)
    y = d * lax.rsqrt(var + RWKV_GN_EPS) * lg_ref[...] + lb_ref[...]
    y_ref[...] = (y + bonus) * gate

    @pl.when(c == pl.num_programs(1) - 1)
    def _():
        s_ref[0] = st_ref[...]


def _rwkv(z, p, s0, batch, seq_len, n_pad, chunk):
    nc = seq_len // chunk
    dim = RWKV_DIM
    row = lambda b, c: (b * nc + c, 0)
    const = lambda b, c: (0, 0)
    st_map = lambda b, c: (b, 0, 0, 0)
    vec = lambda a: a.reshape(1, -1)
    head = jnp.arange(dim) // RWKV_HEAD
    bd = (head[:, None] == head[None, :]).astype(F32)
    small = [vec(p['mu']), vec(p['w0']), p['w2'], vec(p['a0']), p['a2'], p['g2'], vec(p['k_k']), vec(p['k_a']),
             vec(p['r_k']), vec(p['ln_g']), vec(p['ln_b']), bd]
    return pl.pallas_call(
        functools.partial(_rwkv_kernel, n_pad=n_pad),
        grid=(batch, nc),
        in_specs=([pl.BlockSpec((chunk, RWKV_COLS), row)]
                  + [pl.BlockSpec(a.shape, const) for a in small]
                  + [pl.BlockSpec((1, RWKV_HEADS, RWKV_HEAD, RWKV_HEAD), st_map)]),
        out_specs=[pl.BlockSpec((chunk, dim), row),
                   pl.BlockSpec((1, RWKV_HEADS, RWKV_HEAD, RWKV_HEAD), st_map)],
        out_shape=[jax.ShapeDtypeStruct((batch * seq_len, dim), F32),
                   jax.ShapeDtypeStruct((batch, RWKV_HEADS, RWKV_HEAD, RWKV_HEAD), F32)],
        scratch_shapes=[pltpu.VMEM((RWKV_HEADS, RWKV_HEAD, RWKV_HEAD), F32),
                        pltpu.VMEM((SUBLANE, RWKV_COLS), F32)],
        compiler_params=_params(2),
        name="rwkv7",
    )(z, *small, s0)


def _gdn_kernel(*refs, n_pad, has_fix):
    if has_fix:
        z_ref, cw_ref, alog_ref, dt_ref, ng_ref, s0_ref, fix_ref, o_ref, tail_ref, s_ref, st_ref, carry_ref = refs
    else:
        z_ref, cw_ref, alog_ref, dt_ref, ng_ref, s0_ref, o_ref, tail_ref, s_ref, st_ref, carry_ref = refs
    c = pl.program_id(1)
    rows = z_ref.shape[0]
    dim = GDN_DIM

    @pl.when(c == 0)
    def _():
        st_ref[...] = s0_ref[0]
        carry_ref[...] = jnp.zeros_like(carry_ref)

    valid = (c * rows + _iota2((rows, 1), 0)) >= n_pad
    pre = z_ref[:, :3 * dim]
    if has_fix:
        pre = pre + fix_ref[...]
    carry = carry_ref[...]
    conv = cw_ref[3:4, :] * pre
    for j in range(GDN_CONV - 1):
        conv = conv + cw_ref[j:j + 1, :] * _shift_rows(pre, carry, GDN_CONV - 1 - j)
    tail = pre[rows - SUBLANE:, :]
    carry_ref[...] = tail
    qkv = _silu(conv)
    zg = z_ref[:, 3 * dim:4 * dim]
    lo = z_ref[:, 4 * dim:]
    beta_all = _sigmoid(lo)
    g_all = -jnp.exp(alog_ref[...]) * _softplus(lo + dt_ref[...])
    g_all = jnp.where(valid, g_all, 0.0)

    incl = _tri(rows)
    strict = _tri(rows, strict=True)
    incl_f = incl.astype(F32)
    upper_f = (_iota2((rows, rows), 0) <= _iota2((rows, rows), 1)).astype(F32)
    ones_f = jnp.ones((rows, rows), F32)
    for h in range(GDN_HEADS):
        hs = slice(h * GDN_HEAD, (h + 1) * GDN_HEAD)
        q = qkv[:, hs]
        k = qkv[:, dim + h * GDN_HEAD:dim + (h + 1) * GDN_HEAD]
        v = qkv[:, 2 * dim + h * GDN_HEAD:2 * dim + (h + 1) * GDN_HEAD]
        q = q * lax.rsqrt(jnp.sum(q * q, axis=-1, keepdims=True) + NORM_EPS) * (GDN_HEAD ** -0.5)
        k = k * lax.rsqrt(jnp.sum(k * k, axis=-1, keepdims=True) + NORM_EPS)
        k = jnp.where(valid, k, 0.0)
        beta = beta_all[:, h:h + 1]
        g_b = jnp.broadcast_to(g_all[:, GDN_HEADS + h:GDN_HEADS + h + 1], (rows, rows))
        gam_rows = _dot(incl_f, g_b)
        gam_cols = _dot(ones_f, g_b * upper_f)
        dec = jnp.where(incl, jnp.exp(jnp.where(incl, gam_rows - gam_cols, 0.0)), 0.0)
        gam = gam_rows[:, 0:1]
        g_end = gam_rows[rows - 1:rows, 0:1]
        s = st_ref[h]
        m = jnp.where(strict, _dot_nt(k, k) * dec, 0.0) * (-beta)
        t_inv = _unit_lower_inverse(m)
        u = _dot(t_inv, v * beta)
        w = _dot(t_inv, k * (beta * jnp.exp(gam)))
        delta = u - _dot(w, s)
        qk = _dot_nt(q, k) * dec
        o = _dot(q * jnp.exp(gam), s) + _dot(qk, delta)
        st_ref[h] = s * jnp.exp(g_end) + _dot_tn(k * jnp.exp(g_end - gam), delta)
        o = o * lax.rsqrt(jnp.mean(o * o, axis=-1, keepdims=True) + NORM_EPS) * ng_ref[...]
        o_ref[:, hs] = o * _silu(zg[:, hs])

    @pl.when(c == pl.num_programs(1) - 1)
    def _():
        s_ref[0] = st_ref[...]
        tail_ref[0] = tail


def _gdn(z, conv_w, a_log, dt_bias, norm_g, s0, fix, batch, seq_len, n_pad, chunk):
    nc = seq_len // chunk
    dim = GDN_DIM
    row = lambda b, c: (b * nc + c, 0)
    const = lambda b, c: (0, 0)
    st_map = lambda b, c: (b, 0, 0, 0)
    pad_lo = lambda a: jnp.zeros((1, LANE), F32).at[0, GDN_HEADS:2 * GDN_HEADS].set(a)
    in_specs = [pl.BlockSpec((chunk, GDN_Z), row),
                pl.BlockSpec((GDN_CONV, 3 * dim), const),
                pl.BlockSpec((1, LANE), const),
                pl.BlockSpec((1, LANE), const),
                pl.BlockSpec((1, GDN_HEAD), const),
                pl.BlockSpec((1, GDN_HEADS, GDN_HEAD, GDN_HEAD), st_map)]
    args = [z, conv_w, pad_lo(a_log), pad_lo(dt_bias), norm_g.reshape(1, GDN_HEAD), s0]
    if fix is not None:
        in_specs.append(pl.BlockSpec((chunk, 3 * dim), row))
        args.append(fix)
    return pl.pallas_call(
        functools.partial(_gdn_kernel, n_pad=n_pad, has_fix=fix is not None),
        grid=(batch, nc),
        in_specs=in_specs,
        out_specs=[pl.BlockSpec((chunk, dim), row),
                   pl.BlockSpec((1, SUBLANE, 3 * dim), lambda b, c: (b, 0, 0)),
                   pl.BlockSpec((1, GDN_HEADS, GDN_HEAD, GDN_HEAD), st_map)],
        out_shape=[jax.ShapeDtypeStruct((batch * seq_len, dim), F32),
                   jax.ShapeDtypeStruct((batch, SUBLANE, 3 * dim), F32),
                   jax.ShapeDtypeStruct((batch, GDN_HEADS, GDN_HEAD, GDN_HEAD), F32)],
        scratch_shapes=[pltpu.VMEM((GDN_HEADS, GDN_HEAD, GDN_HEAD), F32),
                        pltpu.VMEM((SUBLANE, 3 * dim), F32)],
        compiler_params=_params(2),
        name="gated_deltanet",
    )(*args)


def _pad_rows_fix(buf, seq_len, n_pad):
    b, w, c = buf.shape
    return jnp.zeros((b, seq_len, c), F32).at[:, n_pad - w:n_pad].set(buf).reshape(b * seq_len, c)


def _trunk(x, states, wts, batch, seq_len, n_pad, chunk, carried):
    st_gla, st_rwkv, st_shift, st_gdn, st_gdn_conv, st_ffn_conv = states
    d = x.shape[1]
    rows = batch * seq_len
    if carried:
        ffn_tm = _row_tile(rows, 512)
    else:
        ffn_tm = next(t for t in (352, 192, 64) if seq_len % t == 0)
    new = {k: [] for k in ('gla', 'rwkv', 'shift', 'gdn', 'gdn_conv', 'ffn_conv')}
    for l in range(DEPTH):
        i = l // 2
        x3 = x.reshape(batch, seq_len, d)
        if l % 2 == 0:
            new['shift'].append(x3[:, -1])
            x_in = x
            if carried:
                x_in = x3.at[:, n_pad - 1].set(st_shift[i]).reshape(rows, d)
            z_gla, z_rwkv = _inproj(x_in, wts['w_in_ab'][i], (GLA_Z, RWKV_COLS))
            o_gla, s_gla = _gla(z_gla, wts['gla_gate_w2'][i], wts['gla_gate_b'][i], wts['gla_norm_g'][i],
                                st_gla[i], batch, seq_len, n_pad, chunk)
            y_rwkv, s_rwkv = _rwkv(z_rwkv, {k: wts['rwkv_' + k][i] for k in
                                            ('mu', 'w0', 'w2', 'a0', 'a2', 'g2', 'k_k', 'k_a', 'r_k', 'ln_g', 'ln_b')},
                                   st_rwkv[i], batch, seq_len, n_pad, chunk)
            new['gla'].append(s_gla)
            new['rwkv'].append(s_rwkv)
            o = jnp.concatenate([o_gla, y_rwkv], axis=1)
            w_out = wts['w_out_ab'][i]
        else:
            (z,) = _inproj(x, wts['w_in_c'][i], (GDN_Z,))
            fix = _pad_rows_fix(st_gdn_conv[i], seq_len, n_pad) if carried else None
            o, tail, s_gdn = _gdn(z, wts['gdn_conv_w'][i], wts['gdn_A_log'][i], wts['gdn_dt_bias'][i],
                                  wts['gdn_norm_g'][i], st_gdn[i], fix, batch, seq_len, n_pad, chunk)
            new['gdn'].append(s_gdn)
            new['gdn_conv'].append(tail[:, SUBLANE - (GDN_CONV - 1):])
            w_out = wts['w_out_c'][i]
        x = _outproj_ln(o, x, w_out, wts['ln_mix_g'][l], wts['ln_mix_b'][l], seq_len, n_pad)
        fix = _pad_rows_fix(st_ffn_conv[l], seq_len, n_pad) if carried else None
        x, gate = _conv_ffn_ln(x, wts['w_up'][l], wts['w_down'][l], wts['ffn_conv_w'][l], wts['ffn_conv_b'][l],
                               wts['ln_ffn_g'][l], wts['ln_ffn_b'][l], fix, seq_len, n_pad, ffn_tm, carried)
        if carried:
            new['ffn_conv'].append(gate.reshape(batch, seq_len, -1)[:, seq_len - (FFN_CONV - 1):])
        else:
            per_seq = seq_len // ffn_tm
            new['ffn_conv'].append(gate.reshape(batch, per_seq, SUBLANE, -1)[:, -1, SUBLANE - (FFN_CONV - 1):])
    return (x,) + tuple(jnp.stack(new[k]) for k in ('gla', 'rwkv', 'shift', 'gdn', 'gdn_conv', 'ffn_conv'))


def kernel(x_prompt, x_sample, state_gla, state_rwkv, state_rwkv_shift, state_gdn, state_gdn_conv, state_ffn_conv, meta_tokens, w_in_ab, gla_gate_w2, gla_gate_b, gla_norm_g, rwkv_mu, rwkv_w0, rwkv_w2, rwkv_a0, rwkv_a2, rwkv_g2, rwkv_k_k, rwkv_k_a, rwkv_r_k, rwkv_ln_g, rwkv_ln_b, w_out_ab, w_in_c, gdn_conv_w, gdn_A_log, gdn_dt_bias, gdn_norm_g, w_out_c, w_up, ffn_conv_w, ffn_conv_b, w_down, ln_mix_g, ln_mix_b, ln_ffn_g, ln_ffn_b):
    d = x_prompt.shape[-1]
    n_ab, n_c = w_in_ab.shape[0], w_in_c.shape[0]
    gla_cols = 2 * GLA_DK + 2 * GLA_DV + GLA_GATE_RANK
    lo0 = 2 * GLA_DK + GLA_DV
    w_ab = jnp.concatenate([w_in_ab[:, :, :lo0], w_in_ab[:, :, lo0 + GLA_GATE_RANK:gla_cols],
                            w_in_ab[:, :, lo0:lo0 + GLA_GATE_RANK],
                            jnp.zeros((n_ab, d, LANE - GLA_GATE_RANK), w_in_ab.dtype),
                            w_in_ab[:, :, gla_cols:]], axis=2).astype(BF16)
    w_c = jnp.concatenate([w_in_c, jnp.zeros((n_c, d, LANE - 2 * GDN_HEADS), w_in_c.dtype)], axis=2).astype(BF16)
    wts = {
        'w_in_ab': w_ab, 'gla_gate_w2': gla_gate_w2, 'gla_gate_b': gla_gate_b, 'gla_norm_g': gla_norm_g,
        'rwkv_mu': rwkv_mu, 'rwkv_w0': rwkv_w0, 'rwkv_w2': rwkv_w2, 'rwkv_a0': rwkv_a0, 'rwkv_a2': rwkv_a2,
        'rwkv_g2': rwkv_g2, 'rwkv_k_k': rwkv_k_k, 'rwkv_k_a': rwkv_k_a, 'rwkv_r_k': rwkv_r_k,
        'rwkv_ln_g': rwkv_ln_g, 'rwkv_ln_b': rwkv_ln_b, 'w_out_ab': w_out_ab.astype(BF16),
        'w_in_c': w_c, 'gdn_conv_w': gdn_conv_w, 'gdn_A_log': gdn_A_log, 'gdn_dt_bias': gdn_dt_bias,
        'gdn_norm_g': gdn_norm_g, 'w_out_c': w_out_c.astype(BF16),
        'w_up': w_up.astype(BF16), 'ffn_conv_w': ffn_conv_w, 'ffn_conv_b': ffn_conv_b,
        'w_down': w_down.astype(BF16),
        'ln_mix_g': ln_mix_g, 'ln_mix_b': ln_mix_b, 'ln_ffn_g': ln_ffn_g, 'ln_ffn_b': ln_ffn_b,
    }

    bp, seq, _ = x_prompt.shape
    tp = -(-(N_META + seq + SUBLANE) // PROMPT_CHUNK) * PROMPT_CHUNK
    pad_p = tp - N_META - seq
    meta = jnp.broadcast_to(meta_tokens.astype(F32)[None], (bp, N_META, d))
    xp = jnp.concatenate([jnp.zeros((bp, pad_p, d), F32), meta, x_prompt], axis=1).reshape(bp * tp, d)
    zero_state = lambda s: jnp.zeros((s.shape[0], bp) + s.shape[2:], F32)
    p_out = _trunk(xp, tuple(zero_state(s) for s in (state_gla, state_rwkv, state_rwkv_shift, state_gdn,
                                                      state_gdn_conv, state_ffn_conv)),
                   wts, bp, tp, pad_p, PROMPT_CHUNK, carried=False)
    y_prompt = p_out[0].reshape(bp, tp, d)[:, pad_p + N_META:]

    bs, ts, _ = x_sample.shape
    tsp = -(-(ts + GDN_CONV - 1) // SUBLANE) * SUBLANE
    pad_s = tsp - ts
    xs = jnp.concatenate([jnp.zeros((bs, pad_s, d), F32), x_sample], axis=1).reshape(bs * tsp, d)
    s_out = _trunk(xs, (state_gla, state_rwkv, state_rwkv_shift, state_gdn, state_gdn_conv, state_ffn_conv),
                   wts, bs, tsp, pad_s, tsp, carried=True)
    y_sample = s_out[0].reshape(bs, tsp, d)[:, pad_s:]
    return (y_prompt, y_sample) + tuple(p_out[1:]) + tuple(s_out[1:])
```

```python
import functools
import math

import jax
import jax.numpy as jnp
from jax import lax
from jax.experimental import pallas as pl
from jax.experimental.pallas import tpu as pltpu

F32 = jnp.float32
BF16 = jnp.bfloat16
HIGHEST = lax.Precision.HIGHEST

N_META = 16
GLA_HEADS, GLA_HK, GLA_HV = 4, 64, 128
GLA_DK, GLA_DV = GLA_HEADS * GLA_HK, GLA_HEADS * GLA_HV
GLA_GATE_RANK = 16
GLA_GATE_NORM = 16.0
GLA_SUBCHUNK = 16
RWKV_HEADS, RWKV_HEAD = 8, 64
RWKV_DIM = RWKV_HEADS * RWKV_HEAD
RWKV_COLS = 3 * RWKV_DIM + 64 + 64 + 128
RWKV_GN_EPS = 64e-5
GDN_HEADS, GDN_HEAD = 8, 128
GDN_DIM = GDN_HEADS * GDN_HEAD
GDN_CONV = 4
FFN_CONV = 3
LN_EPS = 1e-5
NORM_EPS = 1e-6
DEPTH = 4
DEEPNORM_ALPHA = (2.0 * DEPTH) ** 0.25

LANE = 128
SUBLANE = 8
PROMPT_CHUNK = 64
VMEM_LIMIT = 56 * 1024 * 1024

GLA_Z = 3 * 512 + LANE
GDN_Z = 4 * GDN_DIM + LANE


def _params(n_axes):
    return pltpu.CompilerParams(dimension_semantics=("arbitrary",) * n_axes,
                                vmem_limit_bytes=VMEM_LIMIT)


def _row_tile(rows, cap):
    t = cap
    while rows % t:
        t //= 2
    return t


_DIMS = {'nn': (((1,), (0,)), ((), ())),
         'nt': (((1,), (1,)), ((), ())),
         'tn': (((0,), (0,)), ((), ()))}


def _dg(a, b, kind='nn'):
    return lax.dot_general(a, b, _DIMS[kind], preferred_element_type=F32)


def _split(x, pieces):
    out = []
    for _ in range(pieces - 1):
        hi = x.astype(BF16)
        out.append(hi)
        x = x - hi.astype(F32)
    out.append(x.astype(BF16))
    return out


def _mm(a, b, kind='nn'):
    return _dg(a.astype(BF16), b.astype(BF16), kind)


def _mm3(a, b, kind='nn'):
    ah, al = _split(a, 2)
    bh, bl = _split(b, 2)
    return _dg(ah, bh, kind) + _dg(al, bh, kind) + _dg(ah, bl, kind)


def _mm_const_lhs(c, x, pieces, kind='nn'):
    cb = c.astype(BF16)
    return sum(_dg(cb, p, kind) for p in _split(x, pieces))


def _mm_const_rhs(x, c, pieces, kind='nn'):
    cb = c.astype(BF16)
    return sum(_dg(p, cb, kind) for p in _split(x, pieces))


def _bdot(a, w):
    return jnp.dot(a.astype(BF16), w, preferred_element_type=F32)


def _iota2(shape, dim):
    return lax.broadcasted_iota(jnp.int32, shape, dim)


def _tri(n, strict=False):
    r, c = _iota2((n, n), 0), _iota2((n, n), 1)
    return (c < r) if strict else (c <= r)


def _softplus(x):
    return jnp.maximum(x, 0.0) + jnp.log1p(jnp.exp(-jnp.abs(x)))


def _sigmoid(x):
    return 1.0 / (1.0 + jnp.exp(-x))


def _silu(x):
    return x * _sigmoid(x)


def _unit_lower_inverses(ms):
    n = ms[0].shape[0]
    eye = (_iota2((n, n), 0) == _iota2((n, n), 1)).astype(F32)
    xs = [eye + m for m in ms]
    ps = list(ms)
    for _ in range(int(math.log2(n)) - 1):
        ps = [_mm(p, p) for p in ps]
        xs = [x + _mm(x, p) for x, p in zip(xs, ps)]
    return xs


def _head_sums(x, width, pieces):
    rows, cols = x.shape
    nb = cols // LANE
    shift = width.bit_length() - 1
    grp = (_iota2((LANE, LANE), 0) >> shift == _iota2((LANE, LANE), 1) >> shift).astype(F32)
    xs = jnp.concatenate([x[:, i * LANE:(i + 1) * LANE] for i in range(nb)], axis=0)
    s = _mm_const_rhs(xs, grp, pieces)
    return jnp.concatenate([s[i * rows:(i + 1) * rows] for i in range(nb)], axis=1)


def _shift_rows(cur, carry, k):
    rows = cur.shape[0]
    out = pltpu.roll(cur, k, 0)
    r = _iota2((rows, 1), 0)
    for j in range(k):
        out = jnp.where(r == j, carry[SUBLANE - k + j:SUBLANE - k + j + 1, :], out)
    return out


def _valid_rows(row0, rows, seq_len, n_pad):
    if seq_len & (seq_len - 1) == 0:
        t = (row0 + _iota2((rows, 1), 0)) & (seq_len - 1)
    else:
        assert rows <= seq_len
        t = lax.rem(row0, seq_len) + _iota2((rows, 1), 0)
        t = jnp.where(t >= seq_len, t - seq_len, t)
    return t >= n_pad


def _layer_norm_rows(h, g, b):
    mu = jnp.mean(h, axis=-1, keepdims=True)
    d = h - mu
    var = jnp.mean(d * d, axis=-1, keepdims=True)
    return d * lax.rsqrt(var + LN_EPS) * g + b


def _inproj_kernel(x_ref, w_ref, *out_refs, widths):
    z = _bdot(x_ref[...], w_ref[...])
    off = 0
    for o_ref, wd in zip(out_refs, widths):
        o_ref[...] = z[:, off:off + wd]
        off += wd


def _inproj(x, w, widths):
    rows, d = x.shape
    n = w.shape[1]
    tm = _row_tile(rows, 256)
    return pl.pallas_call(
        functools.partial(_inproj_kernel, widths=widths),
        grid=(rows // tm,),
        in_specs=[pl.BlockSpec((tm, d), lambda i: (i, 0)),
                  pl.BlockSpec((d, n), lambda i: (0, 0))],
        out_specs=[pl.BlockSpec((tm, wd), lambda i: (i, 0)) for wd in widths],
        out_shape=[jax.ShapeDtypeStruct((rows, wd), F32) for wd in widths],
        compiler_params=_params(1),
        name="inproj",
    )(x, w)


def _outln_kernel(o_ref, x_ref, w_ref, g_ref, b_ref, y_ref, *, seq_len, n_pad):
    tm = x_ref.shape[0]
    h = DEEPNORM_ALPHA * x_ref[...] + _bdot(o_ref[...], w_ref[...])
    y = _layer_norm_rows(h, g_ref[...], b_ref[...])
    valid = _valid_rows(pl.program_id(0) * tm, tm, seq_len, n_pad)
    y_ref[...] = jnp.where(valid, y, 0.0)


def _outproj_ln(o, x, w, g, b, seq_len, n_pad):
    rows, d = x.shape
    k = o.shape[1]
    tm = _row_tile(rows, 512)
    return pl.pallas_call(
        functools.partial(_outln_kernel, seq_len=seq_len, n_pad=n_pad),
        grid=(rows // tm,),
        in_specs=[pl.BlockSpec((tm, k), lambda i: (i, 0)),
                  pl.BlockSpec((tm, d), lambda i: (i, 0)),
                  pl.BlockSpec((k, d), lambda i: (0, 0)),
                  pl.BlockSpec((1, d), lambda i: (0, 0)),
                  pl.BlockSpec((1, d), lambda i: (0, 0))],
        out_specs=pl.BlockSpec((tm, d), lambda i: (i, 0)),
        out_shape=jax.ShapeDtypeStruct((rows, d), F32),
        compiler_params=_params(1),
        name="outproj_ln",
    )(o, x, w, g.reshape(1, d), b.reshape(1, d))


def _ffn_kernel(*refs, seq_len, n_pad, d_ff, ff_chunk, has_fix, full_gate):
    if has_fix:
        x_ref, wup_ref, wdn_ref, cw_ref, cb_ref, g_ref, b_ref, fix_ref, y_ref, gate_ref, carry_ref = refs
    else:
        x_ref, wup_ref, wdn_ref, cw_ref, cb_ref, g_ref, b_ref, y_ref, gate_ref, carry_ref = refs
    i = pl.program_id(0)
    tm = x_ref.shape[0]

    @pl.when(i == 0)
    def _():
        carry_ref[...] = jnp.zeros_like(carry_ref)

    x = x_ref[...]
    xb = x.astype(BF16)
    acc = jnp.zeros(x.shape, F32)
    for c in range(d_ff // ff_chunk):
        lo = c * ff_chunk
        gate = jnp.dot(xb, wup_ref[:, lo:lo + ff_chunk], preferred_element_type=F32)
        up = jnp.dot(xb, wup_ref[:, d_ff + lo:d_ff + lo + ff_chunk], preferred_element_type=F32)
        if has_fix:
            gate = gate + fix_ref[:, lo:lo + ff_chunk]
        carry = carry_ref[:, lo:lo + ff_chunk]
        conv = (cw_ref[0:1, lo:lo + ff_chunk] * _shift_rows(gate, carry, 2)
                + cw_ref[1:2, lo:lo + ff_chunk] * _shift_rows(gate, carry, 1)
                + cw_ref[2:3, lo:lo + ff_chunk] * gate
                + cb_ref[:, lo:lo + ff_chunk])
        act = _silu(conv) * up
        acc = acc + jnp.dot(act.astype(BF16), wdn_ref[lo:lo + ff_chunk, :], preferred_element_type=F32)
        carry_ref[:, lo:lo + ff_chunk] = gate[tm - SUBLANE:, :]
        if full_gate:
            gate_ref[:, lo:lo + ff_chunk] = gate
        else:
            gate_ref[0, :, lo:lo + ff_chunk] = gate[tm - SUBLANE:, :]
    y = _layer_norm_rows(DEEPNORM_ALPHA * x + acc, g_ref[...], b_ref[...])
    valid = _valid_rows(i * tm, tm, seq_len, n_pad)
    y_ref[...] = jnp.where(valid, y, 0.0)


def _conv_ffn_ln(x, w_up, w_down, conv_w, conv_b, g, b, fix, seq_len, n_pad, tm, full_gate):
    rows, d = x.shape
    d_ff = w_down.shape[0]
    ff_chunk = d_ff // 2
    nt = rows // tm
    in_specs = [pl.BlockSpec((tm, d), lambda i: (i, 0)),
                pl.BlockSpec((d, 2 * d_ff), lambda i: (0, 0)),
                pl.BlockSpec((d_ff, d), lambda i: (0, 0)),
                pl.BlockSpec((FFN_CONV, d_ff), lambda i: (0, 0)),
                pl.BlockSpec((1, d_ff), lambda i: (0, 0)),
                pl.BlockSpec((1, d), lambda i: (0, 0)),
                pl.BlockSpec((1, d), lambda i: (0, 0))]
    args = [x, w_up, w_down, conv_w, conv_b.reshape(1, d_ff), g.reshape(1, d), b.reshape(1, d)]
    if fix is not None:
        in_specs.append(pl.BlockSpec((tm, d_ff), lambda i: (i, 0)))
        args.append(fix)
    if full_gate:
        gate_spec = pl.BlockSpec((tm, d_ff), lambda i: (i, 0))
        gate_shape = jax.ShapeDtypeStruct((rows, d_ff), F32)
    else:
        gate_spec = pl.BlockSpec((1, SUBLANE, d_ff), lambda i: (i, 0, 0))
        gate_shape = jax.ShapeDtypeStruct((nt, SUBLANE, d_ff), F32)
    return pl.pallas_call(
        functools.partial(_ffn_kernel, seq_len=seq_len, n_pad=n_pad, d_ff=d_ff, ff_chunk=ff_chunk,
                          has_fix=fix is not None, full_gate=full_gate),
        grid=(nt,),
        in_specs=in_specs,
        out_specs=[pl.BlockSpec((tm, d), lambda i: (i, 0)), gate_spec],
        out_shape=[jax.ShapeDtypeStruct((rows, d), F32), gate_shape],
        scratch_shapes=[pltpu.VMEM((SUBLANE, d_ff), F32)],
        compiler_params=_params(1),
        name="conv_ffn_ln",
    )(*args)


def _gla_kernel(qk_ref, v_ref, og_ref, glo_ref, w2_ref, gb_ref, ng_ref, s0_ref, o_ref, s_ref, st_ref, *, n_pad, sub):
    c = pl.program_id(1)
    rows = qk_ref.shape[0]

    @pl.when(c == 0)
    def _():
        st_ref[...] = s0_ref[0]

    heads = range(GLA_HEADS)
    ks = [slice(h * GLA_HK, (h + 1) * GLA_HK) for h in heads]
    vs = [slice(h * GLA_HV, (h + 1) * GLA_HV) for h in heads]
    nsub = rows // sub
    valid = (c * rows + _iota2((rows, 1), 0)) >= n_pad
    pre = _mm3(glo_ref[...], w2_ref[...]) + gb_ref[...]
    log_a = -_softplus(-pre) * (1.0 / GLA_GATE_NORM)
    log_a = jnp.where(valid, log_a, 0.0)
    q = qk_ref[:, :GLA_DK] * (GLA_HK ** -0.5)
    k = jnp.where(valid, qk_ref[:, GLA_DK:], 0.0)
    shift = sub.bit_length() - 1
    r, cc = _iota2((rows, rows), 0), _iota2((rows, rows), 1)
    same = (r >> shift) == (cc >> shift)
    causal = same & (cc <= r)
    b = _mm_const_lhs(causal.astype(F32), log_a, 3)
    b_tot = _mm_const_lhs(same.astype(F32), log_a, 3)
    qd = (q * jnp.exp(b)).astype(BF16)
    kd = (k * jnp.exp(-b)).astype(BF16)
    kend = (k * jnp.exp(b_tot - b)).astype(BF16)
    vb = v_ref[...].astype(BF16)
    att = [jnp.where(causal, _dg(qd[:, ks[h]], kd[:, ks[h]], 'nt'), 0.0).astype(BF16) for h in heads]
    o = [_dg(att[h], vb[:, vs[h]]) for h in heads]
    s = [st_ref[h] for h in heads]
    o_inter = [[] for _ in heads]
    for j in range(nsub):
        rs = slice(j * sub, (j + 1) * sub)
        for h in heads:
            o_inter[h].append(_dg(qd[rs, ks[h]], s[h].astype(BF16), 'nt'))
        dec = jnp.exp(b_tot[j * sub:j * sub + 1, :])
        s = [s[h] * dec[:, ks[h]] + _dg(vb[rs, vs[h]], kend[rs, ks[h]], 'tn') for h in heads]
    og = og_ref[...]
    for h in heads:
        st_ref[h] = s[h]
        oh = o[h] + (jnp.concatenate(o_inter[h], axis=0) if nsub > 1 else o_inter[h][0])
        oh = oh * lax.rsqrt(jnp.mean(oh * oh, axis=-1, keepdims=True) + NORM_EPS) * ng_ref[:, vs[h]]
        o_ref[:, vs[h]] = oh * _silu(og[:, vs[h]])

    @pl.when(c == pl.num_programs(1) - 1)
    def _():
        s_ref[0] = st_ref[...]


def _gla(z, w2, gate_b, norm_g, s0, batch, seq_len, n_pad, chunk):
    nc = seq_len // chunk
    sub = math.gcd(chunk, GLA_SUBCHUNK)
    row = lambda blk: (lambda b, c: (b * nc + c, blk))
    const = lambda b, c: (0, 0)
    st_map = lambda b, c: (b, 0, 0, 0)
    w2p = jnp.zeros((LANE, GLA_DK), F32).at[:GLA_GATE_RANK].set(w2)
    o, s_t = pl.pallas_call(
        functools.partial(_gla_kernel, n_pad=n_pad, sub=sub),
        grid=(batch, nc),
        in_specs=[pl.BlockSpec((chunk, 2 * GLA_DK), row(0)),
                  pl.BlockSpec((chunk, GLA_DV), row(1)),
                  pl.BlockSpec((chunk, GLA_DV), row(2)),
                  pl.BlockSpec((chunk, LANE), row(3 * 512 // LANE)),
                  pl.BlockSpec((LANE, GLA_DK), const),
                  pl.BlockSpec((1, GLA_DK), const),
                  pl.BlockSpec((1, GLA_DV), const),
                  pl.BlockSpec((1, GLA_HEADS, GLA_HV, GLA_HK), st_map)],
        out_specs=[pl.BlockSpec((chunk, GLA_DV), row(0)),
                   pl.BlockSpec((1, GLA_HEADS, GLA_HV, GLA_HK), st_map)],
        out_shape=[jax.ShapeDtypeStruct((batch * seq_len, GLA_DV), F32),
                   jax.ShapeDtypeStruct((batch, GLA_HEADS, GLA_HV, GLA_HK), F32)],
        scratch_shapes=[pltpu.VMEM((GLA_HEADS, GLA_HV, GLA_HK), F32)],
        compiler_params=_params(2),
        name="gla",
    )(z, z, z, z, w2p, gate_b.reshape(1, GLA_DK), norm_g.reshape(1, GLA_DV), jnp.swapaxes(s0, -1, -2))
    return o, jnp.swapaxes(s_t, -1, -2)


def _rwkv_kernel(z_ref, mu_ref, w0_ref, w2_ref, a0_ref, a2_ref, g2_ref, kk_ref, ka_ref, rk_ref, lg_ref, lb_ref,
                 s0_ref, y_ref, s_ref, st_ref, carry_ref, *, n_pad):
    c = pl.program_id(1)
    rows = z_ref.shape[0]
    dim = RWKV_DIM
    heads = range(RWKV_HEADS)
    hs = [slice(h * RWKV_HEAD, (h + 1) * RWKV_HEAD) for h in heads]

    @pl.when(c == 0)
    def _():
        st_ref[...] = s0_ref[0]
        carry_ref[...] = jnp.zeros_like(carry_ref)

    valid = (c * rows + _iota2((rows, 1), 0)) >= n_pad
    z = z_ref[...]
    z_prev = _shift_rows(z, carry_ref[...], 1)
    carry_ref[...] = z[rows - SUBLANE:, :]
    z = z + mu_ref[...] * (z_prev - z)
    r, kr, vr = z[:, :dim], z[:, dim:2 * dim], z[:, 2 * dim:3 * dim]
    w_lo = z[:, 3 * dim:3 * dim + 64]
    a_lo = z[:, 3 * dim + 64:3 * dim + 128]
    g_lo = z[:, 3 * dim + 128:]
    w_raw = w0_ref[...] + _mm3(jnp.tanh(w_lo), w2_ref[...])
    log_w = -jnp.exp(-_softplus(-w_raw) - 0.5)
    a_lr = _sigmoid(a0_ref[...] + _mm3(a_lo, a2_ref[...]))
    gate = _mm3(_sigmoid(g_lo), g2_ref[...])
    kx = kr * kk_ref[...]
    kk = kx * lax.rsqrt(_head_sums(kx * kx, RWKV_HEAD, 2) + NORM_EPS)
    k_mod = kr * (1.0 + (a_lr - 1.0) * ka_ref[...])
    bonus = _head_sums(r * k_mod * rk_ref[...], RWKV_HEAD, 2) * vr

    log_w = jnp.where(valid, log_w, 0.0)
    k_in = jnp.where(valid, k_mod, 0.0)
    b_in = jnp.where(valid, kk * a_lr, 0.0)
    incl = _tri(rows)
    strict = _tri(rows, strict=True)
    c_incl = _mm_const_lhs(incl.astype(F32), log_w, 3)
    c_end = c_incl[rows - 1:rows, :]
    e_neg = jnp.exp(-c_incl)
    e_end = jnp.exp(c_end - c_incl)
    ar = jnp.concatenate([-kk * jnp.exp(c_incl - log_w), r * jnp.exp(c_incl)], axis=0).astype(BF16)
    bt = (b_in * e_neg).astype(BF16)
    kt = (k_in * e_neg).astype(BF16)
    bk_end = jnp.concatenate([b_in * e_end, k_in * e_end], axis=0).astype(BF16)
    vb = vr.astype(BF16)
    w_end = jnp.exp(c_end)
    s = [st_ref[h] for h in heads]
    g_b = [_dg(ar[:, hs[h]], bt[:, hs[h]], 'nt') for h in heads]
    g_k = [_dg(ar[:, hs[h]], kt[:, hs[h]], 'nt') for h in heads]
    g_s = [_dg(ar[:, hs[h]], s[h].astype(BF16), 'nt') for h in heads]
    m_ab = [jnp.where(strict, g_b[h][:rows], 0.0) for h in heads]
    rhs = [g_s[h][:rows] + _mm(jnp.where(strict, g_k[h][:rows], 0.0), vb[:, hs[h]]) for h in heads]
    t_inv = _unit_lower_inverses(m_ab)
    u = [_mm(t_inv[h], rhs[h]) for h in heads]
    ys = [g_s[h][rows:] + _mm(jnp.where(incl, g_b[h][rows:], 0.0), u[h])
          + _mm(jnp.where(incl, g_k[h][rows:], 0.0), vb[:, hs[h]]) for h in heads]
    for h in heads:
        uv = jnp.concatenate([u[h], vr[:, hs[h]]], axis=0)
        st_ref[h] = s[h] * w_end[:, hs[h]] + _mm(uv, bk_end[:, hs[h]], 'tn')
    y = jnp.concatenate(ys, axis=1)
    mean = _head_sums(y, RWKV_HEAD, 2) * (1.0 / RWKV_HEAD)
    d = y - mean
    var = _head_sums(d * d, RWKV_HEAD, 2) * (1.0 / RWKV_HEAD)
    y = d * lax.rsqrt(var + RWKV_GN_EPS) * lg_ref[...] + lb_ref[...]
    y_ref[...] = (y + bonus) * gate

    @pl.when(c == pl.num_programs(1) - 1)
    def _():
        s_ref[0] = st_ref[...]


def _rwkv(z, p, s0, batch, seq_len, n_pad, chunk):
    nc = seq_len // chunk
    dim = RWKV_DIM
    row = lambda b, c: (b * nc + c, 0)
    const = lambda b, c: (0, 0)
    st_map = lambda b, c: (b, 0, 0, 0)
    vec = lambda a: a.reshape(1, -1)
    small = [vec(p['mu']), vec(p['w0']), p['w2'], vec(p['a0']), p['a2'], p['g2'], vec(p['k_k']), vec(p['k_a']),
             vec(p['r_k']), vec(p['ln_g']), vec(p['ln_b'])]
    return pl.pallas_call(
        functools.partial(_rwkv_kernel, n_pad=n_pad),
        grid=(batch, nc),
        in_specs=([pl.BlockSpec((chunk, RWKV_COLS), row)]
                  + [pl.BlockSpec(a.shape, const) for a in small]
                  + [pl.BlockSpec((1, RWKV_HEADS, RWKV_HEAD, RWKV_HEAD), st_map)]),
        out_specs=[pl.BlockSpec((chunk, dim), row),
                   pl.BlockSpec((1, RWKV_HEADS, RWKV_HEAD, RWKV_HEAD), st_map)],
        out_shape=[jax.ShapeDtypeStruct((batch * seq_len, dim), F32),
                   jax.ShapeDtypeStruct((batch, RWKV_HEADS, RWKV_HEAD, RWKV_HEAD), F32)],
        scratch_shapes=[pltpu.VMEM((RWKV_HEADS, RWKV_HEAD, RWKV_HEAD), F32),
                        pltpu.VMEM((SUBLANE, RWKV_COLS), F32)],
        compiler_params=_params(2),
        name="rwkv7",
    )(z, *small, s0)


def _gdn_kernel(*refs, n_pad, has_fix):
    if has_fix:
        z_ref, cw_ref, alog_ref, dt_ref, ng_ref, s0_ref, fix_ref, o_ref, tail_ref, s_ref, st_ref, carry_ref = refs
    else:
        z_ref, cw_ref, alog_ref, dt_ref, ng_ref, s0_ref, o_ref, tail_ref, s_ref, st_ref, carry_ref = refs
    c = pl.program_id(1)
    rows = z_ref.shape[0]
    dim = GDN_DIM

    @pl.when(c == 0)
    def _():
        st_ref[...] = s0_ref[0]
        carry_ref[...] = jnp.zeros_like(carry_ref)

    valid = (c * rows + _iota2((rows, 1), 0)) >= n_pad
    pre = z_ref[:, :3 * dim]
    if has_fix:
        pre = pre + fix_ref[...]
    carry = carry_ref[...]
    conv = cw_ref[3:4, :] * pre
    for j in range(GDN_CONV - 1):
        conv = conv + cw_ref[j:j + 1, :] * _shift_rows(pre, carry, GDN_CONV - 1 - j)
    tail = pre[rows - SUBLANE:, :]
    carry_ref[...] = tail
    qkv = _silu(conv)
    zg = z_ref[:, 3 * dim:4 * dim]
    lo = z_ref[:, 4 * dim:]
    beta_all = _sigmoid(lo)
    g_all = -jnp.exp(alog_ref[...]) * _softplus(lo + dt_ref[...])
    g_all = jnp.where(valid, g_all, 0.0)

    heads = range(GDN_HEADS)
    hs = [slice(h * GDN_HEAD, (h + 1) * GDN_HEAD) for h in heads]
    incl = _tri(rows)
    strict = _tri(rows, strict=True)
    gam_all = _mm_const_lhs(incl.astype(F32), g_all, 3)
    shift = rows.bit_length() - 1
    pick = (_iota2((GDN_HEADS * rows, LANE), 1)
            == GDN_HEADS + (_iota2((GDN_HEADS * rows, LANE), 0) >> shift)).astype(F32)
    gam_cols = _mm_const_lhs(pick, gam_all, 3, 'nt')
    q_sq = _head_sums(jnp.square(qkv[:, :dim]), GDN_HEAD, 2)
    k_sq = _head_sums(jnp.square(qkv[:, dim:2 * dim]), GDN_HEAD, 2)
    q_all = qkv[:, :dim] * lax.rsqrt(q_sq + NORM_EPS) * (GDN_HEAD ** -0.5)
    k_all = jnp.where(valid, qkv[:, dim:2 * dim] * lax.rsqrt(k_sq + NORM_EPS), 0.0)
    v_all = qkv[:, 2 * dim:]
    kq = [jnp.concatenate([k_all[:, hs[h]], q_all[:, hs[h]]], axis=0).astype(BF16) for h in heads]
    gam = [gam_all[:, GDN_HEADS + h:GDN_HEADS + h + 1] for h in heads]
    beta = [beta_all[:, h:h + 1] for h in heads]
    dec = [jnp.where(incl, jnp.exp(jnp.where(incl, gam[h] - gam_cols[h * rows:(h + 1) * rows], 0.0)), 0.0)
           for h in heads]
    gram = [_dg(kq[h], k_all[:, hs[h]].astype(BF16), 'nt') for h in heads]
    m = [jnp.where(strict, gram[h][:rows] * dec[h], 0.0) * (-beta[h]) for h in heads]
    t_inv = _unit_lower_inverses(m)
    e_gam = [jnp.exp(gam[h]) for h in heads]
    rhs = [jnp.concatenate([v_all[:, hs[h]] * beta[h], k_all[:, hs[h]] * (beta[h] * e_gam[h])], axis=1)
           for h in heads]
    uw = [_mm(t_inv[h], rhs[h]) for h in heads]
    s = [st_ref[h] for h in heads]
    sb = [s[h].astype(BF16) for h in heads]
    delta = [uw[h][:, :GDN_HEAD] - _mm(uw[h][:, GDN_HEAD:], sb[h]) for h in heads]
    o = [_mm(q_all[:, hs[h]] * e_gam[h], sb[h]) + _mm(gram[h][rows:] * dec[h], delta[h]) for h in heads]
    for h in heads:
        g_end = gam[h][rows - 1:rows, :]
        st_ref[h] = s[h] * jnp.exp(g_end) + _mm(k_all[:, hs[h]] * jnp.exp(g_end - gam[h]), delta[h], 'tn')
        oh = o[h] * lax.rsqrt(jnp.mean(o[h] * o[h], axis=-1, keepdims=True) + NORM_EPS) * ng_ref[...]
        o_ref[:, hs[h]] = oh * _silu(zg[:, hs[h]])

    @pl.when(c == pl.num_programs(1) - 1)
    def _():
        s_ref[0] = st_ref[...]
        tail_ref[0] = tail


def _gdn(z, conv_w, a_log, dt_bias, norm_g, s0, fix, batch, seq_len, n_pad, chunk):
    nc = seq_len // chunk
    dim = GDN_DIM
    row = lambda b, c: (b * nc + c, 0)
    const = lambda b, c: (0, 0)
    st_map = lambda b, c: (b, 0, 0, 0)
    pad_lo = lambda a: jnp.zeros((1, LANE), F32).at[0, GDN_HEADS:2 * GDN_HEADS].set(a)
    in_specs = [pl.BlockSpec((chunk, GDN_Z), row),
                pl.BlockSpec((GDN_CONV, 3 * dim), const),
                pl.BlockSpec((1, LANE), const),
                pl.BlockSpec((1, LANE), const),
                pl.BlockSpec((1, GDN_HEAD), const),
                pl.BlockSpec((1, GDN_HEADS, GDN_HEAD, GDN_HEAD), st_map)]
    args = [z, conv_w, pad_lo(a_log), pad_lo(dt_bias), norm_g.reshape(1, GDN_HEAD), s0]
    if fix is not None:
        in_specs.append(pl.BlockSpec((chunk, 3 * dim), row))
        args.append(fix)
    return pl.pallas_call(
        functools.partial(_gdn_kernel, n_pad=n_pad, has_fix=fix is not None),
        grid=(batch, nc),
        in_specs=in_specs,
        out_specs=[pl.BlockSpec((chunk, dim), row),
                   pl.BlockSpec((1, SUBLANE, 3 * dim), lambda b, c: (b, 0, 0)),
                   pl.BlockSpec((1, GDN_HEADS, GDN_HEAD, GDN_HEAD), st_map)],
        out_shape=[jax.ShapeDtypeStruct((batch * seq_len, dim), F32),
                   jax.ShapeDtypeStruct((batch, SUBLANE, 3 * dim), F32),
                   jax.ShapeDtypeStruct((batch, GDN_HEADS, GDN_HEAD, GDN_HEAD), F32)],
        scratch_shapes=[pltpu.VMEM((GDN_HEADS, GDN_HEAD, GDN_HEAD), F32),
                        pltpu.VMEM((SUBLANE, 3 * dim), F32)],
        compiler_params=_params(2),
        name="gated_deltanet",
    )(*args)


def _pad_rows_fix(buf, seq_len, n_pad):
    b, w, c = buf.shape
    return jnp.zeros((b, seq_len, c), F32).at[:, n_pad - w:n_pad].set(buf).reshape(b * seq_len, c)


def _trunk(x, states, wts, batch, seq_len, n_pad, chunk, carried):
    st_gla, st_rwkv, st_shift, st_gdn, st_gdn_conv, st_ffn_conv = states
    d = x.shape[1]
    rows = batch * seq_len
    if carried:
        ffn_tm = _row_tile(rows, 512)
    else:
        ffn_tm = next(t for t in (352, 192, 64) if seq_len % t == 0)
    new = {k: [] for k in ('gla', 'rwkv', 'shift', 'gdn', 'gdn_conv', 'ffn_conv')}
    for l in range(DEPTH):
        i = l // 2
        x3 = x.reshape(batch, seq_len, d)
        if l % 2 == 0:
            new['shift'].append(x3[:, -1])
            x_in = x
            if carried:
                x_in = x3.at[:, n_pad - 1].set(st_shift[i]).reshape(rows, d)
            z_gla, z_rwkv = _inproj(x_in, wts['w_in_ab'][i], (GLA_Z, RWKV_COLS))
            o_gla, s_gla = _gla(z_gla, wts['gla_gate_w2'][i], wts['gla_gate_b'][i], wts['gla_norm_g'][i],
                                st_gla[i], batch, seq_len, n_pad, chunk)
            y_rwkv, s_rwkv = _rwkv(z_rwkv, {k: wts['rwkv_' + k][i] for k in
                                            ('mu', 'w0', 'w2', 'a0', 'a2', 'g2', 'k_k', 'k_a', 'r_k', 'ln_g', 'ln_b')},
                                   st_rwkv[i], batch, seq_len, n_pad, chunk)
            new['gla'].append(s_gla)
            new['rwkv'].append(s_rwkv)
            o = jnp.concatenate([o_gla, y_rwkv], axis=1)
            w_out = wts['w_out_ab'][i]
        else:
            (z,) = _inproj(x, wts['w_in_c'][i], (GDN_Z,))
            fix = _pad_rows_fix(st_gdn_conv[i], seq_len, n_pad) if carried else None
            o, tail, s_gdn = _gdn(z, wts['gdn_conv_w'][i], wts['gdn_A_log'][i], wts['gdn_dt_bias'][i],
                                  wts['gdn_norm_g'][i], st_gdn[i], fix, batch, seq_len, n_pad, chunk)
            new['gdn'].append(s_gdn)
            new['gdn_conv'].append(tail[:, SUBLANE - (GDN_CONV - 1):])
            w_out = wts['w_out_c'][i]
        x = _outproj_ln(o, x, w_out, wts['ln_mix_g'][l], wts['ln_mix_b'][l], seq_len, n_pad)
        fix = _pad_rows_fix(st_ffn_conv[l], seq_len, n_pad) if carried else None
        x, gate = _conv_ffn_ln(x, wts['w_up'][l], wts['w_down'][l], wts['ffn_conv_w'][l], wts['ffn_conv_b'][l],
                               wts['ln_ffn_g'][l], wts['ln_ffn_b'][l], fix, seq_len, n_pad, ffn_tm, carried)
        if carried:
            new['ffn_conv'].append(gate.reshape(batch, seq_len, -1)[:, seq_len - (FFN_CONV - 1):])
        else:
            per_seq = seq_len // ffn_tm
            new['ffn_conv'].append(gate.reshape(batch, per_seq, SUBLANE, -1)[:, -1, SUBLANE - (FFN_CONV - 1):])
    return (x,) + tuple(jnp.stack(new[k]) for k in ('gla', 'rwkv', 'shift', 'gdn', 'gdn_conv', 'ffn_conv'))


def kernel(x_prompt, x_sample, state_gla, state_rwkv, state_rwkv_shift, state_gdn, state_gdn_conv, state_ffn_conv, meta_tokens, w_in_ab, gla_gate_w2, gla_gate_b, gla_norm_g, rwkv_mu, rwkv_w0, rwkv_w2, rwkv_a0, rwkv_a2, rwkv_g2, rwkv_k_k, rwkv_k_a, rwkv_r_k, rwkv_ln_g, rwkv_ln_b, w_out_ab, w_in_c, gdn_conv_w, gdn_A_log, gdn_dt_bias, gdn_norm_g, w_out_c, w_up, ffn_conv_w, ffn_conv_b, w_down, ln_mix_g, ln_mix_b, ln_ffn_g, ln_ffn_b):
    d = x_prompt.shape[-1]
    n_ab, n_c = w_in_ab.shape[0], w_in_c.shape[0]
    gla_cols = 2 * GLA_DK + 2 * GLA_DV + GLA_GATE_RANK
    lo0 = 2 * GLA_DK + GLA_DV
    w_ab = jnp.concatenate([w_in_ab[:, :, :lo0], w_in_ab[:, :, lo0 + GLA_GATE_RANK:gla_cols],
                            w_in_ab[:, :, lo0:lo0 + GLA_GATE_RANK],
                            jnp.zeros((n_ab, d, LANE - GLA_GATE_RANK), w_in_ab.dtype),
                            w_in_ab[:, :, gla_cols:]], axis=2).astype(BF16)
    w_c = jnp.concatenate([w_in_c, jnp.zeros((n_c, d, LANE - 2 * GDN_HEADS), w_in_c.dtype)], axis=2).astype(BF16)
    wts = {
        'w_in_ab': w_ab, 'gla_gate_w2': gla_gate_w2, 'gla_gate_b': gla_gate_b, 'gla_norm_g': gla_norm_g,
        'rwkv_mu': rwkv_mu, 'rwkv_w0': rwkv_w0, 'rwkv_w2': rwkv_w2, 'rwkv_a0': rwkv_a0, 'rwkv_a2': rwkv_a2,
        'rwkv_g2': rwkv_g2, 'rwkv_k_k': rwkv_k_k, 'rwkv_k_a': rwkv_k_a, 'rwkv_r_k': rwkv_r_k,
        'rwkv_ln_g': rwkv_ln_g, 'rwkv_ln_b': rwkv_ln_b, 'w_out_ab': w_out_ab.astype(BF16),
        'w_in_c': w_c, 'gdn_conv_w': gdn_conv_w, 'gdn_A_log': gdn_A_log, 'gdn_dt_bias': gdn_dt_bias,
        'gdn_norm_g': gdn_norm_g, 'w_out_c': w_out_c.astype(BF16),
        'w_up': w_up.astype(BF16), 'ffn_conv_w': ffn_conv_w, 'ffn_conv_b': ffn_conv_b,
        'w_down': w_down.astype(BF16),
        'ln_mix_g': ln_mix_g, 'ln_mix_b': ln_mix_b, 'ln_ffn_g': ln_ffn_g, 'ln_ffn_b': ln_ffn_b,
    }

    bp, seq, _ = x_prompt.shape
    tp = -(-(N_META + seq + SUBLANE) // PROMPT_CHUNK) * PROMPT_CHUNK
    pad_p = tp - N_META - seq
    meta = jnp.broadcast_to(meta_tokens.astype(F32)[None], (bp, N_META, d))
    xp = jnp.concatenate([jnp.zeros((bp, pad_p, d), F32), meta, x_prompt], axis=1).reshape(bp * tp, d)
    zero_state = lambda s: jnp.zeros((s.shape[0], bp) + s.shape[2:], F32)
    p_out = _trunk(xp, tuple(zero_state(s) for s in (state_gla, state_rwkv, state_rwkv_shift, state_gdn,
                                                      state_gdn_conv, state_ffn_conv)),
                   wts, bp, tp, pad_p, PROMPT_CHUNK, carried=False)
    y_prompt = p_out[0].reshape(bp, tp, d)[:, pad_p + N_META:]

    bs, ts, _ = x_sample.shape
    tsp = -(-(ts + GDN_CONV - 1) // SUBLANE) * SUBLANE
    pad_s = tsp - ts
    xs = jnp.concatenate([jnp.zeros((bs, pad_s, d), F32), x_sample], axis=1).reshape(bs * tsp, d)
    s_out = _trunk(xs, (state_gla, state_rwkv, state_rwkv_shift, state_gdn, state_gdn_conv, state_ffn_conv),
                   wts, bs, tsp, pad_s, tsp, carried=True)
    y_sample = s_out[0].reshape(bs, tsp, d)[:, pad_s:]
    return (y_prompt, y_sample) + tuple(p_out[1:]) + tuple(s_out[1:])
```

```python
import functools
import math

import jax
import jax.numpy as jnp
from jax import lax
from jax.experimental import pallas as pl
from jax.experimental.pallas import tpu as pltpu

F32 = jnp.float32
BF16 = jnp.bfloat16

N_META = 16
GLA_HEADS, GLA_HK, GLA_HV = 4, 64, 128
GLA_DK, GLA_DV = GLA_HEADS * GLA_HK, GLA_HEADS * GLA_HV
GLA_GATE_RANK = 16
GLA_GATE_NORM = 16.0
GLA_SUBCHUNK = 16
RWKV_HEADS, RWKV_HEAD = 8, 64
RWKV_DIM = RWKV_HEADS * RWKV_HEAD
RWKV_COLS = 3 * RWKV_DIM + 64 + 64 + 128
RWKV_GN_EPS = 64e-5
GDN_HEADS, GDN_HEAD = 8, 128
GDN_DIM = GDN_HEADS * GDN_HEAD
GDN_CONV = 4
FFN_CONV = 3
LN_EPS = 1e-5
NORM_EPS = 1e-6
DEPTH = 4
DEEPNORM_ALPHA = (2.0 * DEPTH) ** 0.25

LANE = 128
SUBLANE = 8
PROMPT_CHUNK = 64
PROMPT_GROUP = 4
SAMPLE_GROUP = 8
VMEM_LIMIT = 56 * 1024 * 1024

GLA_Z = 3 * 512 + LANE
GDN_Z = 4 * GDN_DIM + LANE


def _params(n_axes):
    return pltpu.CompilerParams(dimension_semantics=("arbitrary",) * n_axes,
                                vmem_limit_bytes=VMEM_LIMIT)


def _row_tile(rows, cap):
    t = cap
    while rows % t:
        t //= 2
    return t


_DIMS = {'nn': (((1,), (0,)), ((), ())),
         'nt': (((1,), (1,)), ((), ())),
         'tn': (((0,), (0,)), ((), ()))}


def _dg(a, b, kind='nn'):
    return lax.dot_general(a, b, _DIMS[kind], preferred_element_type=F32)


def _split(x, pieces):
    out = []
    for _ in range(pieces - 1):
        hi = x.astype(BF16)
        out.append(hi)
        x = x - hi.astype(F32)
    out.append(x.astype(BF16))
    return out


def _mm(a, b, kind='nn'):
    return _dg(a.astype(BF16), b.astype(BF16), kind)


def _mm3(a, b, kind='nn'):
    ah, al = _split(a, 2)
    bh, bl = _split(b, 2)
    return _dg(ah, bh, kind) + _dg(al, bh, kind) + _dg(ah, bl, kind)


def _mm_const_lhs(c, x, pieces, kind='nn'):
    cb = c.astype(BF16)
    return sum(_dg(cb, p, kind) for p in _split(x, pieces))


def _mm_const_rhs(x, c, pieces, kind='nn'):
    cb = c.astype(BF16)
    return sum(_dg(p, cb, kind) for p in _split(x, pieces))


def _bdot(a, w):
    return jnp.dot(a.astype(BF16), w, preferred_element_type=F32)


def _iota2(shape, dim):
    return lax.broadcasted_iota(jnp.int32, shape, dim)


def _tri(n, strict=False):
    r, c = _iota2((n, n), 0), _iota2((n, n), 1)
    return (c < r) if strict else (c <= r)


def _softplus(x):
    return jnp.maximum(x, 0.0) + jnp.log1p(jnp.exp(-jnp.abs(x)))


def _sigmoid(x):
    return 1.0 / (1.0 + jnp.exp(-x))


def _silu(x):
    return x * _sigmoid(x)


def _unit_lower_inverses(ms):
    n = ms[0].shape[0]
    eye = (_iota2((n, n), 0) == _iota2((n, n), 1)).astype(F32)
    xs = [eye + m for m in ms]
    ps = list(ms)
    for _ in range(int(math.log2(n)) - 1):
        ps = [_mm(p, p) for p in ps]
        xs = [x + _mm(x, p) for x, p in zip(xs, ps)]
    return xs


def _head_sums(x, width, pieces):
    rows, cols = x.shape
    nb = cols // LANE
    shift = width.bit_length() - 1
    grp = (_iota2((LANE, LANE), 0) >> shift == _iota2((LANE, LANE), 1) >> shift).astype(F32)
    xs = jnp.concatenate([x[:, i * LANE:(i + 1) * LANE] for i in range(nb)], axis=0)
    s = _mm_const_rhs(xs, grp, pieces)
    return jnp.concatenate([s[i * rows:(i + 1) * rows] for i in range(nb)], axis=1)


def _shift_rows(cur, carry, k):
    rolled = pltpu.roll(cur, k, 0)
    head = jnp.where(_iota2((SUBLANE, 1), 0) < k, pltpu.roll(carry, k, 0), rolled[:SUBLANE])
    if cur.shape[0] == SUBLANE:
        return head
    return jnp.concatenate([head, rolled[SUBLANE:]], axis=0)


def _shift_rows_seqs(cur, carries, k):
    n = len(carries)
    rows = cur.shape[0] // n
    return jnp.concatenate([_shift_rows(cur[g * rows:(g + 1) * rows], carries[g], k) for g in range(n)], axis=0) \
        if n > 1 else _shift_rows(cur, carries[0], k)


def _seq_rows(shape, seq_rows):
    return _iota2(shape, 0) & (seq_rows - 1)


def _valid_rows(row0, rows, seq_len, n_pad):
    if seq_len & (seq_len - 1) == 0:
        t = (row0 + _iota2((rows, 1), 0)) & (seq_len - 1)
    else:
        t = lax.rem(row0, seq_len) + _iota2((rows, 1), 0)
        for _ in range(-(-rows // seq_len)):
            t = jnp.where(t >= seq_len, t - seq_len, t)
    return t >= n_pad


def _layer_norm_rows(h, g, b):
    mu = jnp.mean(h, axis=-1, keepdims=True)
    d = h - mu
    var = jnp.mean(d * d, axis=-1, keepdims=True)
    return d * lax.rsqrt(var + LN_EPS) * g + b


def _inproj_kernel(x_ref, w_ref, *out_refs, widths):
    z = _bdot(x_ref[...], w_ref[...])
    off = 0
    for o_ref, wd in zip(out_refs, widths):
        o_ref[...] = z[:, off:off + wd]
        off += wd


def _inproj(x, w, widths):
    rows, d = x.shape
    n = w.shape[1]
    tm = _row_tile(rows, 256)
    return pl.pallas_call(
        functools.partial(_inproj_kernel, widths=widths),
        grid=(rows // tm,),
        in_specs=[pl.BlockSpec((tm, d), lambda i: (i, 0)),
                  pl.BlockSpec((d, n), lambda i: (0, 0))],
        out_specs=[pl.BlockSpec((tm, wd), lambda i: (i, 0)) for wd in widths],
        out_shape=[jax.ShapeDtypeStruct((rows, wd), F32) for wd in widths],
        compiler_params=_params(1),
        name="inproj",
    )(x, w)


def _outln_kernel(*refs, seq_len, n_pad):
    o_refs, (x_ref, w_ref, g_ref, b_ref, y_ref) = refs[:-5], refs[-5:]
    tm = x_ref.shape[0]
    h = DEEPNORM_ALPHA * x_ref[...]
    off = 0
    for o_ref in o_refs:
        h = h + _bdot(o_ref[...], w_ref[off:off + o_ref.shape[1], :])
        off += o_ref.shape[1]
    y = _layer_norm_rows(h, g_ref[...], b_ref[...])
    valid = _valid_rows(pl.program_id(0) * tm, tm, seq_len, n_pad)
    y_ref[...] = jnp.where(valid, y, 0.0)


def _outproj_ln(os, x, w, g, b, seq_len, n_pad):
    rows, d = x.shape
    k = w.shape[0]
    tm = _row_tile(rows, 512)
    return pl.pallas_call(
        functools.partial(_outln_kernel, seq_len=seq_len, n_pad=n_pad),
        grid=(rows // tm,),
        in_specs=[pl.BlockSpec((tm, o.shape[1]), lambda i: (i, 0)) for o in os] + [
                  pl.BlockSpec((tm, d), lambda i: (i, 0)),
                  pl.BlockSpec((k, d), lambda i: (0, 0)),
                  pl.BlockSpec((1, d), lambda i: (0, 0)),
                  pl.BlockSpec((1, d), lambda i: (0, 0))],
        out_specs=pl.BlockSpec((tm, d), lambda i: (i, 0)),
        out_shape=jax.ShapeDtypeStruct((rows, d), F32),
        compiler_params=_params(1),
        name="outproj_ln",
    )(*os, x, w, g.reshape(1, d), b.reshape(1, d))


def _ffn_kernel(*refs, seq_len, n_pad, d_ff, ff_chunk, has_fix, full_gate):
    if has_fix:
        x_ref, wup_ref, wdn_ref, cw_ref, cb_ref, g_ref, b_ref, fix_ref, y_ref, gate_ref, carry_ref = refs
    else:
        x_ref, wup_ref, wdn_ref, cw_ref, cb_ref, g_ref, b_ref, y_ref, gate_ref, carry_ref = refs
    i = pl.program_id(0)
    tm = x_ref.shape[0]

    @pl.when(i == 0)
    def _():
        carry_ref[...] = jnp.zeros_like(carry_ref)

    x = x_ref[...]
    xb = x.astype(BF16)
    acc = jnp.zeros(x.shape, F32)
    for c in range(d_ff // ff_chunk):
        lo = c * ff_chunk
        gate = jnp.dot(xb, wup_ref[:, lo:lo + ff_chunk], preferred_element_type=F32)
        up = jnp.dot(xb, wup_ref[:, d_ff + lo:d_ff + lo + ff_chunk], preferred_element_type=F32)
        if has_fix:
            gate = gate + fix_ref[:, lo:lo + ff_chunk]
        carry = carry_ref[:, lo:lo + ff_chunk]
        conv = (cw_ref[0:1, lo:lo + ff_chunk] * _shift_rows(gate, carry, 2)
                + cw_ref[1:2, lo:lo + ff_chunk] * _shift_rows(gate, carry, 1)
                + cw_ref[2:3, lo:lo + ff_chunk] * gate
                + cb_ref[:, lo:lo + ff_chunk])
        act = _silu(conv) * up
        acc = acc + jnp.dot(act.astype(BF16), wdn_ref[lo:lo + ff_chunk, :], preferred_element_type=F32)
        carry_ref[:, lo:lo + ff_chunk] = gate[tm - SUBLANE:, :]
        if full_gate:
            gate_ref[:, lo:lo + ff_chunk] = gate
        else:
            gate_ref[0, :, lo:lo + ff_chunk] = gate[tm - SUBLANE:, :]
    y = _layer_norm_rows(DEEPNORM_ALPHA * x + acc, g_ref[...], b_ref[...])
    valid = _valid_rows(i * tm, tm, seq_len, n_pad)
    y_ref[...] = jnp.where(valid, y, 0.0)


def _conv_ffn_ln(x, w_up, w_down, conv_w, conv_b, g, b, fix, seq_len, n_pad, tm, full_gate):
    rows, d = x.shape
    d_ff = w_down.shape[0]
    ff_chunk = d_ff // 2
    nt = rows // tm
    in_specs = [pl.BlockSpec((tm, d), lambda i: (i, 0)),
                pl.BlockSpec((d, 2 * d_ff), lambda i: (0, 0)),
                pl.BlockSpec((d_ff, d), lambda i: (0, 0)),
                pl.BlockSpec((FFN_CONV, d_ff), lambda i: (0, 0)),
                pl.BlockSpec((1, d_ff), lambda i: (0, 0)),
                pl.BlockSpec((1, d), lambda i: (0, 0)),
                pl.BlockSpec((1, d), lambda i: (0, 0))]
    args = [x, w_up, w_down, conv_w, conv_b.reshape(1, d_ff), g.reshape(1, d), b.reshape(1, d)]
    if fix is not None:
        in_specs.append(pl.BlockSpec((tm, d_ff), lambda i: (i, 0)))
        args.append(fix)
    if full_gate:
        gate_spec = pl.BlockSpec((tm, d_ff), lambda i: (i, 0))
        gate_shape = jax.ShapeDtypeStruct((rows, d_ff), F32)
    else:
        gate_spec = pl.BlockSpec((1, SUBLANE, d_ff), lambda i: (i, 0, 0))
        gate_shape = jax.ShapeDtypeStruct((nt, SUBLANE, d_ff), F32)
    return pl.pallas_call(
        functools.partial(_ffn_kernel, seq_len=seq_len, n_pad=n_pad, d_ff=d_ff, ff_chunk=ff_chunk,
                          has_fix=fix is not None, full_gate=full_gate),
        grid=(nt,),
        in_specs=in_specs,
        out_specs=[pl.BlockSpec((tm, d), lambda i: (i, 0)), gate_spec],
        out_shape=[jax.ShapeDtypeStruct((rows, d), F32), gate_shape],
        scratch_shapes=[pltpu.VMEM((SUBLANE, d_ff), F32)],
        compiler_params=_params(1),
        name="conv_ffn_ln",
    )(*args)


def _group(batch, preferred):
    g = preferred
    while batch % g:
        g //= 2
    return g


def _state_spec(layer, group, dims):
    return pl.BlockSpec((None, group) + dims, lambda bg, c: (layer, bg) + (0,) * len(dims))


def _stacked(ref):
    g, rows, cols = ref.shape
    return ref[...].reshape(g * rows, cols)


def _seq_masks(rows, seq_rows):
    shift = seq_rows.bit_length() - 1
    r, c = _iota2((rows, rows), 0), _iota2((rows, rows), 1)
    same = (r >> shift) == (c >> shift)
    return same, same & (c <= r)


def _gla_kernel(qk_ref, v_ref, og_ref, glo_ref, w2_ref, gb_ref, ng_ref, s0_ref, buf_ref, o_ref, s_ref, st_ref,
                *, n_pad, sub):
    del buf_ref
    c = pl.program_id(1)
    group, chunk = qk_ref.shape[0], qk_ref.shape[1]
    rows = group * chunk
    nsub = chunk // sub
    chains = [(g, h) for g in range(group) for h in range(GLA_HEADS)]
    ks = [slice(h * GLA_HK, (h + 1) * GLA_HK) for h in range(GLA_HEADS)]
    vs = [slice(h * GLA_HV, (h + 1) * GLA_HV) for h in range(GLA_HEADS)]
    rs = [slice(g * chunk, (g + 1) * chunk) for g in range(group)]

    @pl.when(c == 0)
    def _():
        for i, (g, h) in enumerate(chains):
            st_ref[i] = s0_ref[g, h].T

    valid = (c * chunk + _seq_rows((rows, 1), chunk)) >= n_pad
    pre = _mm3(_stacked(glo_ref), w2_ref[...]) + gb_ref[...]
    log_a = -_softplus(-pre) * (1.0 / GLA_GATE_NORM)
    log_a = jnp.where(valid, log_a, 0.0)
    qk = _stacked(qk_ref)
    q = qk[:, :GLA_DK] * (GLA_HK ** -0.5)
    k = jnp.where(valid, qk[:, GLA_DK:], 0.0)
    same, causal = _seq_masks(rows, sub)
    b = _mm_const_lhs(causal.astype(F32), log_a, 3)
    b_tot = _mm_const_lhs(same.astype(F32), log_a, 3)
    qd = (q * jnp.exp(b)).astype(BF16)
    kd = (k * jnp.exp(-b)).astype(BF16)
    kend = (k * jnp.exp(b_tot - b)).astype(BF16)
    vb = _stacked(v_ref).astype(BF16)
    causal_c = _seq_masks(chunk, sub)[1]
    att = [jnp.where(causal_c, _dg(qd[rs[g], ks[h]], kd[rs[g], ks[h]], 'nt'), 0.0).astype(BF16) for g, h in chains]
    o = [_dg(att[i], vb[rs[g], vs[h]]) for i, (g, h) in enumerate(chains)]
    s = [st_ref[i] for i in range(len(chains))]
    o_inter = [[] for _ in chains]
    for j in range(nsub):
        for i, (g, h) in enumerate(chains):
            sl = slice(g * chunk + j * sub, g * chunk + (j + 1) * sub)
            o_inter[i].append(_dg(qd[sl, ks[h]], s[i].astype(BF16), 'nt'))
        nxt = []
        for i, (g, h) in enumerate(chains):
            sl = slice(g * chunk + j * sub, g * chunk + (j + 1) * sub)
            dec = jnp.exp(b_tot[g * chunk + j * sub:g * chunk + j * sub + 1, ks[h]])
            nxt.append(s[i] * dec + _dg(vb[sl, vs[h]], kend[sl, ks[h]], 'tn'))
        s = nxt
    og = _stacked(og_ref)
    for i, (g, h) in enumerate(chains):
        st_ref[i] = s[i]
        oh = o[i] + (jnp.concatenate(o_inter[i], axis=0) if nsub > 1 else o_inter[i][0])
        oh = oh * lax.rsqrt(jnp.mean(oh * oh, axis=-1, keepdims=True) + NORM_EPS) * ng_ref[:, vs[h]]
        o_ref[g, :, vs[h]] = oh * _silu(og[rs[g], vs[h]])

    @pl.when(c == pl.num_programs(1) - 1)
    def _():
        for i, (g, h) in enumerate(chains):
            s_ref[g, h] = st_ref[i].T


def _gla(z, w2, gate_b, norm_g, s_all, layer, s_buf, n_pad, chunk, group):
    batch, seq_len, _ = z.shape
    nc = seq_len // chunk
    sub = math.gcd(chunk, GLA_SUBCHUNK)
    row = lambda blk: (lambda bg, c: (bg, c, blk))
    const = lambda bg, c: (0, 0)
    st = _state_spec(layer, group, (GLA_HEADS, GLA_HK, GLA_HV))
    w2p = jnp.zeros((LANE, GLA_DK), F32).at[:GLA_GATE_RANK].set(w2)
    return pl.pallas_call(
        functools.partial(_gla_kernel, n_pad=n_pad, sub=sub),
        grid=(batch // group, nc),
        in_specs=[pl.BlockSpec((group, chunk, 2 * GLA_DK), row(0)),
                  pl.BlockSpec((group, chunk, GLA_DV), row(1)),
                  pl.BlockSpec((group, chunk, GLA_DV), row(2)),
                  pl.BlockSpec((group, chunk, LANE), row(3 * 512 // LANE)),
                  pl.BlockSpec((LANE, GLA_DK), const),
                  pl.BlockSpec((1, GLA_DK), const),
                  pl.BlockSpec((1, GLA_DV), const),
                  st,
                  pl.BlockSpec(memory_space=pl.ANY)],
        out_specs=[pl.BlockSpec((group, chunk, GLA_DV), row(0)), st],
        out_shape=[jax.ShapeDtypeStruct((batch, seq_len, GLA_DV), F32),
                   jax.ShapeDtypeStruct(s_buf.shape, F32)],
        scratch_shapes=[pltpu.VMEM((group * GLA_HEADS, GLA_HV, GLA_HK), F32)],
        input_output_aliases={8: 1},
        compiler_params=_params(2),
        name="gla",
    )(z, z, z, z, w2p, gate_b.reshape(1, GLA_DK), norm_g.reshape(1, GLA_DV), s_all, s_buf)


def _rwkv_kernel(z_ref, mu_ref, w0_ref, w2_ref, a0_ref, a2_ref, g2_ref, kk_ref, ka_ref, rk_ref, lg_ref, lb_ref,
                 s0_ref, buf_ref, y_ref, s_ref, st_ref, carry_ref, *, n_pad):
    del buf_ref
    c = pl.program_id(1)
    group, chunk = z_ref.shape[0], z_ref.shape[1]
    rows = group * chunk
    dim = RWKV_DIM
    chains = [(g, h) for g in range(group) for h in range(RWKV_HEADS)]
    hs = [slice(h * RWKV_HEAD, (h + 1) * RWKV_HEAD) for h in range(RWKV_HEADS)]
    rs = [slice(g * chunk, (g + 1) * chunk) for g in range(group)]

    @pl.when(c == 0)
    def _():
        for i, (g, h) in enumerate(chains):
            st_ref[i] = s0_ref[g, h]
        carry_ref[...] = jnp.zeros_like(carry_ref)

    valid = (c * chunk + _seq_rows((rows, 1), chunk)) >= n_pad
    z = _stacked(z_ref)
    z_prev = _shift_rows_seqs(z, [carry_ref[g] for g in range(group)], 1)
    for g in range(group):
        carry_ref[g] = z[(g + 1) * chunk - SUBLANE:(g + 1) * chunk]
    z = z + mu_ref[...] * (z_prev - z)
    r, kr, vr = z[:, :dim], z[:, dim:2 * dim], z[:, 2 * dim:3 * dim]
    w_lo = z[:, 3 * dim:3 * dim + 64]
    a_lo = z[:, 3 * dim + 64:3 * dim + 128]
    g_lo = z[:, 3 * dim + 128:]
    w_raw = w0_ref[...] + _mm3(jnp.tanh(w_lo), w2_ref[...])
    log_w = -jnp.exp(-_softplus(-w_raw) - 0.5)
    a_lr = _sigmoid(a0_ref[...] + _mm3(a_lo, a2_ref[...]))
    gate = _mm3(_sigmoid(g_lo), g2_ref[...])
    kx = kr * kk_ref[...]
    kk = kx * lax.rsqrt(_head_sums(kx * kx, RWKV_HEAD, 2) + NORM_EPS)
    k_mod = kr * (1.0 + (a_lr - 1.0) * ka_ref[...])
    bonus = _head_sums(r * k_mod * rk_ref[...], RWKV_HEAD, 2) * vr

    log_w = jnp.where(valid, log_w, 0.0)
    k_in = jnp.where(valid, k_mod, 0.0)
    b_in = jnp.where(valid, kk * a_lr, 0.0)
    same, causal = _seq_masks(rows, chunk)
    c_incl = _mm_const_lhs(causal.astype(F32), log_w, 3)
    c_tot = _mm_const_lhs(same.astype(F32), log_w, 3)
    e_neg = jnp.exp(-c_incl)
    e_end = jnp.exp(c_tot - c_incl)
    a_t = -kk * jnp.exp(c_incl - log_w)
    r_t = r * jnp.exp(c_incl)
    bt = (b_in * e_neg).astype(BF16)
    kt = (k_in * e_neg).astype(BF16)
    b_e, k_e = b_in * e_end, k_in * e_end
    vb = vr.astype(BF16)
    incl = _tri(chunk)
    strict = _tri(chunk, strict=True)
    n = len(chains)
    s = [st_ref[i] for i in range(n)]
    ar = [jnp.concatenate([a_t[rs[g], hs[h]], r_t[rs[g], hs[h]]], axis=0).astype(BF16) for g, h in chains]
    g_b = [_dg(ar[i], bt[rs[g], hs[h]], 'nt') for i, (g, h) in enumerate(chains)]
    g_k = [_dg(ar[i], kt[rs[g], hs[h]], 'nt') for i, (g, h) in enumerate(chains)]
    g_s = [_dg(ar[i], s[i].astype(BF16), 'nt') for i in range(n)]
    m_ab = [jnp.where(strict, g_b[i][:chunk], 0.0) for i in range(n)]
    rhs = [g_s[i][:chunk] + _mm(jnp.where(strict, g_k[i][:chunk], 0.0), vb[rs[g], hs[h]])
           for i, (g, h) in enumerate(chains)]
    t_inv = _unit_lower_inverses(m_ab)
    u = [_mm(t_inv[i], rhs[i]) for i in range(n)]
    ys = [g_s[i][chunk:] + _mm(jnp.where(incl, g_b[i][chunk:], 0.0), u[i])
          + _mm(jnp.where(incl, g_k[i][chunk:], 0.0), vb[rs[g], hs[h]]) for i, (g, h) in enumerate(chains)]
    for i, (g, h) in enumerate(chains):
        uv = jnp.concatenate([u[i], vr[rs[g], hs[h]]], axis=0)
        bk = jnp.concatenate([b_e[rs[g], hs[h]], k_e[rs[g], hs[h]]], axis=0)
        w_end = jnp.exp(c_tot[g * chunk:g * chunk + 1, hs[h]])
        st_ref[i] = s[i] * w_end + _mm(uv, bk, 'tn')
    per_seq = [jnp.concatenate(ys[g * RWKV_HEADS:(g + 1) * RWKV_HEADS], axis=1) for g in range(group)]
    y = jnp.concatenate(per_seq, axis=0) if group > 1 else per_seq[0]
    mean = _head_sums(y, RWKV_HEAD, 2) * (1.0 / RWKV_HEAD)
    d = y - mean
    var = _head_sums(d * d, RWKV_HEAD, 2) * (1.0 / RWKV_HEAD)
    y = d * lax.rsqrt(var + RWKV_GN_EPS) * lg_ref[...] + lb_ref[...]
    y_ref[...] = ((y + bonus) * gate).reshape(group, chunk, dim)

    @pl.when(c == pl.num_programs(1) - 1)
    def _():
        for i, (g, h) in enumerate(chains):
            s_ref[g, h] = st_ref[i]


def _rwkv(z, p, s_all, layer, s_buf, n_pad, chunk, group):
    batch, seq_len, _ = z.shape
    nc = seq_len // chunk
    dim = RWKV_DIM
    row = lambda bg, c: (bg, c, 0)
    const = lambda bg, c: (0, 0)
    st = _state_spec(layer, group, (RWKV_HEADS, RWKV_HEAD, RWKV_HEAD))
    vec = lambda a: a.reshape(1, -1)
    small = [vec(p['mu']), vec(p['w0']), p['w2'], vec(p['a0']), p['a2'], p['g2'], vec(p['k_k']), vec(p['k_a']),
             vec(p['r_k']), vec(p['ln_g']), vec(p['ln_b'])]
    return pl.pallas_call(
        functools.partial(_rwkv_kernel, n_pad=n_pad),
        grid=(batch // group, nc),
        in_specs=([pl.BlockSpec((group, chunk, RWKV_COLS), row)]
                  + [pl.BlockSpec(a.shape, const) for a in small]
                  + [st, pl.BlockSpec(memory_space=pl.ANY)]),
        out_specs=[pl.BlockSpec((group, chunk, dim), row), st],
        out_shape=[jax.ShapeDtypeStruct((batch, seq_len, dim), F32),
                   jax.ShapeDtypeStruct(s_buf.shape, F32)],
        scratch_shapes=[pltpu.VMEM((group * RWKV_HEADS, RWKV_HEAD, RWKV_HEAD), F32),
                        pltpu.VMEM((group, SUBLANE, RWKV_COLS), F32)],
        input_output_aliases={len(small) + 2: 1},
        compiler_params=_params(2),
        name="rwkv7",
    )(z, *small, s_all, s_buf)


def _gdn_kernel(*refs, n_pad, has_fix):
    if has_fix:
        (z_ref, cw_ref, alog_ref, dt_ref, ng_ref, s0_ref, buf_ref, fix_ref,
         o_ref, tail_ref, s_ref, st_ref, carry_ref) = refs
    else:
        z_ref, cw_ref, alog_ref, dt_ref, ng_ref, s0_ref, buf_ref, o_ref, tail_ref, s_ref, st_ref, carry_ref = refs
    del buf_ref
    c = pl.program_id(1)
    group, chunk = z_ref.shape[0], z_ref.shape[1]
    rows = group * chunk
    dim = GDN_DIM
    chains = [(g, h) for g in range(group) for h in range(GDN_HEADS)]
    hs = [slice(h * GDN_HEAD, (h + 1) * GDN_HEAD) for h in range(GDN_HEADS)]
    rs = [slice(g * chunk, (g + 1) * chunk) for g in range(group)]

    @pl.when(c == 0)
    def _():
        for i, (g, h) in enumerate(chains):
            st_ref[i] = s0_ref[g, h]
        carry_ref[...] = jnp.zeros_like(carry_ref)

    valid = (c * chunk + _seq_rows((rows, 1), chunk)) >= n_pad
    z = _stacked(z_ref)
    pre = z[:, :3 * dim]
    if has_fix:
        pre = pre + _stacked(fix_ref)
    carries = [carry_ref[g] for g in range(group)]
    conv = cw_ref[3:4, :] * pre
    for j in range(GDN_CONV - 1):
        conv = conv + cw_ref[j:j + 1, :] * _shift_rows_seqs(pre, carries, GDN_CONV - 1 - j)
    tails = [pre[(g + 1) * chunk - SUBLANE:(g + 1) * chunk] for g in range(group)]
    for g in range(group):
        carry_ref[g] = tails[g]
    qkv = _silu(conv)
    zg = z[:, 3 * dim:4 * dim]
    lo = z[:, 4 * dim:]
    beta_all = _sigmoid(lo)
    g_all = -jnp.exp(alog_ref[...]) * _softplus(lo + dt_ref[...])
    g_all = jnp.where(valid, g_all, 0.0)

    incl = _tri(chunk)
    strict = _tri(chunk, strict=True)
    causal = _seq_masks(rows, chunk)[1]
    gam_all = _mm_const_lhs(causal.astype(F32), g_all, 3)
    shift = chunk.bit_length() - 1
    pick = (_iota2((GDN_HEADS * chunk, LANE), 1)
            == GDN_HEADS + (_iota2((GDN_HEADS * chunk, LANE), 0) >> shift)).astype(F32)
    gam_cols = [_mm_const_lhs(pick, gam_all[rs[g]], 3, 'nt') for g in range(group)]
    q_sq = _head_sums(jnp.square(qkv[:, :dim]), GDN_HEAD, 2)
    k_sq = _head_sums(jnp.square(qkv[:, dim:2 * dim]), GDN_HEAD, 2)
    q_all = qkv[:, :dim] * lax.rsqrt(q_sq + NORM_EPS) * (GDN_HEAD ** -0.5)
    k_all = jnp.where(valid, qkv[:, dim:2 * dim] * lax.rsqrt(k_sq + NORM_EPS), 0.0)
    v_all = qkv[:, 2 * dim:]
    n = len(chains)
    gam = [gam_all[rs[g], GDN_HEADS + h:GDN_HEADS + h + 1] for g, h in chains]
    beta = [beta_all[rs[g], h:h + 1] for g, h in chains]
    kq = [jnp.concatenate([k_all[rs[g], hs[h]], q_all[rs[g], hs[h]]], axis=0).astype(BF16) for g, h in chains]
    dec = [jnp.where(incl, jnp.exp(jnp.where(incl, gam[i] - gam_cols[g][h * chunk:(h + 1) * chunk], 0.0)), 0.0)
           for i, (g, h) in enumerate(chains)]
    gram = [_dg(kq[i], k_all[rs[g], hs[h]].astype(BF16), 'nt') for i, (g, h) in enumerate(chains)]
    m = [jnp.where(strict, gram[i][:chunk] * dec[i], 0.0) * (-beta[i]) for i in range(n)]
    t_inv = _unit_lower_inverses(m)
    e_gam = [jnp.exp(gam[i]) for i in range(n)]
    rhs = [jnp.concatenate([v_all[rs[g], hs[h]] * beta[i], k_all[rs[g], hs[h]] * (beta[i] * e_gam[i])], axis=1)
           for i, (g, h) in enumerate(chains)]
    uw = [_mm(t_inv[i], rhs[i]) for i in range(n)]
    s = [st_ref[i] for i in range(n)]
    sb = [s[i].astype(BF16) for i in range(n)]
    delta = [uw[i][:, :GDN_HEAD] - _mm(uw[i][:, GDN_HEAD:], sb[i]) for i in range(n)]
    o = [_mm(q_all[rs[g], hs[h]] * e_gam[i], sb[i]) + _mm(gram[i][chunk:] * dec[i], delta[i])
         for i, (g, h) in enumerate(chains)]
    for i, (g, h) in enumerate(chains):
        g_end = gam[i][chunk - 1:chunk, :]
        st_ref[i] = s[i] * jnp.exp(g_end) + _mm(k_all[rs[g], hs[h]] * jnp.exp(g_end - gam[i]), delta[i], 'tn')
        oh = o[i] * lax.rsqrt(jnp.mean(o[i] * o[i], axis=-1, keepdims=True) + NORM_EPS) * ng_ref[...]
        o_ref[g, :, hs[h]] = oh * _silu(zg[rs[g], hs[h]])

    @pl.when(c == pl.num_programs(1) - 1)
    def _():
        for i, (g, h) in enumerate(chains):
            s_ref[g, h] = st_ref[i]
        for g in range(group):
            tail_ref[g] = tails[g]


def _gdn(z, conv_w, a_log, dt_bias, norm_g, s_all, layer, s_buf, fix, n_pad, chunk, group):
    batch, seq_len, _ = z.shape
    nc = seq_len // chunk
    dim = GDN_DIM
    row = lambda bg, c: (bg, c, 0)
    const = lambda bg, c: (0, 0)
    st = _state_spec(layer, group, (GDN_HEADS, GDN_HEAD, GDN_HEAD))
    pad_lo = lambda a: jnp.zeros((1, LANE), F32).at[0, GDN_HEADS:2 * GDN_HEADS].set(a)
    in_specs = [pl.BlockSpec((group, chunk, GDN_Z), row),
                pl.BlockSpec((GDN_CONV, 3 * dim), const),
                pl.BlockSpec((1, LANE), const),
                pl.BlockSpec((1, LANE), const),
                pl.BlockSpec((1, GDN_HEAD), const),
                st,
                pl.BlockSpec(memory_space=pl.ANY)]
    args = [z, conv_w, pad_lo(a_log), pad_lo(dt_bias), norm_g.reshape(1, GDN_HEAD), s_all, s_buf]
    if fix is not None:
        in_specs.append(pl.BlockSpec((group, chunk, 3 * dim), row))
        args.append(fix)
    return pl.pallas_call(
        functools.partial(_gdn_kernel, n_pad=n_pad, has_fix=fix is not None),
        grid=(batch // group, nc),
        in_specs=in_specs,
        out_specs=[pl.BlockSpec((group, chunk, dim), row),
                   pl.BlockSpec((group, SUBLANE, 3 * dim), lambda bg, c: (bg, 0, 0)),
                   st],
        out_shape=[jax.ShapeDtypeStruct((batch, seq_len, dim), F32),
                   jax.ShapeDtypeStruct((batch, SUBLANE, 3 * dim), F32),
                   jax.ShapeDtypeStruct(s_buf.shape, F32)],
        scratch_shapes=[pltpu.VMEM((group * GDN_HEADS, GDN_HEAD, GDN_HEAD), F32),
                        pltpu.VMEM((group, SUBLANE, 3 * dim), F32)],
        input_output_aliases={6: 2},
        compiler_params=_params(2),
        name="gated_deltanet",
    )(*args)


def _pad_rows_fix(buf, seq_len, n_pad):
    b, w, c = buf.shape
    return jnp.zeros((b, seq_len, c), F32).at[:, n_pad - w:n_pad].set(buf).reshape(b * seq_len, c)


def _trunk(x, states, wts, batch, seq_len, n_pad, chunk, group, carried):
    st_gla, st_rwkv, st_shift, st_gdn, st_gdn_conv, st_ffn_conv = states
    d = x.shape[1]
    rows = batch * seq_len
    if carried:
        ffn_tm = _row_tile(rows, 512)
    else:
        ffn_tm = next(t for t in (352, 192, 64) if seq_len % t == 0)
    seqs = lambda a: a.reshape(batch, seq_len, a.shape[-1])
    flat = lambda a: a.reshape(rows, a.shape[-1])
    new_gla, new_rwkv, new_gdn = (jnp.zeros(s.shape, F32) for s in (st_gla, st_rwkv, st_gdn))
    new = {k: [] for k in ('shift', 'gdn_conv', 'ffn_conv')}
    for l in range(DEPTH):
        i = l // 2
        if l % 2 == 0:
            new['shift'].append(seqs(x)[:, -1])
            x_in = x
            if carried:
                x_in = flat(seqs(x).at[:, n_pad - 1].set(st_shift[i]))
            z_gla, z_rwkv = _inproj(x_in, wts['w_in_ab'][i], (GLA_Z, RWKV_COLS))
            o_gla, new_gla = _gla(seqs(z_gla), wts['gla_gate_w2'][i], wts['gla_gate_b'][i], wts['gla_norm_g'][i],
                                  st_gla, i, new_gla, n_pad, chunk, group)
            y_rwkv, new_rwkv = _rwkv(seqs(z_rwkv), {k: wts['rwkv_' + k][i] for k in
                                                    ('mu', 'w0', 'w2', 'a0', 'a2', 'g2', 'k_k', 'k_a', 'r_k', 'ln_g',
                                                     'ln_b')},
                                     st_rwkv, i, new_rwkv, n_pad, chunk, group)
            mixed = (flat(o_gla), flat(y_rwkv))
            w_out = wts['w_out_ab'][i]
        else:
            (z,) = _inproj(x, wts['w_in_c'][i], (GDN_Z,))
            fix = seqs(_pad_rows_fix(st_gdn_conv[i], seq_len, n_pad)) if carried else None
            o, tail, new_gdn = _gdn(seqs(z), wts['gdn_conv_w'][i], wts['gdn_A_log'][i], wts['gdn_dt_bias'][i],
                                    wts['gdn_norm_g'][i], st_gdn, i, new_gdn, fix, n_pad, chunk, group)
            new['gdn_conv'].append(tail[:, SUBLANE - (GDN_CONV - 1):])
            mixed = (flat(o),)
            w_out = wts['w_out_c'][i]
        x = _outproj_ln(mixed, x, w_out, wts['ln_mix_g'][l], wts['ln_mix_b'][l], seq_len, n_pad)
        fix = _pad_rows_fix(st_ffn_conv[l], seq_len, n_pad) if carried else None
        x, gate = _conv_ffn_ln(x, wts['w_up'][l], wts['w_down'][l], wts['ffn_conv_w'][l], wts['ffn_conv_b'][l],
                               wts['ln_ffn_g'][l], wts['ln_ffn_b'][l], fix, seq_len, n_pad, ffn_tm, carried)
        if carried:
            new['ffn_conv'].append(gate.reshape(batch, seq_len, -1)[:, seq_len - (FFN_CONV - 1):])
        else:
            per_seq = seq_len // ffn_tm
            new['ffn_conv'].append(gate.reshape(batch, per_seq, SUBLANE, -1)[:, -1, SUBLANE - (FFN_CONV - 1):])
    return (x, new_gla, new_rwkv, jnp.stack(new['shift']), new_gdn, jnp.stack(new['gdn_conv']),
            jnp.stack(new['ffn_conv']))


def kernel(x_prompt, x_sample, state_gla, state_rwkv, state_rwkv_shift, state_gdn, state_gdn_conv, state_ffn_conv, meta_tokens, w_in_ab, gla_gate_w2, gla_gate_b, gla_norm_g, rwkv_mu, rwkv_w0, rwkv_w2, rwkv_a0, rwkv_a2, rwkv_g2, rwkv_k_k, rwkv_k_a, rwkv_r_k, rwkv_ln_g, rwkv_ln_b, w_out_ab, w_in_c, gdn_conv_w, gdn_A_log, gdn_dt_bias, gdn_norm_g, w_out_c, w_up, ffn_conv_w, ffn_conv_b, w_down, ln_mix_g, ln_mix_b, ln_ffn_g, ln_ffn_b):
    d = x_prompt.shape[-1]
    n_ab, n_c = w_in_ab.shape[0], w_in_c.shape[0]
    gla_cols = 2 * GLA_DK + 2 * GLA_DV + GLA_GATE_RANK
    lo0 = 2 * GLA_DK + GLA_DV
    w_ab = jnp.concatenate([w_in_ab[:, :, :lo0], w_in_ab[:, :, lo0 + GLA_GATE_RANK:gla_cols],
                            w_in_ab[:, :, lo0:lo0 + GLA_GATE_RANK],
                            jnp.zeros((n_ab, d, LANE - GLA_GATE_RANK), w_in_ab.dtype),
                            w_in_ab[:, :, gla_cols:]], axis=2).astype(BF16)
    w_c = jnp.concatenate([w_in_c, jnp.zeros((n_c, d, LANE - 2 * GDN_HEADS), w_in_c.dtype)], axis=2).astype(BF16)
    wts = {
        'w_in_ab': w_ab, 'gla_gate_w2': gla_gate_w2, 'gla_gate_b': gla_gate_b, 'gla_norm_g': gla_norm_g,
        'rwkv_mu': rwkv_mu, 'rwkv_w0': rwkv_w0, 'rwkv_w2': rwkv_w2, 'rwkv_a0': rwkv_a0, 'rwkv_a2': rwkv_a2,
        'rwkv_g2': rwkv_g2, 'rwkv_k_k': rwkv_k_k, 'rwkv_k_a': rwkv_k_a, 'rwkv_r_k': rwkv_r_k,
        'rwkv_ln_g': rwkv_ln_g, 'rwkv_ln_b': rwkv_ln_b, 'w_out_ab': w_out_ab.astype(BF16),
        'w_in_c': w_c, 'gdn_conv_w': gdn_conv_w, 'gdn_A_log': gdn_A_log, 'gdn_dt_bias': gdn_dt_bias,
        'gdn_norm_g': gdn_norm_g, 'w_out_c': w_out_c.astype(BF16),
        'w_up': w_up.astype(BF16), 'ffn_conv_w': ffn_conv_w, 'ffn_conv_b': ffn_conv_b,
        'w_down': w_down.astype(BF16),
        'ln_mix_g': ln_mix_g, 'ln_mix_b': ln_mix_b, 'ln_ffn_g': ln_ffn_g, 'ln_ffn_b': ln_ffn_b,
    }

    bp, seq, _ = x_prompt.shape
    tp = -(-(N_META + seq + SUBLANE) // PROMPT_CHUNK) * PROMPT_CHUNK
    pad_p = tp - N_META - seq
    meta = jnp.broadcast_to(meta_tokens.astype(F32)[None], (bp, N_META, d))
    xp = jnp.concatenate([jnp.zeros((bp, pad_p, d), F32), meta, x_prompt], axis=1).reshape(bp * tp, d)
    zero_state = lambda s: jnp.zeros((s.shape[0], bp) + s.shape[2:], F32)
    p_out = _trunk(xp, tuple(zero_state(s) for s in (state_gla, state_rwkv, state_rwkv_shift, state_gdn,
                                                      state_gdn_conv, state_ffn_conv)),
                   wts, bp, tp, pad_p, PROMPT_CHUNK, _group(bp, PROMPT_GROUP), carried=False)
    y_prompt = p_out[0].reshape(bp, tp, d)[:, pad_p + N_META:]

    bs, ts, _ = x_sample.shape
    tsp = -(-(ts + GDN_CONV - 1) // SUBLANE) * SUBLANE
    pad_s = tsp - ts
    xs = jnp.concatenate([jnp.zeros((bs, pad_s, d), F32), x_sample], axis=1).reshape(bs * tsp, d)
    s_out = _trunk(xs, (state_gla, state_rwkv, state_rwkv_shift, state_gdn, state_gdn_conv, state_ffn_conv),
                   wts, bs, tsp, pad_s, tsp, _group(bs, SAMPLE_GROUP), carried=True)
    y_sample = s_out[0].reshape(bs, tsp, d)[:, pad_s:]
    return (y_prompt, y_sample) + tuple(p_out[1:]) + tuple(s_out[1:])
```

```python
import functools
import math

import jax
import jax.numpy as jnp
from jax import lax
from jax.experimental import pallas as pl
from jax.experimental.pallas import tpu as pltpu

F32 = jnp.float32
BF16 = jnp.bfloat16

N_META = 16
GLA_HEADS, GLA_HK, GLA_HV = 4, 64, 128
GLA_DK, GLA_DV = GLA_HEADS * GLA_HK, GLA_HEADS * GLA_HV
GLA_GATE_RANK = 16
GLA_GATE_NORM = 16.0
GLA_SUBCHUNK = 16
RWKV_HEADS, RWKV_HEAD = 8, 64
RWKV_DIM = RWKV_HEADS * RWKV_HEAD
RWKV_COLS = 3 * RWKV_DIM + 64 + 64 + 128
RWKV_GN_EPS = 64e-5
GDN_HEADS, GDN_HEAD = 8, 128
GDN_DIM = GDN_HEADS * GDN_HEAD
GDN_CONV = 4
FFN_CONV = 3
LN_EPS = 1e-5
NORM_EPS = 1e-6
DEPTH = 4
DEEPNORM_ALPHA = (2.0 * DEPTH) ** 0.25

LANE = 128
SUBLANE = 8
PROMPT_CHUNK = 64
PROMPT_GROUP = 4
SAMPLE_GROUP = 8
VMEM_LIMIT = 56 * 1024 * 1024

GLA_Z = 3 * 512 + LANE


def _params(n_axes):
    return pltpu.CompilerParams(dimension_semantics=("arbitrary",) * n_axes,
                                vmem_limit_bytes=VMEM_LIMIT)


def _row_tile(rows, cap):
    t = cap
    while rows % t:
        t //= 2
    return t


def _seq_tile(seq_len, rows, carried):
    if carried:
        return _row_tile(rows, 256)
    return next(t for t in (704, 352, 192, 64) if seq_len % t == 0)


def _resident(shape, layer=None):
    if layer is None:
        return pl.BlockSpec(shape, lambda *_: (0,) * len(shape), pipeline_mode=pl.Buffered(1))
    return pl.BlockSpec((None,) + tuple(shape), lambda *_: (layer,) + (0,) * len(shape),
                        pipeline_mode=pl.Buffered(1))


_DIMS = {'nn': (((1,), (0,)), ((), ())),
         'nt': (((1,), (1,)), ((), ())),
         'tn': (((0,), (0,)), ((), ()))}


def _dg(a, b, kind='nn'):
    return lax.dot_general(a, b, _DIMS[kind], preferred_element_type=F32)


def _split(x, pieces):
    out = []
    for _ in range(pieces - 1):
        hi = x.astype(BF16)
        out.append(hi)
        x = x - hi.astype(F32)
    out.append(x.astype(BF16))
    return out


def _mm(a, b, kind='nn'):
    return _dg(a.astype(BF16), b.astype(BF16), kind)


def _mm3(a, b, kind='nn'):
    ah, al = _split(a, 2)
    bh, bl = _split(b, 2)
    return _dg(ah, bh, kind) + _dg(al, bh, kind) + _dg(ah, bl, kind)


def _mm_const_lhs(c, x, pieces, kind='nn'):
    cb = c.astype(BF16)
    return sum(_dg(cb, p, kind) for p in _split(x, pieces))


def _mm_const_rhs(x, c, pieces, kind='nn'):
    cb = c.astype(BF16)
    return sum(_dg(p, cb, kind) for p in _split(x, pieces))


def _bdot(a, w):
    return jnp.dot(a.astype(BF16), w, preferred_element_type=F32)


def _iota2(shape, dim):
    return lax.broadcasted_iota(jnp.int32, shape, dim)


def _tri(n, strict=False):
    r, c = _iota2((n, n), 0), _iota2((n, n), 1)
    return (c < r) if strict else (c <= r)


def _softplus(x):
    return jnp.maximum(x, 0.0) + jnp.log1p(jnp.exp(-jnp.abs(x)))


def _sigmoid(x):
    return 1.0 / (1.0 + jnp.exp(-x))


def _silu(x):
    return x * _sigmoid(x)


def _unit_lower_inverses(ms):
    n = ms[0].shape[0]
    eye = (_iota2((n, n), 0) == _iota2((n, n), 1)).astype(F32)
    xs = [eye + m for m in ms]
    ps = list(ms)
    for _ in range(int(math.log2(n)) - 1):
        ps = [_mm(p, p) for p in ps]
        xs = [x + _mm(x, p) for x, p in zip(xs, ps)]
    return xs


def _head_sums(x, width, pieces):
    rows, cols = x.shape
    nb = cols // LANE
    shift = width.bit_length() - 1
    grp = (_iota2((LANE, LANE), 0) >> shift == _iota2((LANE, LANE), 1) >> shift).astype(F32)
    xs = jnp.concatenate([x[:, i * LANE:(i + 1) * LANE] for i in range(nb)], axis=0)
    s = _mm_const_rhs(xs, grp, pieces)
    return jnp.concatenate([s[i * rows:(i + 1) * rows] for i in range(nb)], axis=1)


def _shift_rows(cur, carry, k):
    rolled = pltpu.roll(cur, k, 0)
    head = jnp.where(_iota2((SUBLANE, 1), 0) < k, pltpu.roll(carry, k, 0), rolled[:SUBLANE])
    if cur.shape[0] == SUBLANE:
        return head
    return jnp.concatenate([head, rolled[SUBLANE:]], axis=0)


def _shift_rows_seqs(cur, carries, k):
    n = len(carries)
    rows = cur.shape[0] // n
    return jnp.concatenate([_shift_rows(cur[g * rows:(g + 1) * rows], carries[g], k) for g in range(n)], axis=0) \
        if n > 1 else _shift_rows(cur, carries[0], k)


def _seq_rows(shape, seq_rows):
    return _iota2(shape, 0) & (seq_rows - 1)


def _valid_rows(row0, rows, seq_len, n_pad):
    if seq_len & (seq_len - 1) == 0:
        t = (row0 + _iota2((rows, 1), 0)) & (seq_len - 1)
    else:
        t = lax.rem(row0, seq_len) + _iota2((rows, 1), 0)
        for _ in range(-(-rows // seq_len)):
            t = jnp.where(t >= seq_len, t - seq_len, t)
    return t >= n_pad


def _layer_norm_rows(h, g, b):
    mu = jnp.mean(h, axis=-1, keepdims=True)
    d = h - mu
    var = jnp.mean(d * d, axis=-1, keepdims=True)
    return d * lax.rsqrt(var + LN_EPS) * g + b


def _inproj_kernel(x_ref, w_ref, *out_refs, widths):
    z = _bdot(x_ref[...], w_ref[...])
    off = 0
    for o_ref, wd in zip(out_refs, widths):
        o_ref[...] = z[:, off:off + wd]
        off += wd


def _inproj(x, w, layer, widths):
    rows, d = x.shape
    n = w.shape[2]
    tm = _row_tile(rows, 512)
    return pl.pallas_call(
        functools.partial(_inproj_kernel, widths=widths),
        grid=(rows // tm,),
        in_specs=[pl.BlockSpec((tm, d), lambda i: (i, 0)), _resident((d, n), layer)],
        out_specs=[pl.BlockSpec((tm, wd), lambda i: (i, 0)) for wd in widths],
        out_shape=[jax.ShapeDtypeStruct((rows, wd), F32) for wd in widths],
        compiler_params=_params(1),
        name="inproj",
    )(x, w)


def _inproj_gdn_kernel(*refs, has_fix, full_pre):
    if has_fix:
        x_ref, w_ref, cw_ref, fix_ref, act_ref, rest_ref, pre_ref, carry_ref = refs
    else:
        x_ref, w_ref, cw_ref, act_ref, rest_ref, pre_ref, carry_ref = refs
    tm = x_ref.shape[0]
    width = act_ref.shape[1]

    @pl.when(pl.program_id(0) == 0)
    def _():
        carry_ref[...] = jnp.zeros_like(carry_ref)

    xb = x_ref[...].astype(BF16)
    step = 4 * LANE
    slabs = [slice(lo, lo + step) for lo in range(0, width, step)]
    nxt = jnp.dot(xb, w_ref[:, slabs[0]], preferred_element_type=F32)
    for j, cs in enumerate(slabs):
        pre = nxt
        if j + 1 < len(slabs):
            nxt = jnp.dot(xb, w_ref[:, slabs[j + 1]], preferred_element_type=F32)
        else:
            rest_ref[...] = jnp.dot(xb, w_ref[:, width:], preferred_element_type=F32)
        if has_fix:
            pre = pre + fix_ref[:, cs]
        carry = carry_ref[:, cs]
        conv = cw_ref[GDN_CONV - 1:GDN_CONV, cs] * pre
        for tap in range(GDN_CONV - 1):
            conv = conv + cw_ref[tap:tap + 1, cs] * _shift_rows(pre, carry, GDN_CONV - 1 - tap)
        carry_ref[:, cs] = pre[tm - SUBLANE:, :]
        act_ref[:, cs] = _silu(conv)
        if full_pre:
            pre_ref[:, cs] = pre
        else:
            pre_ref[0, :, cs] = pre[tm - SUBLANE:, :]


def _inproj_gdn(x, w, layer, conv_w, fix, tm, full_pre):
    rows, d = x.shape
    n = w.shape[2]
    width = conv_w.shape[1]
    nt = rows // tm
    in_specs = [pl.BlockSpec((tm, d), lambda i: (i, 0)), _resident((d, n), layer), _resident(conv_w.shape)]
    args = [x, w, conv_w]
    if fix is not None:
        in_specs.append(pl.BlockSpec((tm, width), lambda i: (i, 0)))
        args.append(fix)
    if full_pre:
        pre_spec = pl.BlockSpec((tm, width), lambda i: (i, 0))
        pre_shape = jax.ShapeDtypeStruct((rows, width), F32)
    else:
        pre_spec = pl.BlockSpec((1, SUBLANE, width), lambda i: (i, 0, 0))
        pre_shape = jax.ShapeDtypeStruct((nt, SUBLANE, width), F32)
    return pl.pallas_call(
        functools.partial(_inproj_gdn_kernel, has_fix=fix is not None, full_pre=full_pre),
        grid=(nt,),
        in_specs=in_specs,
        out_specs=[pl.BlockSpec((tm, width), lambda i: (i, 0)),
                   pl.BlockSpec((tm, n - width), lambda i: (i, 0)),
                   pre_spec],
        out_shape=[jax.ShapeDtypeStruct((rows, width), F32),
                   jax.ShapeDtypeStruct((rows, n - width), F32),
                   pre_shape],
        scratch_shapes=[pltpu.VMEM((SUBLANE, width), F32)],
        compiler_params=_params(1),
        name="inproj_gdn",
    )(*args)


def _tail_kernel(*refs, n_mixed, seq_len, n_pad, d_ff, ff_chunk, has_fix, full_gate):
    o_refs, refs = refs[:n_mixed], refs[n_mixed:]
    if has_fix:
        (x_ref, wout_ref, g1_ref, b1_ref, wup_ref, wdn_ref, cw_ref, cb_ref, g2_ref, b2_ref, fix_ref,
         y_ref, gate_ref, carry_ref) = refs
    else:
        (x_ref, wout_ref, g1_ref, b1_ref, wup_ref, wdn_ref, cw_ref, cb_ref, g2_ref, b2_ref,
         y_ref, gate_ref, carry_ref) = refs
    i = pl.program_id(0)
    tm = x_ref.shape[0]

    @pl.when(i == 0)
    def _():
        carry_ref[...] = jnp.zeros_like(carry_ref)

    valid = _valid_rows(i * tm, tm, seq_len, n_pad)
    h = DEEPNORM_ALPHA * x_ref[...]
    off = 0
    for o_ref in o_refs:
        h = h + _bdot(o_ref[...], wout_ref[off:off + o_ref.shape[1], :])
        off += o_ref.shape[1]
    x = jnp.where(valid, _layer_norm_rows(h, g1_ref[...], b1_ref[...]), 0.0)

    xb = x.astype(BF16)
    acc = jnp.zeros(x.shape, F32)
    for c in range(d_ff // ff_chunk):
        lo = c * ff_chunk
        gate = jnp.dot(xb, wup_ref[:, lo:lo + ff_chunk], preferred_element_type=F32)
        up = jnp.dot(xb, wup_ref[:, d_ff + lo:d_ff + lo + ff_chunk], preferred_element_type=F32)
        if has_fix:
            gate = gate + fix_ref[:, lo:lo + ff_chunk]
        carry = carry_ref[:, lo:lo + ff_chunk]
        conv = (cw_ref[0:1, lo:lo + ff_chunk] * _shift_rows(gate, carry, 2)
                + cw_ref[1:2, lo:lo + ff_chunk] * _shift_rows(gate, carry, 1)
                + cw_ref[2:3, lo:lo + ff_chunk] * gate
                + cb_ref[:, lo:lo + ff_chunk])
        act = _silu(conv) * up
        acc = acc + jnp.dot(act.astype(BF16), wdn_ref[lo:lo + ff_chunk, :], preferred_element_type=F32)
        carry_ref[:, lo:lo + ff_chunk] = gate[tm - SUBLANE:, :]
        if full_gate:
            gate_ref[:, lo:lo + ff_chunk] = gate
        else:
            gate_ref[0, :, lo:lo + ff_chunk] = gate[tm - SUBLANE:, :]
    y = _layer_norm_rows(DEEPNORM_ALPHA * x + acc, g2_ref[...], b2_ref[...])
    y_ref[...] = jnp.where(valid, y, 0.0)


def _layer_tail(mixed, x, w_out, mix_layer, ln1, w_up, w_down, layer, conv_w, conv_b, ln2, fix,
                seq_len, n_pad, tm, full_gate):
    rows, d = x.shape
    d_ff = w_down.shape[1]
    ff_chunk = d_ff // 2
    nt = rows // tm
    vec = lambda a: a.reshape(1, -1)
    row_block = lambda width: pl.BlockSpec((tm, width), lambda i: (i, 0))
    in_specs = ([row_block(o.shape[1]) for o in mixed]
                + [row_block(d), _resident((w_out.shape[1], d), mix_layer), _resident((1, d)), _resident((1, d)),
                   _resident((d, 2 * d_ff), layer), _resident((d_ff, d), layer),
                   _resident((FFN_CONV, d_ff)), _resident((1, d_ff)), _resident((1, d)), _resident((1, d))])
    args = list(mixed) + [x, w_out, vec(ln1[0]), vec(ln1[1]), w_up, w_down, conv_w, vec(conv_b),
                          vec(ln2[0]), vec(ln2[1])]
    if fix is not None:
        in_specs.append(row_block(d_ff))
        args.append(fix)
    if full_gate:
        gate_spec = row_block(d_ff)
        gate_shape = jax.ShapeDtypeStruct((rows, d_ff), F32)
    else:
        gate_spec = pl.BlockSpec((1, SUBLANE, d_ff), lambda i: (i, 0, 0))
        gate_shape = jax.ShapeDtypeStruct((nt, SUBLANE, d_ff), F32)
    return pl.pallas_call(
        functools.partial(_tail_kernel, n_mixed=len(mixed), seq_len=seq_len, n_pad=n_pad, d_ff=d_ff,
                          ff_chunk=ff_chunk, has_fix=fix is not None, full_gate=full_gate),
        grid=(nt,),
        in_specs=in_specs,
        out_specs=[row_block(d), gate_spec],
        out_shape=[jax.ShapeDtypeStruct((rows, d), F32), gate_shape],
        scratch_shapes=[pltpu.VMEM((SUBLANE, d_ff), F32)],
        compiler_params=_params(1),
        name="layer_tail",
    )(*args)


def _group(batch, preferred):
    g = preferred
    while batch % g:
        g //= 2
    return g


def _state_spec(layer, group, dims):
    return pl.BlockSpec((None, group) + dims, lambda bg, c: (layer, bg) + (0,) * len(dims))


def _stacked(ref):
    g, rows, cols = ref.shape
    return ref[...].reshape(g * rows, cols)


def _seq_masks(rows, seq_rows):
    shift = seq_rows.bit_length() - 1
    r, c = _iota2((rows, rows), 0), _iota2((rows, rows), 1)
    same = (r >> shift) == (c >> shift)
    return same, same & (c <= r)


def _gla_kernel(qk_ref, v_ref, og_ref, glo_ref, w2_ref, gb_ref, ng_ref, s0_ref, buf_ref, o_ref, s_ref, st_ref,
                *, n_pad, sub):
    del buf_ref
    c = pl.program_id(1)
    group, chunk = qk_ref.shape[0], qk_ref.shape[1]
    rows = group * chunk
    nsub = chunk // sub
    chains = [(g, h) for g in range(group) for h in range(GLA_HEADS)]
    ks = [slice(h * GLA_HK, (h + 1) * GLA_HK) for h in range(GLA_HEADS)]
    vs = [slice(h * GLA_HV, (h + 1) * GLA_HV) for h in range(GLA_HEADS)]
    rs = [slice(g * chunk, (g + 1) * chunk) for g in range(group)]

    @pl.when(c == 0)
    def _():
        for i, (g, h) in enumerate(chains):
            st_ref[i] = s0_ref[g, h].T

    valid = (c * chunk + _seq_rows((rows, 1), chunk)) >= n_pad
    pre = _mm3(_stacked(glo_ref), w2_ref[...]) + gb_ref[...]
    log_a = -_softplus(-pre) * (1.0 / GLA_GATE_NORM)
    log_a = jnp.where(valid, log_a, 0.0)
    qk = _stacked(qk_ref)
    q = qk[:, :GLA_DK] * (GLA_HK ** -0.5)
    k = jnp.where(valid, qk[:, GLA_DK:], 0.0)
    same, causal = _seq_masks(rows, sub)
    b = _mm_const_lhs(causal.astype(F32), log_a, 3)
    b_tot = _mm_const_lhs(same.astype(F32), log_a, 3)
    qd = (q * jnp.exp(b)).astype(BF16)
    kd = (k * jnp.exp(-b)).astype(BF16)
    kend = (k * jnp.exp(b_tot - b)).astype(BF16)
    vb = _stacked(v_ref).astype(BF16)
    causal_c = _seq_masks(chunk, sub)[1]
    att = [jnp.where(causal_c, _dg(qd[rs[g], ks[h]], kd[rs[g], ks[h]], 'nt'), 0.0).astype(BF16) for g, h in chains]
    o = [_dg(att[i], vb[rs[g], vs[h]]) for i, (g, h) in enumerate(chains)]
    s = [st_ref[i] for i in range(len(chains))]
    o_inter = [[] for _ in chains]
    for j in range(nsub):
        for i, (g, h) in enumerate(chains):
            sl = slice(g * chunk + j * sub, g * chunk + (j + 1) * sub)
            o_inter[i].append(_dg(qd[sl, ks[h]], s[i].astype(BF16), 'nt'))
        nxt = []
        for i, (g, h) in enumerate(chains):
            sl = slice(g * chunk + j * sub, g * chunk + (j + 1) * sub)
            dec = jnp.exp(b_tot[g * chunk + j * sub:g * chunk + j * sub + 1, ks[h]])
            nxt.append(s[i] * dec + _dg(vb[sl, vs[h]], kend[sl, ks[h]], 'tn'))
        s = nxt
    og = _stacked(og_ref)
    for i, (g, h) in enumerate(chains):
        st_ref[i] = s[i]
        oh = o[i] + (jnp.concatenate(o_inter[i], axis=0) if nsub > 1 else o_inter[i][0])
        oh = oh * lax.rsqrt(jnp.mean(oh * oh, axis=-1, keepdims=True) + NORM_EPS) * ng_ref[:, vs[h]]
        o_ref[g, :, vs[h]] = oh * _silu(og[rs[g], vs[h]])

    @pl.when(c == pl.num_programs(1) - 1)
    def _():
        for i, (g, h) in enumerate(chains):
            s_ref[g, h] = st_ref[i].T


def _gla(z, w2, gate_b, norm_g, s_all, layer, s_buf, n_pad, chunk, group):
    batch, seq_len, _ = z.shape
    nc = seq_len // chunk
    sub = math.gcd(chunk, GLA_SUBCHUNK)
    row = lambda blk: (lambda bg, c: (bg, c, blk))
    const = lambda bg, c: (0, 0)
    st = _state_spec(layer, group, (GLA_HEADS, GLA_HK, GLA_HV))
    w2p = jnp.zeros((LANE, GLA_DK), F32).at[:GLA_GATE_RANK].set(w2)
    return pl.pallas_call(
        functools.partial(_gla_kernel, n_pad=n_pad, sub=sub),
        grid=(batch // group, nc),
        in_specs=[pl.BlockSpec((group, chunk, 2 * GLA_DK), row(0)),
                  pl.BlockSpec((group, chunk, GLA_DV), row(1)),
                  pl.BlockSpec((group, chunk, GLA_DV), row(2)),
                  pl.BlockSpec((group, chunk, LANE), row(3 * 512 // LANE)),
                  pl.BlockSpec((LANE, GLA_DK), const),
                  pl.BlockSpec((1, GLA_DK), const),
                  pl.BlockSpec((1, GLA_DV), const),
                  st,
                  pl.BlockSpec(memory_space=pl.ANY)],
        out_specs=[pl.BlockSpec((group, chunk, GLA_DV), row(0)), st],
        out_shape=[jax.ShapeDtypeStruct((batch, seq_len, GLA_DV), F32),
                   jax.ShapeDtypeStruct(s_buf.shape, F32)],
        scratch_shapes=[pltpu.VMEM((group * GLA_HEADS, GLA_HV, GLA_HK), F32)],
        input_output_aliases={8: 1},
        compiler_params=_params(2),
        name="gla",
    )(z, z, z, z, w2p, gate_b.reshape(1, GLA_DK), norm_g.reshape(1, GLA_DV), s_all, s_buf)


def _rwkv_kernel(z_ref, mu_ref, w0_ref, w2_ref, a0_ref, a2_ref, g2_ref, kk_ref, ka_ref, rk_ref, lg_ref, lb_ref,
                 s0_ref, buf_ref, y_ref, s_ref, st_ref, carry_ref, *, n_pad):
    del buf_ref
    c = pl.program_id(1)
    group, chunk = z_ref.shape[0], z_ref.shape[1]
    rows = group * chunk
    dim = RWKV_DIM
    chains = [(g, h) for g in range(group) for h in range(RWKV_HEADS)]
    hs = [slice(h * RWKV_HEAD, (h + 1) * RWKV_HEAD) for h in range(RWKV_HEADS)]
    rs = [slice(g * chunk, (g + 1) * chunk) for g in range(group)]

    @pl.when(c == 0)
    def _():
        for i, (g, h) in enumerate(chains):
            st_ref[i] = s0_ref[g, h]
        carry_ref[...] = jnp.zeros_like(carry_ref)

    valid = (c * chunk + _seq_rows((rows, 1), chunk)) >= n_pad
    z = _stacked(z_ref)
    z_prev = _shift_rows_seqs(z, [carry_ref[g] for g in range(group)], 1)
    for g in range(group):
        carry_ref[g] = z[(g + 1) * chunk - SUBLANE:(g + 1) * chunk]
    z = z + mu_ref[...] * (z_prev - z)
    r, kr, vr = z[:, :dim], z[:, dim:2 * dim], z[:, 2 * dim:3 * dim]
    w_lo = z[:, 3 * dim:3 * dim + 64]
    a_lo = z[:, 3 * dim + 64:3 * dim + 128]
    g_lo = z[:, 3 * dim + 128:]
    w_raw = w0_ref[...] + _mm3(jnp.tanh(w_lo), w2_ref[...])
    log_w = -jnp.exp(-_softplus(-w_raw) - 0.5)
    a_lr = _sigmoid(a0_ref[...] + _mm3(a_lo, a2_ref[...]))
    gate = _mm3(_sigmoid(g_lo), g2_ref[...])
    kx = kr * kk_ref[...]
    kk = kx * lax.rsqrt(_head_sums(kx * kx, RWKV_HEAD, 2) + NORM_EPS)
    k_mod = kr * (1.0 + (a_lr - 1.0) * ka_ref[...])
    bonus = _head_sums(r * k_mod * rk_ref[...], RWKV_HEAD, 2) * vr

    log_w = jnp.where(valid, log_w, 0.0)
    k_in = jnp.where(valid, k_mod, 0.0)
    b_in = jnp.where(valid, kk * a_lr, 0.0)
    same, causal = _seq_masks(rows, chunk)
    c_incl = _mm_const_lhs(causal.astype(F32), log_w, 3)
    c_tot = _mm_const_lhs(same.astype(F32), log_w, 3)
    e_neg = jnp.exp(-c_incl)
    e_end = jnp.exp(c_tot - c_incl)
    a_t = -kk * jnp.exp(c_incl - log_w)
    r_t = r * jnp.exp(c_incl)
    bt = (b_in * e_neg).astype(BF16)
    kt = (k_in * e_neg).astype(BF16)
    b_e, k_e = b_in * e_end, k_in * e_end
    vb = vr.astype(BF16)
    incl = _tri(chunk)
    strict = _tri(chunk, strict=True)
    n = len(chains)
    s = [st_ref[i] for i in range(n)]
    ar = [jnp.concatenate([a_t[rs[g], hs[h]], r_t[rs[g], hs[h]]], axis=0).astype(BF16) for g, h in chains]
    g_b = [_dg(ar[i], bt[rs[g], hs[h]], 'nt') for i, (g, h) in enumerate(chains)]
    g_k = [_dg(ar[i], kt[rs[g], hs[h]], 'nt') for i, (g, h) in enumerate(chains)]
    g_s = [_dg(ar[i], s[i].astype(BF16), 'nt') for i in range(n)]
    m_ab = [jnp.where(strict, g_b[i][:chunk], 0.0) for i in range(n)]
    rhs = [g_s[i][:chunk] + _mm(jnp.where(strict, g_k[i][:chunk], 0.0), vb[rs[g], hs[h]])
           for i, (g, h) in enumerate(chains)]
    t_inv = _unit_lower_inverses(m_ab)
    u = [_mm(t_inv[i], rhs[i]) for i in range(n)]
    ys = [g_s[i][chunk:] + _mm(jnp.where(incl, g_b[i][chunk:], 0.0), u[i])
          + _mm(jnp.where(incl, g_k[i][chunk:], 0.0), vb[rs[g], hs[h]]) for i, (g, h) in enumerate(chains)]
    for i, (g, h) in enumerate(chains):
        uv = jnp.concatenate([u[i], vr[rs[g], hs[h]]], axis=0)
        bk = jnp.concatenate([b_e[rs[g], hs[h]], k_e[rs[g], hs[h]]], axis=0)
        w_end = jnp.exp(c_tot[g * chunk:g * chunk + 1, hs[h]])
        st_ref[i] = s[i] * w_end + _mm(uv, bk, 'tn')
    per_seq = [jnp.concatenate(ys[g * RWKV_HEADS:(g + 1) * RWKV_HEADS], axis=1) for g in range(group)]
    y = jnp.concatenate(per_seq, axis=0) if group > 1 else per_seq[0]
    mean = _head_sums(y, RWKV_HEAD, 2) * (1.0 / RWKV_HEAD)
    d = y - mean
    var = _head_sums(d * d, RWKV_HEAD, 2) * (1.0 / RWKV_HEAD)
    y = d * lax.rsqrt(var + RWKV_GN_EPS) * lg_ref[...] + lb_ref[...]
    y_ref[...] = ((y + bonus) * gate).reshape(group, chunk, dim)

    @pl.when(c == pl.num_programs(1) - 1)
    def _():
        for i, (g, h) in enumerate(chains):
            s_ref[g, h] = st_ref[i]


def _rwkv(z, p, s_all, layer, s_buf, n_pad, chunk, group):
    batch, seq_len, _ = z.shape
    nc = seq_len // chunk
    dim = RWKV_DIM
    row = lambda bg, c: (bg, c, 0)
    const = lambda bg, c: (0, 0)
    st = _state_spec(layer, group, (RWKV_HEADS, RWKV_HEAD, RWKV_HEAD))
    vec = lambda a: a.reshape(1, -1)
    small = [vec(p['mu']), vec(p['w0']), p['w2'], vec(p['a0']), p['a2'], p['g2'], vec(p['k_k']), vec(p['k_a']),
             vec(p['r_k']), vec(p['ln_g']), vec(p['ln_b'])]
    return pl.pallas_call(
        functools.partial(_rwkv_kernel, n_pad=n_pad),
        grid=(batch // group, nc),
        in_specs=([pl.BlockSpec((group, chunk, RWKV_COLS), row)]
                  + [pl.BlockSpec(a.shape, const) for a in small]
                  + [st, pl.BlockSpec(memory_space=pl.ANY)]),
        out_specs=[pl.BlockSpec((group, chunk, dim), row), st],
        out_shape=[jax.ShapeDtypeStruct((batch, seq_len, dim), F32),
                   jax.ShapeDtypeStruct(s_buf.shape, F32)],
        scratch_shapes=[pltpu.VMEM((group * RWKV_HEADS, RWKV_HEAD, RWKV_HEAD), F32),
                        pltpu.VMEM((group, SUBLANE, RWKV_COLS), F32)],
        input_output_aliases={len(small) + 2: 1},
        compiler_params=_params(2),
        name="rwkv7",
    )(z, *small, s_all, s_buf)


def _gdn_kernel(qkv_ref, rest_ref, alog_ref, dt_ref, ng_ref, s0_ref, buf_ref, o_ref, s_ref, st_ref, *, n_pad):
    del buf_ref
    c = pl.program_id(1)
    group, chunk = qkv_ref.shape[0], qkv_ref.shape[1]
    rows = group * chunk
    dim = GDN_DIM
    chains = [(g, h) for g in range(group) for h in range(GDN_HEADS)]
    hs = [slice(h * GDN_HEAD, (h + 1) * GDN_HEAD) for h in range(GDN_HEADS)]
    rs = [slice(g * chunk, (g + 1) * chunk) for g in range(group)]

    @pl.when(c == 0)
    def _():
        for i, (g, h) in enumerate(chains):
            st_ref[i] = s0_ref[g, h]

    valid = (c * chunk + _seq_rows((rows, 1), chunk)) >= n_pad
    qkv = _stacked(qkv_ref)
    rest = _stacked(rest_ref)
    zg = rest[:, :dim]
    lo = rest[:, dim:]
    beta_all = _sigmoid(lo)
    g_all = -jnp.exp(alog_ref[...]) * _softplus(lo + dt_ref[...])
    g_all = jnp.where(valid, g_all, 0.0)

    incl = _tri(chunk)
    strict = _tri(chunk, strict=True)
    causal = _seq_masks(rows, chunk)[1]
    gam_all = _mm_const_lhs(causal.astype(F32), g_all, 3)
    shift = chunk.bit_length() - 1
    pick = (_iota2((GDN_HEADS * chunk, LANE), 1)
            == GDN_HEADS + (_iota2((GDN_HEADS * chunk, LANE), 0) >> shift)).astype(F32)
    gam_cols = [_mm_const_lhs(pick, gam_all[rs[g]], 3, 'nt') for g in range(group)]
    q_sq = _head_sums(jnp.square(qkv[:, :dim]), GDN_HEAD, 2)
    k_sq = _head_sums(jnp.square(qkv[:, dim:2 * dim]), GDN_HEAD, 2)
    q_all = qkv[:, :dim] * lax.rsqrt(q_sq + NORM_EPS) * (GDN_HEAD ** -0.5)
    k_all = jnp.where(valid, qkv[:, dim:2 * dim] * lax.rsqrt(k_sq + NORM_EPS), 0.0)
    v_all = qkv[:, 2 * dim:]
    n = len(chains)
    gam = [gam_all[rs[g], GDN_HEADS + h:GDN_HEADS + h + 1] for g, h in chains]
    beta = [beta_all[rs[g], h:h + 1] for g, h in chains]
    kq = [jnp.concatenate([k_all[rs[g], hs[h]], q_all[rs[g], hs[h]]], axis=0).astype(BF16) for g, h in chains]
    dec = [jnp.where(incl, jnp.exp(jnp.where(incl, gam[i] - gam_cols[g][h * chunk:(h + 1) * chunk], 0.0)), 0.0)
           for i, (g, h) in enumerate(chains)]
    gram = [_dg(kq[i], k_all[rs[g], hs[h]].astype(BF16), 'nt') for i, (g, h) in enumerate(chains)]
    m = [jnp.where(strict, gram[i][:chunk] * dec[i], 0.0) * (-beta[i]) for i in range(n)]
    t_inv = _unit_lower_inverses(m)
    e_gam = [jnp.exp(gam[i]) for i in range(n)]
    rhs = [jnp.concatenate([v_all[rs[g], hs[h]] * beta[i], k_all[rs[g], hs[h]] * (beta[i] * e_gam[i])], axis=1)
           for i, (g, h) in enumerate(chains)]
    uw = [_mm(t_inv[i], rhs[i]) for i in range(n)]
    s = [st_ref[i] for i in range(n)]
    sb = [s[i].astype(BF16) for i in range(n)]
    delta = [uw[i][:, :GDN_HEAD] - _mm(uw[i][:, GDN_HEAD:], sb[i]) for i in range(n)]
    o = [_mm(q_all[rs[g], hs[h]] * e_gam[i], sb[i]) + _mm(gram[i][chunk:] * dec[i], delta[i])
         for i, (g, h) in enumerate(chains)]
    for i, (g, h) in enumerate(chains):
        g_end = gam[i][chunk - 1:chunk, :]
        st_ref[i] = s[i] * jnp.exp(g_end) + _mm(k_all[rs[g], hs[h]] * jnp.exp(g_end - gam[i]), delta[i], 'tn')
        oh = o[i] * lax.rsqrt(jnp.mean(o[i] * o[i], axis=-1, keepdims=True) + NORM_EPS) * ng_ref[...]
        o_ref[g, :, hs[h]] = oh * _silu(zg[rs[g], hs[h]])

    @pl.when(c == pl.num_programs(1) - 1)
    def _():
        for i, (g, h) in enumerate(chains):
            s_ref[g, h] = st_ref[i]


def _gdn(qkv, rest, a_log, dt_bias, norm_g, s_all, layer, s_buf, n_pad, chunk, group):
    batch, seq_len, _ = qkv.shape
    nc = seq_len // chunk
    dim = GDN_DIM
    row = lambda bg, c: (bg, c, 0)
    const = lambda bg, c: (0, 0)
    st = _state_spec(layer, group, (GDN_HEADS, GDN_HEAD, GDN_HEAD))
    pad_lo = lambda a: jnp.zeros((1, LANE), F32).at[0, GDN_HEADS:2 * GDN_HEADS].set(a)
    return pl.pallas_call(
        functools.partial(_gdn_kernel, n_pad=n_pad),
        grid=(batch // group, nc),
        in_specs=[pl.BlockSpec((group, chunk, 3 * dim), row),
                  pl.BlockSpec((group, chunk, dim + LANE), row),
                  pl.BlockSpec((1, LANE), const),
                  pl.BlockSpec((1, LANE), const),
                  pl.BlockSpec((1, GDN_HEAD), const),
                  st,
                  pl.BlockSpec(memory_space=pl.ANY)],
        out_specs=[pl.BlockSpec((group, chunk, dim), row), st],
        out_shape=[jax.ShapeDtypeStruct((batch, seq_len, dim), F32),
                   jax.ShapeDtypeStruct(s_buf.shape, F32)],
        scratch_shapes=[pltpu.VMEM((group * GDN_HEADS, GDN_HEAD, GDN_HEAD), F32)],
        input_output_aliases={6: 1},
        compiler_params=_params(2),
        name="gated_deltanet",
    )(qkv, rest, pad_lo(a_log), pad_lo(dt_bias), norm_g.reshape(1, GDN_HEAD), s_all, s_buf)


def _pad_rows_fix(buf, seq_len, n_pad):
    b, w, c = buf.shape
    return jnp.zeros((b, seq_len, c), F32).at[:, n_pad - w:n_pad].set(buf).reshape(b * seq_len, c)


def _trunk(x, states, wts, batch, seq_len, n_pad, chunk, group, carried):
    st_gla, st_rwkv, st_shift, st_gdn, st_gdn_conv, st_ffn_conv = states
    d = x.shape[1]
    rows = batch * seq_len
    tm = _seq_tile(seq_len, rows, carried)
    tile_tail = lambda a, w: a.reshape(batch, seq_len // tm, SUBLANE, -1)[:, -1, SUBLANE - w:]
    seq_tail = lambda a, w: a.reshape(batch, seq_len, -1)[:, seq_len - w:]
    conv_rows = seq_tail if carried else tile_tail
    seqs = lambda a: a.reshape(batch, seq_len, a.shape[-1])
    flat = lambda a: a.reshape(rows, a.shape[-1])
    new_gla, new_rwkv, new_gdn = (jnp.zeros(s.shape, F32) for s in (st_gla, st_rwkv, st_gdn))
    new = {k: [] for k in ('shift', 'gdn_conv', 'ffn_conv')}
    for l in range(DEPTH):
        i = l // 2
        if l % 2 == 0:
            new['shift'].append(seqs(x)[:, -1])
            x_in = x
            if carried:
                x_in = flat(seqs(x).at[:, n_pad - 1].set(st_shift[i]))
            z_gla, z_rwkv = _inproj(x_in, wts['w_in_ab'], i, (GLA_Z, RWKV_COLS))
            o_gla, new_gla = _gla(seqs(z_gla), wts['gla_gate_w2'][i], wts['gla_gate_b'][i], wts['gla_norm_g'][i],
                                  st_gla, i, new_gla, n_pad, chunk, group)
            y_rwkv, new_rwkv = _rwkv(seqs(z_rwkv), {k: wts['rwkv_' + k][i] for k in
                                                    ('mu', 'w0', 'w2', 'a0', 'a2', 'g2', 'k_k', 'k_a', 'r_k', 'ln_g',
                                                     'ln_b')},
                                     st_rwkv, i, new_rwkv, n_pad, chunk, group)
            mixed = (flat(o_gla), flat(y_rwkv))
            w_out = wts['w_out_ab']
        else:
            fix = _pad_rows_fix(st_gdn_conv[i], seq_len, n_pad) if carried else None
            qkv, rest, pre = _inproj_gdn(x, wts['w_in_c'], i, wts['gdn_conv_w'][i], fix, tm, carried)
            new['gdn_conv'].append(conv_rows(pre, GDN_CONV - 1))
            o, new_gdn = _gdn(seqs(qkv), seqs(rest), wts['gdn_A_log'][i], wts['gdn_dt_bias'][i],
                              wts['gdn_norm_g'][i], st_gdn, i, new_gdn, n_pad, chunk, group)
            mixed = (flat(o),)
            w_out = wts['w_out_c']
        fix = _pad_rows_fix(st_ffn_conv[l], seq_len, n_pad) if carried else None
        x, gate = _layer_tail(mixed, x, w_out, i, (wts['ln_mix_g'][l], wts['ln_mix_b'][l]),
                              wts['w_up'], wts['w_down'], l, wts['ffn_conv_w'][l], wts['ffn_conv_b'][l],
                              (wts['ln_ffn_g'][l], wts['ln_ffn_b'][l]), fix, seq_len, n_pad, tm, carried)
        new['ffn_conv'].append(conv_rows(gate, FFN_CONV - 1))
    return (x, new_gla, new_rwkv, jnp.stack(new['shift']), new_gdn, jnp.stack(new['gdn_conv']),
            jnp.stack(new['ffn_conv']))


def kernel(x_prompt, x_sample, state_gla, state_rwkv, state_rwkv_shift, state_gdn, state_gdn_conv, state_ffn_conv, meta_tokens, w_in_ab, gla_gate_w2, gla_gate_b, gla_norm_g, rwkv_mu, rwkv_w0, rwkv_w2, rwkv_a0, rwkv_a2, rwkv_g2, rwkv_k_k, rwkv_k_a, rwkv_r_k, rwkv_ln_g, rwkv_ln_b, w_out_ab, w_in_c, gdn_conv_w, gdn_A_log, gdn_dt_bias, gdn_norm_g, w_out_c, w_up, ffn_conv_w, ffn_conv_b, w_down, ln_mix_g, ln_mix_b, ln_ffn_g, ln_ffn_b):
    d = x_prompt.shape[-1]
    n_ab, n_c = w_in_ab.shape[0], w_in_c.shape[0]
    gla_cols = 2 * GLA_DK + 2 * GLA_DV + GLA_GATE_RANK
    lo0 = 2 * GLA_DK + GLA_DV
    w_ab = jnp.concatenate([w_in_ab[:, :, :lo0], w_in_ab[:, :, lo0 + GLA_GATE_RANK:gla_cols],
                            w_in_ab[:, :, lo0:lo0 + GLA_GATE_RANK],
                            jnp.zeros((n_ab, d, LANE - GLA_GATE_RANK), w_in_ab.dtype),
                            w_in_ab[:, :, gla_cols:]], axis=2).astype(BF16)
    w_c = jnp.concatenate([w_in_c, jnp.zeros((n_c, d, LANE - 2 * GDN_HEADS), w_in_c.dtype)], axis=2).astype(BF16)
    wts = {
        'w_in_ab': w_ab, 'gla_gate_w2': gla_gate_w2, 'gla_gate_b': gla_gate_b, 'gla_norm_g': gla_norm_g,
        'rwkv_mu': rwkv_mu, 'rwkv_w0': rwkv_w0, 'rwkv_w2': rwkv_w2, 'rwkv_a0': rwkv_a0, 'rwkv_a2': rwkv_a2,
        'rwkv_g2': rwkv_g2, 'rwkv_k_k': rwkv_k_k, 'rwkv_k_a': rwkv_k_a, 'rwkv_r_k': rwkv_r_k,
        'rwkv_ln_g': rwkv_ln_g, 'rwkv_ln_b': rwkv_ln_b, 'w_out_ab': w_out_ab.astype(BF16),
        'w_in_c': w_c, 'gdn_conv_w': gdn_conv_w, 'gdn_A_log': gdn_A_log, 'gdn_dt_bias': gdn_dt_bias,
        'gdn_norm_g': gdn_norm_g, 'w_out_c': w_out_c.astype(BF16),
        'w_up': w_up.astype(BF16), 'ffn_conv_w': ffn_conv_w, 'ffn_conv_b': ffn_conv_b,
        'w_down': w_down.astype(BF16),
        'ln_mix_g': ln_mix_g, 'ln_mix_b': ln_mix_b, 'ln_ffn_g': ln_ffn_g, 'ln_ffn_b': ln_ffn_b,
    }

    bp, seq, _ = x_prompt.shape
    tp = -(-(N_META + seq + SUBLANE) // PROMPT_CHUNK) * PROMPT_CHUNK
    pad_p = tp - N_META - seq
    meta = jnp.broadcast_to(meta_tokens.astype(F32)[None], (bp, N_META, d))
    xp = jnp.concatenate([jnp.zeros((bp, pad_p, d), F32), meta, x_prompt], axis=1).reshape(bp * tp, d)
    zero_state = lambda s: jnp.zeros((s.shape[0], bp) + s.shape[2:], F32)
    p_out = _trunk(xp, tuple(zero_state(s) for s in (state_gla, state_rwkv, state_rwkv_shift, state_gdn,
                                                      state_gdn_conv, state_ffn_conv)),
                   wts, bp, tp, pad_p, PROMPT_CHUNK, _group(bp, PROMPT_GROUP), carried=False)
    y_prompt = p_out[0].reshape(bp, tp, d)[:, pad_p + N_META:]

    bs, ts, _ = x_sample.shape
    tsp = -(-(ts + GDN_CONV - 1) // SUBLANE) * SUBLANE
    pad_s = tsp - ts
    xs = jnp.concatenate([jnp.zeros((bs, pad_s, d), F32), x_sample], axis=1).reshape(bs * tsp, d)
    s_out = _trunk(xs, (state_gla, state_rwkv, state_rwkv_shift, state_gdn, state_gdn_conv, state_ffn_conv),
                   wts, bs, tsp, pad_s, tsp, _group(bs, SAMPLE_GROUP), carried=True)
    y_sample = s_out[0].reshape(bs, tsp, d)[:, pad_s:]
    return (y_prompt, y_sample) + tuple(p_out[1:]) + tuple(s_out[1:])
```

```python
import functools
import math

import jax
import jax.numpy as jnp
from jax import lax
from jax.experimental import pallas as pl
from jax.experimental.pallas import tpu as pltpu

F32 = jnp.float32
BF16 = jnp.bfloat16

N_META = 16
GLA_HEADS, GLA_HK, GLA_HV = 4, 64, 128
GLA_DK, GLA_DV = GLA_HEADS * GLA_HK, GLA_HEADS * GLA_HV
GLA_GATE_RANK = 16
GLA_GATE_NORM = 16.0
GLA_SUBCHUNK = 16
RWKV_HEADS, RWKV_HEAD = 8, 64
RWKV_DIM = RWKV_HEADS * RWKV_HEAD
RWKV_COLS = 3 * RWKV_DIM + 64 + 64 + 128
RWKV_GN_EPS = 64e-5
GDN_HEADS, GDN_HEAD = 8, 128
GDN_DIM = GDN_HEADS * GDN_HEAD
GDN_CONV = 4
FFN_CONV = 3
LN_EPS = 1e-5
NORM_EPS = 1e-6
DEPTH = 4
DEEPNORM_ALPHA = (2.0 * DEPTH) ** 0.25

LANE = 128
SUBLANE = 8
PROMPT_CHUNK = 64
PROMPT_GROUP = 4
SAMPLE_GROUP = 8
VMEM_LIMIT = 56 * 1024 * 1024

GLA_Z = 3 * 512 + LANE


def _params(n_axes):
    return pltpu.CompilerParams(dimension_semantics=("arbitrary",) * n_axes,
                                vmem_limit_bytes=VMEM_LIMIT)


def _row_tile(rows, cap):
    t = cap
    while rows % t:
        t //= 2
    return t


def _seq_tile(seq_len, rows, carried):
    if carried:
        return _row_tile(rows, 256)
    return next(t for t in (704, 352, 192, 64) if seq_len % t == 0)


def _resident(shape, layer=None):
    if layer is None:
        return pl.BlockSpec(shape, lambda *_: (0,) * len(shape), pipeline_mode=pl.Buffered(1))
    return pl.BlockSpec((None,) + tuple(shape), lambda *_: (layer,) + (0,) * len(shape),
                        pipeline_mode=pl.Buffered(1))


_DIMS = {'nn': (((1,), (0,)), ((), ())),
         'nt': (((1,), (1,)), ((), ())),
         'tn': (((0,), (0,)), ((), ()))}


def _dg(a, b, kind='nn'):
    return lax.dot_general(a, b, _DIMS[kind], preferred_element_type=F32)


def _split(x, pieces):
    out = []
    for _ in range(pieces - 1):
        hi = x.astype(BF16)
        out.append(hi)
        x = x - hi.astype(F32)
    out.append(x.astype(BF16))
    return out


def _mm(a, b, kind='nn'):
    return _dg(a.astype(BF16), b.astype(BF16), kind)


def _mm3(a, b, kind='nn'):
    ah, al = _split(a, 2)
    bh, bl = _split(b, 2)
    return _dg(ah, bh, kind) + _dg(al, bh, kind) + _dg(ah, bl, kind)


def _mm_const_lhs(c, x, pieces, kind='nn'):
    cb = c.astype(BF16)
    return sum(_dg(cb, p, kind) for p in _split(x, pieces))


def _mm_const_rhs(x, c, pieces, kind='nn'):
    cb = c.astype(BF16)
    return sum(_dg(p, cb, kind) for p in _split(x, pieces))


def _bdot(a, w):
    return jnp.dot(a.astype(BF16), w, preferred_element_type=F32)


def _iota2(shape, dim):
    return lax.broadcasted_iota(jnp.int32, shape, dim)


def _tri(n, strict=False):
    r, c = _iota2((n, n), 0), _iota2((n, n), 1)
    return (c < r) if strict else (c <= r)


def _softplus(x):
    return jnp.maximum(x, 0.0) + jnp.log1p(jnp.exp(-jnp.abs(x)))


def _sigmoid(x):
    return 1.0 / (1.0 + jnp.exp(-x))


def _silu(x):
    return x * _sigmoid(x)


def _unit_lower_inverses(ms):
    n = ms[0].shape[0]
    eye = (_iota2((n, n), 0) == _iota2((n, n), 1)).astype(F32)
    xs = [eye + m for m in ms]
    ps = list(ms)
    for _ in range(int(math.log2(n)) - 1):
        ps = [_mm(p, p) for p in ps]
        xs = [x + _mm(x, p) for x, p in zip(xs, ps)]
    return xs


def _pair_blocks(x, split):
    first = _iota2(x.shape, 1) < split
    return jnp.concatenate([jnp.where(first, x, 0.0), jnp.where(first, 0.0, x)], axis=0)


def _unit_lower_inverse_pairs(ms):
    n = ms[0].shape[0]
    eye = ((_iota2((n, 2 * n), 1) & (n - 1)) == _iota2((n, 2 * n), 0)).astype(F32)
    xs = [eye + m for m in ms]
    ps = list(ms)
    bds = [_pair_blocks(p, n).astype(BF16) for p in ps]
    for _ in range(int(math.log2(n)) - 1):
        ps = [_mm(p, bd) for p, bd in zip(ps, bds)]
        bds = [_pair_blocks(p, n).astype(BF16) for p in ps]
        xs = [x + _mm(x, bd) for x, bd in zip(xs, bds)]
    return xs


def _head_sums(x, width, pieces):
    rows, cols = x.shape
    nb = cols // LANE
    shift = width.bit_length() - 1
    grp = (_iota2((LANE, LANE), 0) >> shift == _iota2((LANE, LANE), 1) >> shift).astype(F32)
    xs = jnp.concatenate([x[:, i * LANE:(i + 1) * LANE] for i in range(nb)], axis=0)
    s = _mm_const_rhs(xs, grp, pieces)
    return jnp.concatenate([s[i * rows:(i + 1) * rows] for i in range(nb)], axis=1)


def _shift_rows(cur, carry, k):
    rolled = pltpu.roll(cur, k, 0)
    head = jnp.where(_iota2((SUBLANE, 1), 0) < k, pltpu.roll(carry, k, 0), rolled[:SUBLANE])
    if cur.shape[0] == SUBLANE:
        return head
    return jnp.concatenate([head, rolled[SUBLANE:]], axis=0)


def _shift_rows_seqs(cur, carries, k):
    n = len(carries)
    rows = cur.shape[0] // n
    return jnp.concatenate([_shift_rows(cur[g * rows:(g + 1) * rows], carries[g], k) for g in range(n)], axis=0) \
        if n > 1 else _shift_rows(cur, carries[0], k)


def _seq_rows(shape, seq_rows):
    return _iota2(shape, 0) & (seq_rows - 1)


def _valid_rows(row0, rows, seq_len, n_pad):
    if seq_len & (seq_len - 1) == 0:
        t = (row0 + _iota2((rows, 1), 0)) & (seq_len - 1)
    else:
        t = lax.rem(row0, seq_len) + _iota2((rows, 1), 0)
        for _ in range(-(-rows // seq_len)):
            t = jnp.where(t >= seq_len, t - seq_len, t)
    return t >= n_pad


def _layer_norm_rows(h, g, b):
    mu = jnp.mean(h, axis=-1, keepdims=True)
    d = h - mu
    var = jnp.mean(d * d, axis=-1, keepdims=True)
    return d * lax.rsqrt(var + LN_EPS) * g + b


def _inproj_kernel(x_ref, w_ref, *out_refs, widths):
    z = _bdot(x_ref[...], w_ref[...])
    off = 0
    for o_ref, wd in zip(out_refs, widths):
        o_ref[...] = z[:, off:off + wd]
        off += wd


def _inproj(x, w, layer, widths):
    rows, d = x.shape
    n = w.shape[2]
    tm = _row_tile(rows, 512)
    return pl.pallas_call(
        functools.partial(_inproj_kernel, widths=widths),
        grid=(rows // tm,),
        in_specs=[pl.BlockSpec((tm, d), lambda i: (i, 0)), _resident((d, n), layer)],
        out_specs=[pl.BlockSpec((tm, wd), lambda i: (i, 0)) for wd in widths],
        out_shape=[jax.ShapeDtypeStruct((rows, wd), F32) for wd in widths],
        compiler_params=_params(1),
        name="inproj",
    )(x, w)


def _inproj_gdn_kernel(*refs, has_fix, full_pre):
    if has_fix:
        x_ref, w_ref, cw_ref, fix_ref, act_ref, rest_ref, pre_ref, carry_ref = refs
    else:
        x_ref, w_ref, cw_ref, act_ref, rest_ref, pre_ref, carry_ref = refs
    tm = x_ref.shape[0]
    width = act_ref.shape[1]

    @pl.when(pl.program_id(0) == 0)
    def _():
        carry_ref[...] = jnp.zeros_like(carry_ref)

    xb = x_ref[...].astype(BF16)
    step = 4 * LANE
    for lo in range(0, width, step):
        cs = slice(lo, lo + step)
        pre = jnp.dot(xb, w_ref[:, cs], preferred_element_type=F32)
        if has_fix:
            pre = pre + fix_ref[:, cs]
        carry = carry_ref[:, cs]
        conv = cw_ref[GDN_CONV - 1:GDN_CONV, cs] * pre
        for tap in range(GDN_CONV - 1):
            conv = conv + cw_ref[tap:tap + 1, cs] * _shift_rows(pre, carry, GDN_CONV - 1 - tap)
        carry_ref[:, cs] = pre[tm - SUBLANE:, :]
        act_ref[:, cs] = _silu(conv)
        if full_pre:
            pre_ref[:, cs] = pre
        else:
            pre_ref[0, :, cs] = pre[tm - SUBLANE:, :]
    rest_ref[...] = jnp.dot(xb, w_ref[:, width:], preferred_element_type=F32)


def _inproj_gdn(x, w, layer, conv_w, fix, tm, full_pre):
    rows, d = x.shape
    n = w.shape[2]
    width = conv_w.shape[1]
    nt = rows // tm
    in_specs = [pl.BlockSpec((tm, d), lambda i: (i, 0)), _resident((d, n), layer), _resident(conv_w.shape)]
    args = [x, w, conv_w]
    if fix is not None:
        in_specs.append(pl.BlockSpec((tm, width), lambda i: (i, 0)))
        args.append(fix)
    if full_pre:
        pre_spec = pl.BlockSpec((tm, width), lambda i: (i, 0))
        pre_shape = jax.ShapeDtypeStruct((rows, width), F32)
    else:
        pre_spec = pl.BlockSpec((1, SUBLANE, width), lambda i: (i, 0, 0))
        pre_shape = jax.ShapeDtypeStruct((nt, SUBLANE, width), F32)
    return pl.pallas_call(
        functools.partial(_inproj_gdn_kernel, has_fix=fix is not None, full_pre=full_pre),
        grid=(nt,),
        in_specs=in_specs,
        out_specs=[pl.BlockSpec((tm, width), lambda i: (i, 0)),
                   pl.BlockSpec((tm, n - width), lambda i: (i, 0)),
                   pre_spec],
        out_shape=[jax.ShapeDtypeStruct((rows, width), F32),
                   jax.ShapeDtypeStruct((rows, n - width), F32),
                   pre_shape],
        scratch_shapes=[pltpu.VMEM((SUBLANE, width), F32)],
        compiler_params=_params(1),
        name="inproj_gdn",
    )(*args)


def _tail_kernel(*refs, n_mixed, seq_len, n_pad, d_ff, ff_chunk, has_fix, full_gate):
    o_refs, refs = refs[:n_mixed], refs[n_mixed:]
    if has_fix:
        (x_ref, wout_ref, g1_ref, b1_ref, wup_ref, wdn_ref, cw_ref, cb_ref, g2_ref, b2_ref, fix_ref,
         y_ref, gate_ref, carry_ref) = refs
    else:
        (x_ref, wout_ref, g1_ref, b1_ref, wup_ref, wdn_ref, cw_ref, cb_ref, g2_ref, b2_ref,
         y_ref, gate_ref, carry_ref) = refs
    i = pl.program_id(0)
    tm = x_ref.shape[0]

    @pl.when(i == 0)
    def _():
        carry_ref[...] = jnp.zeros_like(carry_ref)

    valid = _valid_rows(i * tm, tm, seq_len, n_pad)
    h = DEEPNORM_ALPHA * x_ref[...]
    off = 0
    for o_ref in o_refs:
        h = h + _bdot(o_ref[...], wout_ref[off:off + o_ref.shape[1], :])
        off += o_ref.shape[1]
    x = jnp.where(valid, _layer_norm_rows(h, g1_ref[...], b1_ref[...]), 0.0)

    xb = x.astype(BF16)
    acc = jnp.zeros(x.shape, F32)
    for c in range(d_ff // ff_chunk):
        lo = c * ff_chunk
        gate = jnp.dot(xb, wup_ref[:, lo:lo + ff_chunk], preferred_element_type=F32)
        up = jnp.dot(xb, wup_ref[:, d_ff + lo:d_ff + lo + ff_chunk], preferred_element_type=F32)
        if has_fix:
            gate = gate + fix_ref[:, lo:lo + ff_chunk]
        carry = carry_ref[:, lo:lo + ff_chunk]
        conv = (cw_ref[0:1, lo:lo + ff_chunk] * _shift_rows(gate, carry, 2)
                + cw_ref[1:2, lo:lo + ff_chunk] * _shift_rows(gate, carry, 1)
                + cw_ref[2:3, lo:lo + ff_chunk] * gate
                + cb_ref[:, lo:lo + ff_chunk])
        act = _silu(conv) * up
        acc = acc + jnp.dot(act.astype(BF16), wdn_ref[lo:lo + ff_chunk, :], preferred_element_type=F32)
        carry_ref[:, lo:lo + ff_chunk] = gate[tm - SUBLANE:, :]
        if full_gate:
            gate_ref[:, lo:lo + ff_chunk] = gate
        else:
            gate_ref[0, :, lo:lo + ff_chunk] = gate[tm - SUBLANE:, :]
    y = _layer_norm_rows(DEEPNORM_ALPHA * x + acc, g2_ref[...], b2_ref[...])
    y_ref[...] = jnp.where(valid, y, 0.0)


def _layer_tail(mixed, x, w_out, mix_layer, ln1, w_up, w_down, layer, conv_w, conv_b, ln2, fix,
                seq_len, n_pad, tm, full_gate):
    rows, d = x.shape
    d_ff = w_down.shape[1]
    ff_chunk = d_ff // 2
    nt = rows // tm
    vec = lambda a: a.reshape(1, -1)
    row_block = lambda width: pl.BlockSpec((tm, width), lambda i: (i, 0))
    in_specs = ([row_block(o.shape[1]) for o in mixed]
                + [row_block(d), _resident((w_out.shape[1], d), mix_layer), _resident((1, d)), _resident((1, d)),
                   _resident((d, 2 * d_ff), layer), _resident((d_ff, d), layer),
                   _resident((FFN_CONV, d_ff)), _resident((1, d_ff)), _resident((1, d)), _resident((1, d))])
    args = list(mixed) + [x, w_out, vec(ln1[0]), vec(ln1[1]), w_up, w_down, conv_w, vec(conv_b),
                          vec(ln2[0]), vec(ln2[1])]
    if fix is not None:
        in_specs.append(row_block(d_ff))
        args.append(fix)
    if full_gate:
        gate_spec = row_block(d_ff)
        gate_shape = jax.ShapeDtypeStruct((rows, d_ff), F32)
    else:
        gate_spec = pl.BlockSpec((1, SUBLANE, d_ff), lambda i: (i, 0, 0))
        gate_shape = jax.ShapeDtypeStruct((nt, SUBLANE, d_ff), F32)
    return pl.pallas_call(
        functools.partial(_tail_kernel, n_mixed=len(mixed), seq_len=seq_len, n_pad=n_pad, d_ff=d_ff,
                          ff_chunk=ff_chunk, has_fix=fix is not None, full_gate=full_gate),
        grid=(nt,),
        in_specs=in_specs,
        out_specs=[row_block(d), gate_spec],
        out_shape=[jax.ShapeDtypeStruct((rows, d), F32), gate_shape],
        scratch_shapes=[pltpu.VMEM((SUBLANE, d_ff), F32)],
        compiler_params=_params(1),
        name="layer_tail",
    )(*args)


def _group(batch, preferred):
    g = preferred
    while batch % g:
        g //= 2
    return g


def _state_spec(layer, group, dims):
    return pl.BlockSpec((None, group) + dims, lambda bg, c: (layer, bg) + (0,) * len(dims))


def _alias_io(s_buf, n_inputs, out_index):
    if s_buf is None:
        return [], [], {}
    return [pl.BlockSpec(memory_space=pl.ANY)], [s_buf], {n_inputs: out_index}


def _stacked(ref):
    g, rows, cols = ref.shape
    return ref[...].reshape(g * rows, cols)


def _seq_masks(rows, seq_rows):
    shift = seq_rows.bit_length() - 1
    r, c = _iota2((rows, rows), 0), _iota2((rows, rows), 1)
    same = (r >> shift) == (c >> shift)
    return same, same & (c <= r)


def _gla_kernel(qk_ref, v_ref, og_ref, glo_ref, w2_ref, gb_ref, ng_ref, s0_ref, *rest, n_pad, sub):
    o_ref, s_ref, st_ref = rest[-3:]
    c = pl.program_id(1)
    group, chunk = qk_ref.shape[0], qk_ref.shape[1]
    rows = group * chunk
    nsub = chunk // sub
    chains = [(g, h) for g in range(group) for h in range(GLA_HEADS)]
    ks = [slice(h * GLA_HK, (h + 1) * GLA_HK) for h in range(GLA_HEADS)]
    vs = [slice(h * GLA_HV, (h + 1) * GLA_HV) for h in range(GLA_HEADS)]
    rs = [slice(g * chunk, (g + 1) * chunk) for g in range(group)]

    @pl.when(c == 0)
    def _():
        for i, (g, h) in enumerate(chains):
            st_ref[i] = s0_ref[g, h].T

    valid = (c * chunk + _seq_rows((rows, 1), chunk)) >= n_pad
    pre = _mm3(_stacked(glo_ref), w2_ref[...]) + gb_ref[...]
    log_a = -_softplus(-pre) * (1.0 / GLA_GATE_NORM)
    log_a = jnp.where(valid, log_a, 0.0)
    qk = _stacked(qk_ref)
    q = qk[:, :GLA_DK] * (GLA_HK ** -0.5)
    k = jnp.where(valid, qk[:, GLA_DK:], 0.0)
    same, causal = _seq_masks(rows, sub)
    b = _mm_const_lhs(causal.astype(F32), log_a, 3)
    b_tot = _mm_const_lhs(same.astype(F32), log_a, 3)
    qd = (q * jnp.exp(b)).astype(BF16)
    kd = (k * jnp.exp(-b)).astype(BF16)
    kend = (k * jnp.exp(b_tot - b)).astype(BF16)
    vb = _stacked(v_ref).astype(BF16)
    causal_c = _seq_masks(chunk, sub)[1]
    att = [jnp.where(causal_c, _dg(qd[rs[g], ks[h]], kd[rs[g], ks[h]], 'nt'), 0.0).astype(BF16) for g, h in chains]
    o = [_dg(att[i], vb[rs[g], vs[h]]) for i, (g, h) in enumerate(chains)]
    s = [st_ref[i] for i in range(len(chains))]
    o_inter = [[] for _ in chains]
    for j in range(nsub):
        for i, (g, h) in enumerate(chains):
            sl = slice(g * chunk + j * sub, g * chunk + (j + 1) * sub)
            o_inter[i].append(_dg(qd[sl, ks[h]], s[i].astype(BF16), 'nt'))
        nxt = []
        for i, (g, h) in enumerate(chains):
            sl = slice(g * chunk + j * sub, g * chunk + (j + 1) * sub)
            dec = jnp.exp(b_tot[g * chunk + j * sub:g * chunk + j * sub + 1, ks[h]])
            nxt.append(s[i] * dec + _dg(vb[sl, vs[h]], kend[sl, ks[h]], 'tn'))
        s = nxt
    og = _stacked(og_ref)
    for i, (g, h) in enumerate(chains):
        st_ref[i] = s[i]
        oh = o[i] + (jnp.concatenate(o_inter[i], axis=0) if nsub > 1 else o_inter[i][0])
        oh = oh * lax.rsqrt(jnp.mean(oh * oh, axis=-1, keepdims=True) + NORM_EPS) * ng_ref[:, vs[h]]
        o_ref[g, :, vs[h]] = oh * _silu(og[rs[g], vs[h]])

    @pl.when(c == pl.num_programs(1) - 1)
    def _():
        for i, (g, h) in enumerate(chains):
            s_ref[g, h] = st_ref[i].T


def _gla(z, w2, gate_b, norm_g, s_all, layer, s_buf, n_pad, chunk, group):
    batch, seq_len, _ = z.shape
    nc = seq_len // chunk
    sub = math.gcd(chunk, GLA_SUBCHUNK)
    row = lambda blk: (lambda bg, c: (bg, c, blk))
    const = lambda bg, c: (0, 0)
    st = _state_spec(layer, group, (GLA_HEADS, GLA_HK, GLA_HV))
    w2p = jnp.zeros((LANE, GLA_DK), F32).at[:GLA_GATE_RANK].set(w2)
    alias_specs, alias_args, aliases = _alias_io(s_buf, 8, 1)
    return pl.pallas_call(
        functools.partial(_gla_kernel, n_pad=n_pad, sub=sub),
        grid=(batch // group, nc),
        in_specs=[pl.BlockSpec((group, chunk, 2 * GLA_DK), row(0)),
                  pl.BlockSpec((group, chunk, GLA_DV), row(1)),
                  pl.BlockSpec((group, chunk, GLA_DV), row(2)),
                  pl.BlockSpec((group, chunk, LANE), row(3 * 512 // LANE)),
                  pl.BlockSpec((LANE, GLA_DK), const),
                  pl.BlockSpec((1, GLA_DK), const),
                  pl.BlockSpec((1, GLA_DV), const),
                  st] + alias_specs,
        out_specs=[pl.BlockSpec((group, chunk, GLA_DV), row(0)), st],
        out_shape=[jax.ShapeDtypeStruct((batch, seq_len, GLA_DV), F32),
                   jax.ShapeDtypeStruct(s_all.shape, F32)],
        scratch_shapes=[pltpu.VMEM((group * GLA_HEADS, GLA_HV, GLA_HK), F32)],
        input_output_aliases=aliases,
        compiler_params=_params(2),
        name="gla",
    )(z, z, z, z, w2p, gate_b.reshape(1, GLA_DK), norm_g.reshape(1, GLA_DV), s_all, *alias_args)


def _rwkv_kernel(z_ref, mu_ref, w0_ref, w2_ref, a0_ref, a2_ref, g2_ref, kk_ref, ka_ref, rk_ref, lg_ref, lb_ref,
                 s0_ref, *rest, n_pad):
    y_ref, s_ref, st_ref, carry_ref = rest[-4:]
    c = pl.program_id(1)
    group, chunk = z_ref.shape[0], z_ref.shape[1]
    dim = RWKV_DIM
    pairs = RWKV_HEADS // 2

    @pl.when(c == 0)
    def _():
        zero = jnp.zeros((RWKV_HEAD, RWKV_HEAD), F32)
        for g in range(group):
            for q in range(pairs):
                top = jnp.concatenate([s0_ref[g, 2 * q], zero], axis=1)
                bottom = jnp.concatenate([zero, s0_ref[g, 2 * q + 1]], axis=1)
                st_ref[g * pairs + q] = jnp.concatenate([top, bottom], axis=0)
        carry_ref[...] = jnp.zeros_like(carry_ref)

    def prep(g0, g1):
        rows = (g1 - g0) * chunk
        valid = (c * chunk + _seq_rows((rows, 1), chunk)) >= n_pad
        z = z_ref[g0:g1].reshape(rows, z_ref.shape[2])
        z_prev = _shift_rows_seqs(z, [carry_ref[g] for g in range(g0, g1)], 1)
        for g in range(g0, g1):
            carry_ref[g] = z[(g - g0 + 1) * chunk - SUBLANE:(g - g0 + 1) * chunk]
        z = z + mu_ref[...] * (z_prev - z)
        r, kr, vr = z[:, :dim], z[:, dim:2 * dim], z[:, 2 * dim:3 * dim]
        w_lo = z[:, 3 * dim:3 * dim + 64]
        a_lo = z[:, 3 * dim + 64:3 * dim + 128]
        g_lo = z[:, 3 * dim + 128:]
        w_raw = w0_ref[...] + _mm3(jnp.tanh(w_lo), w2_ref[...])
        log_w = -jnp.exp(-_softplus(-w_raw) - 0.5)
        a_lr = _sigmoid(a0_ref[...] + _mm3(a_lo, a2_ref[...]))
        gate = _mm3(_sigmoid(g_lo), g2_ref[...])
        kx = kr * kk_ref[...]
        kk = kx * lax.rsqrt(_head_sums(kx * kx, RWKV_HEAD, 2) + NORM_EPS)
        k_mod = kr * (1.0 + (a_lr - 1.0) * ka_ref[...])
        bonus = _head_sums(r * k_mod * rk_ref[...], RWKV_HEAD, 2) * vr
        log_w = jnp.where(valid, log_w, 0.0)
        k_in = jnp.where(valid, k_mod, 0.0)
        b_in = jnp.where(valid, kk * a_lr, 0.0)
        same, causal = _seq_masks(rows, chunk)
        c_incl = _mm_const_lhs(causal.astype(F32), log_w, 3)
        c_tot = _mm_const_lhs(same.astype(F32), log_w, 3)
        e_neg = jnp.exp(-c_incl)
        e_end = jnp.exp(c_tot - c_incl)
        return dict(a_t=-kk * jnp.exp(c_incl - log_w), r_t=r * jnp.exp(c_incl),
                    bt=b_in * e_neg, kt=k_in * e_neg,
                    b_e=b_in * e_end, k_e=k_in * e_end, vr=vr, c_tot=c_tot,
                    bonus=bonus, gate=gate)

    def chains(p, g0, g1):
        ch = [(g, q) for g in range(g1 - g0) for q in range(pairs)]
        rs = [slice(g * chunk, (g + 1) * chunk) for g in range(g1 - g0)]
        pb = [slice(q * LANE, (q + 1) * LANE) for q in range(pairs)]
        n = len(ch)
        slot = [(g0 + g) * pairs + q for g, q in ch]
        col = _iota2((chunk, 2 * chunk), 1) & (chunk - 1)
        row = _iota2((chunk, 2 * chunk), 0)
        strict2, incl2 = col < row, col <= row
        blk = lambda x, g, q: _pair_blocks(x[rs[g], pb[q]], RWKV_HEAD).astype(BF16)
        s = [st_ref[i] for i in slot]
        ar = [jnp.concatenate([p['a_t'][rs[g], pb[q]], p['r_t'][rs[g], pb[q]]], axis=0).astype(BF16) for g, q in ch]
        vbd = [blk(p['vr'], g, q) for g, q in ch]
        g_b = [_dg(ar[i], blk(p['bt'], g, q), 'nt') for i, (g, q) in enumerate(ch)]
        g_k = [_dg(ar[i], blk(p['kt'], g, q), 'nt') for i, (g, q) in enumerate(ch)]
        g_s = [_dg(ar[i], s[i].astype(BF16), 'nt') for i in range(n)]
        m_ab = [jnp.where(strict2, g_b[i][:chunk], 0.0) for i in range(n)]
        rhs = [g_s[i][:chunk] + _mm(jnp.where(strict2, g_k[i][:chunk], 0.0), vbd[i]) for i in range(n)]
        t_inv = _unit_lower_inverse_pairs(m_ab)
        u = [_mm(t_inv[i], _pair_blocks(rhs[i], RWKV_HEAD)) for i in range(n)]
        ys = [g_s[i][chunk:] + _mm(jnp.where(incl2, g_b[i][chunk:], 0.0), _pair_blocks(u[i], RWKV_HEAD))
              + _mm(jnp.where(incl2, g_k[i][chunk:], 0.0), vbd[i]) for i in range(n)]
        same_head = (_iota2((LANE, LANE), 0) < RWKV_HEAD) == (_iota2((LANE, LANE), 1) < RWKV_HEAD)
        for i, (g, q) in enumerate(ch):
            uv = jnp.concatenate([u[i], p['vr'][rs[g], pb[q]]], axis=0)
            bk = jnp.concatenate([p['b_e'][rs[g], pb[q]], p['k_e'][rs[g], pb[q]]], axis=0)
            w_end = jnp.exp(p['c_tot'][g * chunk:g * chunk + 1, pb[q]])
            st_ref[slot[i]] = s[i] * w_end + jnp.where(same_head, _mm(uv, bk, 'tn'), 0.0)
        per_seq = [jnp.concatenate(ys[g * pairs:(g + 1) * pairs], axis=1) for g in range(g1 - g0)]
        return jnp.concatenate(per_seq, axis=0) if g1 - g0 > 1 else per_seq[0]

    def finish(p, y, g0, g1):
        mean = _head_sums(y, RWKV_HEAD, 2) * (1.0 / RWKV_HEAD)
        d = y - mean
        var = _head_sums(d * d, RWKV_HEAD, 2) * (1.0 / RWKV_HEAD)
        y = d * lax.rsqrt(var + RWKV_GN_EPS) * lg_ref[...] + lb_ref[...]
        y_ref[g0:g1] = ((y + p['bonus']) * p['gate']).reshape(g1 - g0, chunk, dim)

    p = prep(0, group)
    finish(p, chains(p, 0, group), 0, group)

    @pl.when(c == pl.num_programs(1) - 1)
    def _():
        for g in range(group):
            for q in range(pairs):
                s_ref[g, 2 * q] = st_ref[g * pairs + q, :RWKV_HEAD, :RWKV_HEAD]
                s_ref[g, 2 * q + 1] = st_ref[g * pairs + q, RWKV_HEAD:, RWKV_HEAD:]


def _rwkv(z, p, s_all, layer, s_buf, n_pad, chunk, group):
    batch, seq_len, _ = z.shape
    nc = seq_len // chunk
    dim = RWKV_DIM
    row = lambda bg, c: (bg, c, 0)
    const = lambda bg, c: (0, 0)
    st = _state_spec(layer, group, (RWKV_HEADS, RWKV_HEAD, RWKV_HEAD))
    vec = lambda a: a.reshape(1, -1)
    small = [vec(p['mu']), vec(p['w0']), p['w2'], vec(p['a0']), p['a2'], p['g2'], vec(p['k_k']), vec(p['k_a']),
             vec(p['r_k']), vec(p['ln_g']), vec(p['ln_b'])]
    alias_specs, alias_args, aliases = _alias_io(s_buf, len(small) + 2, 1)
    return pl.pallas_call(
        functools.partial(_rwkv_kernel, n_pad=n_pad),
        grid=(batch // group, nc),
        in_specs=([pl.BlockSpec((group, chunk, RWKV_COLS), row)]
                  + [pl.BlockSpec(a.shape, const) for a in small]
                  + [st] + alias_specs),
        out_specs=[pl.BlockSpec((group, chunk, dim), row), st],
        out_shape=[jax.ShapeDtypeStruct((batch, seq_len, dim), F32),
                   jax.ShapeDtypeStruct(s_all.shape, F32)],
        scratch_shapes=[pltpu.VMEM((group * RWKV_HEADS // 2, LANE, LANE), F32),
                        pltpu.VMEM((group, SUBLANE, RWKV_COLS), F32)],
        input_output_aliases=aliases,
        compiler_params=_params(2),
        name="rwkv7",
    )(z, *small, s_all, *alias_args)


def _gdn_kernel(qkv_ref, rest_ref, alog_ref, dt_ref, ng_ref, s0_ref, *rest, n_pad):
    o_ref, s_ref, st_ref = rest[-3:]
    c = pl.program_id(1)
    group, chunk = qkv_ref.shape[0], qkv_ref.shape[1]
    rows = group * chunk
    dim = GDN_DIM
    chains = [(g, h) for g in range(group) for h in range(GDN_HEADS)]
    hs = [slice(h * GDN_HEAD, (h + 1) * GDN_HEAD) for h in range(GDN_HEADS)]
    rs = [slice(g * chunk, (g + 1) * chunk) for g in range(group)]

    @pl.when(c == 0)
    def _():
        for i, (g, h) in enumerate(chains):
            st_ref[i] = s0_ref[g, h]

    valid = (c * chunk + _seq_rows((rows, 1), chunk)) >= n_pad
    qkv = _stacked(qkv_ref)
    rest = _stacked(rest_ref)
    zg = rest[:, :dim]
    lo = rest[:, dim:]
    beta_all = _sigmoid(lo)
    g_all = -jnp.exp(alog_ref[...]) * _softplus(lo + dt_ref[...])
    g_all = jnp.where(valid, g_all, 0.0)

    incl = _tri(chunk)
    strict = _tri(chunk, strict=True)
    causal = _seq_masks(rows, chunk)[1]
    gam_all = _mm_const_lhs(causal.astype(F32), g_all, 3)
    shift = chunk.bit_length() - 1
    pick = (_iota2((GDN_HEADS * chunk, LANE), 1)
            == GDN_HEADS + (_iota2((GDN_HEADS * chunk, LANE), 0) >> shift)).astype(F32)
    gam_cols = [_mm_const_lhs(pick, gam_all[rs[g]], 3, 'nt') for g in range(group)]
    q_sq = _head_sums(jnp.square(qkv[:, :dim]), GDN_HEAD, 2)
    k_sq = _head_sums(jnp.square(qkv[:, dim:2 * dim]), GDN_HEAD, 2)
    q_all = qkv[:, :dim] * lax.rsqrt(q_sq + NORM_EPS) * (GDN_HEAD ** -0.5)
    k_all = jnp.where(valid, qkv[:, dim:2 * dim] * lax.rsqrt(k_sq + NORM_EPS), 0.0)
    v_all = qkv[:, 2 * dim:]
    n = len(chains)
    gam = [gam_all[rs[g], GDN_HEADS + h:GDN_HEADS + h + 1] for g, h in chains]
    beta = [beta_all[rs[g], h:h + 1] for g, h in chains]
    kq = [jnp.concatenate([k_all[rs[g], hs[h]], q_all[rs[g], hs[h]]], axis=0).astype(BF16) for g, h in chains]
    dec = [jnp.where(incl, jnp.exp(jnp.where(incl, gam[i] - gam_cols[g][h * chunk:(h + 1) * chunk], 0.0)), 0.0)
           for i, (g, h) in enumerate(chains)]
    gram = [_dg(kq[i], k_all[rs[g], hs[h]].astype(BF16), 'nt') for i, (g, h) in enumerate(chains)]
    m = [jnp.where(strict, gram[i][:chunk] * dec[i], 0.0) * (-beta[i]) for i in range(n)]
    t_inv = _unit_lower_inverses(m)
    e_gam = [jnp.exp(gam[i]) for i in range(n)]
    rhs = [jnp.concatenate([v_all[rs[g], hs[h]] * beta[i], k_all[rs[g], hs[h]] * (beta[i] * e_gam[i])], axis=1)
           for i, (g, h) in enumerate(chains)]
    uw = [_mm(t_inv[i], rhs[i]) for i in range(n)]
    s = [st_ref[i] for i in range(n)]
    sb = [s[i].astype(BF16) for i in range(n)]
    delta = [uw[i][:, :GDN_HEAD] - _mm(uw[i][:, GDN_HEAD:], sb[i]) for i in range(n)]
    o = [_mm(q_all[rs[g], hs[h]] * e_gam[i], sb[i]) + _mm(gram[i][chunk:] * dec[i], delta[i])
         for i, (g, h) in enumerate(chains)]
    for i, (g, h) in enumerate(chains):
        g_end = gam[i][chunk - 1:chunk, :]
        st_ref[i] = s[i] * jnp.exp(g_end) + _mm(k_all[rs[g], hs[h]] * jnp.exp(g_end - gam[i]), delta[i], 'tn')
        oh = o[i] * lax.rsqrt(jnp.mean(o[i] * o[i], axis=-1, keepdims=True) + NORM_EPS) * ng_ref[...]
        o_ref[g, :, hs[h]] = oh * _silu(zg[rs[g], hs[h]])

    @pl.when(c == pl.num_programs(1) - 1)
    def _():
        for i, (g, h) in enumerate(chains):
            s_ref[g, h] = st_ref[i]


def _gdn(qkv, rest, a_log, dt_bias, norm_g, s_all, layer, s_buf, n_pad, chunk, group):
    batch, seq_len, _ = qkv.shape
    nc = seq_len // chunk
    dim = GDN_DIM
    row = lambda bg, c: (bg, c, 0)
    const = lambda bg, c: (0, 0)
    st = _state_spec(layer, group, (GDN_HEADS, GDN_HEAD, GDN_HEAD))
    pad_lo = lambda a: jnp.zeros((1, LANE), F32).at[0, GDN_HEADS:2 * GDN_HEADS].set(a)
    alias_specs, alias_args, aliases = _alias_io(s_buf, 6, 1)
    return pl.pallas_call(
        functools.partial(_gdn_kernel, n_pad=n_pad),
        grid=(batch // group, nc),
        in_specs=[pl.BlockSpec((group, chunk, 3 * dim), row),
                  pl.BlockSpec((group, chunk, dim + LANE), row),
                  pl.BlockSpec((1, LANE), const),
                  pl.BlockSpec((1, LANE), const),
                  pl.BlockSpec((1, GDN_HEAD), const),
                  st] + alias_specs,
        out_specs=[pl.BlockSpec((group, chunk, dim), row), st],
        out_shape=[jax.ShapeDtypeStruct((batch, seq_len, dim), F32),
                   jax.ShapeDtypeStruct(s_all.shape, F32)],
        scratch_shapes=[pltpu.VMEM((group * GDN_HEADS, GDN_HEAD, GDN_HEAD), F32)],
        input_output_aliases=aliases,
        compiler_params=_params(2),
        name="gated_deltanet",
    )(qkv, rest, pad_lo(a_log), pad_lo(dt_bias), norm_g.reshape(1, GDN_HEAD), s_all, *alias_args)


def _pad_rows_fix(buf, seq_len, n_pad):
    b, w, c = buf.shape
    return jnp.zeros((b, seq_len, c), F32).at[:, n_pad - w:n_pad].set(buf).reshape(b * seq_len, c)


def _trunk(x, states, wts, batch, seq_len, n_pad, chunk, group, carried):
    st_gla, st_rwkv, st_shift, st_gdn, st_gdn_conv, st_ffn_conv = states
    d = x.shape[1]
    rows = batch * seq_len
    tm = _seq_tile(seq_len, rows, carried)
    tile_tail = lambda a, w: a.reshape(batch, seq_len // tm, SUBLANE, -1)[:, -1, SUBLANE - w:]
    seq_tail = lambda a, w: a.reshape(batch, seq_len, -1)[:, seq_len - w:]
    conv_rows = seq_tail if carried else tile_tail
    seqs = lambda a: a.reshape(batch, seq_len, a.shape[-1])
    flat = lambda a: a.reshape(rows, a.shape[-1])
    new_gla = new_rwkv = new_gdn = None
    new = {k: [] for k in ('shift', 'gdn_conv', 'ffn_conv')}
    for l in range(DEPTH):
        i = l // 2
        if l % 2 == 0:
            new['shift'].append(seqs(x)[:, -1])
            x_in = x
            if carried:
                x_in = flat(seqs(x).at[:, n_pad - 1].set(st_shift[i]))
            z_gla, z_rwkv = _inproj(x_in, wts['w_in_ab'], i, (GLA_Z, RWKV_COLS))
            o_gla, new_gla = _gla(seqs(z_gla), wts['gla_gate_w2'][i], wts['gla_gate_b'][i], wts['gla_norm_g'][i],
                                  st_gla, i, new_gla, n_pad, chunk, group)
            y_rwkv, new_rwkv = _rwkv(seqs(z_rwkv), {k: wts['rwkv_' + k][i] for k in
                                                    ('mu', 'w0', 'w2', 'a0', 'a2', 'g2', 'k_k', 'k_a', 'r_k', 'ln_g',
                                                     'ln_b')},
                                     st_rwkv, i, new_rwkv, n_pad, chunk, group)
            mixed = (flat(o_gla), flat(y_rwkv))
            w_out = wts['w_out_ab']
        else:
            fix = _pad_rows_fix(st_gdn_conv[i], seq_len, n_pad) if carried else None
            qkv, rest, pre = _inproj_gdn(x, wts['w_in_c'], i, wts['gdn_conv_w'][i], fix, tm, carried)
            new['gdn_conv'].append(conv_rows(pre, GDN_CONV - 1))
            o, new_gdn = _gdn(seqs(qkv), seqs(rest), wts['gdn_A_log'][i], wts['gdn_dt_bias'][i],
                              wts['gdn_norm_g'][i], st_gdn, i, new_gdn, n_pad, chunk, group)
            mixed = (flat(o),)
            w_out = wts['w_out_c']
        fix = _pad_rows_fix(st_ffn_conv[l], seq_len, n_pad) if carried else None
        x, gate = _layer_tail(mixed, x, w_out, i, (wts['ln_mix_g'][l], wts['ln_mix_b'][l]),
                              wts['w_up'], wts['w_down'], l, wts['ffn_conv_w'][l], wts['ffn_conv_b'][l],
                              (wts['ln_ffn_g'][l], wts['ln_ffn_b'][l]), fix, seq_len, n_pad, tm, carried)
        new['ffn_conv'].append(conv_rows(gate, FFN_CONV - 1))
    return (x, new_gla, new_rwkv, jnp.stack(new['shift']), new_gdn, jnp.stack(new['gdn_conv']),
            jnp.stack(new['ffn_conv']))


def kernel(x_prompt, x_sample, state_gla, state_rwkv, state_rwkv_shift, state_gdn, state_gdn_conv, state_ffn_conv, meta_tokens, w_in_ab, gla_gate_w2, gla_gate_b, gla_norm_g, rwkv_mu, rwkv_w0, rwkv_w2, rwkv_a0, rwkv_a2, rwkv_g2, rwkv_k_k, rwkv_k_a, rwkv_r_k, rwkv_ln_g, rwkv_ln_b, w_out_ab, w_in_c, gdn_conv_w, gdn_A_log, gdn_dt_bias, gdn_norm_g, w_out_c, w_up, ffn_conv_w, ffn_conv_b, w_down, ln_mix_g, ln_mix_b, ln_ffn_g, ln_ffn_b):
    d = x_prompt.shape[-1]
    n_ab, n_c = w_in_ab.shape[0], w_in_c.shape[0]
    gla_cols = 2 * GLA_DK + 2 * GLA_DV + GLA_GATE_RANK
    lo0 = 2 * GLA_DK + GLA_DV
    w_ab = jnp.concatenate([w_in_ab[:, :, :lo0], w_in_ab[:, :, lo0 + GLA_GATE_RANK:gla_cols],
                            w_in_ab[:, :, lo0:lo0 + GLA_GATE_RANK],
                            jnp.zeros((n_ab, d, LANE - GLA_GATE_RANK), w_in_ab.dtype),
                            w_in_ab[:, :, gla_cols:]], axis=2).astype(BF16)
    w_c = jnp.concatenate([w_in_c, jnp.zeros((n_c, d, LANE - 2 * GDN_HEADS), w_in_c.dtype)], axis=2).astype(BF16)
    wts = {
        'w_in_ab': w_ab, 'gla_gate_w2': gla_gate_w2, 'gla_gate_b': gla_gate_b, 'gla_norm_g': gla_norm_g,
        'rwkv_mu': rwkv_mu, 'rwkv_w0': rwkv_w0, 'rwkv_w2': rwkv_w2, 'rwkv_a0': rwkv_a0, 'rwkv_a2': rwkv_a2,
        'rwkv_g2': rwkv_g2, 'rwkv_k_k': rwkv_k_k, 'rwkv_k_a': rwkv_k_a, 'rwkv_r_k': rwkv_r_k,
        'rwkv_ln_g': rwkv_ln_g, 'rwkv_ln_b': rwkv_ln_b, 'w_out_ab': w_out_ab.astype(BF16),
        'w_in_c': w_c, 'gdn_conv_w': gdn_conv_w, 'gdn_A_log': gdn_A_log, 'gdn_dt_bias': gdn_dt_bias,
        'gdn_norm_g': gdn_norm_g, 'w_out_c': w_out_c.astype(BF16),
        'w_up': w_up.astype(BF16), 'ffn_conv_w': ffn_conv_w, 'ffn_conv_b': ffn_conv_b,
        'w_down': w_down.astype(BF16),
        'ln_mix_g': ln_mix_g, 'ln_mix_b': ln_mix_b, 'ln_ffn_g': ln_ffn_g, 'ln_ffn_b': ln_ffn_b,
    }

    bp, seq, _ = x_prompt.shape
    tp = -(-(N_META + seq + SUBLANE) // PROMPT_CHUNK) * PROMPT_CHUNK
    pad_p = tp - N_META - seq
    meta = jnp.broadcast_to(meta_tokens.astype(F32)[None], (bp, N_META, d))
    xp = jnp.concatenate([jnp.zeros((bp, pad_p, d), F32), meta, x_prompt], axis=1).reshape(bp * tp, d)
    zero_state = lambda s: jnp.zeros((s.shape[0], bp) + s.shape[2:], F32)
    p_out = _trunk(xp, tuple(zero_state(s) for s in (state_gla, state_rwkv, state_rwkv_shift, state_gdn,
                                                      state_gdn_conv, state_ffn_conv)),
                   wts, bp, tp, pad_p, PROMPT_CHUNK, _group(bp, PROMPT_GROUP), carried=False)
    y_prompt = p_out[0].reshape(bp, tp, d)[:, pad_p + N_META:]

    bs, ts, _ = x_sample.shape
    tsp = -(-(ts + GDN_CONV - 1) // SUBLANE) * SUBLANE
    pad_s = tsp - ts
    xs = jnp.concatenate([jnp.zeros((bs, pad_s, d), F32), x_sample], axis=1).reshape(bs * tsp, d)
    s_out = _trunk(xs, (state_gla, state_rwkv, state_rwkv_shift, state_gdn, state_gdn_conv, state_ffn_conv),
                   wts, bs, tsp, pad_s, tsp, _group(bs, SAMPLE_GROUP), carried=True)
    y_sample = s_out[0].reshape(bs, tsp, d)[:, pad_s:]
    return (y_prompt, y_sample) + tuple(p_out[1:]) + tuple(s_out[1:])
```

```python
import functools
import math

import jax
import jax.numpy as jnp
from jax import lax
from jax.experimental import pallas as pl
from jax.experimental.pallas import tpu as pltpu

F32 = jnp.float32
BF16 = jnp.bfloat16

N_META = 16
GLA_HEADS, GLA_HK, GLA_HV = 4, 64, 128
GLA_DK, GLA_DV = GLA_HEADS * GLA_HK, GLA_HEADS * GLA_HV
GLA_GATE_RANK = 16
GLA_GATE_NORM = 16.0
GLA_SUBCHUNK = 16
RWKV_HEADS, RWKV_HEAD = 8, 64
RWKV_DIM = RWKV_HEADS * RWKV_HEAD
RWKV_COLS = 3 * RWKV_DIM + 64 + 64 + 128
RWKV_GN_EPS = 64e-5
GDN_HEADS, GDN_HEAD = 8, 128
GDN_DIM = GDN_HEADS * GDN_HEAD
GDN_CONV = 4
FFN_CONV = 3
LN_EPS = 1e-5
NORM_EPS = 1e-6
DEPTH = 4
DEEPNORM_ALPHA = (2.0 * DEPTH) ** 0.25

LANE = 128
SUBLANE = 8
PROMPT_CHUNK = 64
PROMPT_GROUP = 4
SAMPLE_GROUP = 8
VMEM_LIMIT = 56 * 1024 * 1024

GLA_Z = 3 * 512 + LANE


def _params(n_axes):
    return pltpu.CompilerParams(dimension_semantics=("arbitrary",) * n_axes,
                                vmem_limit_bytes=VMEM_LIMIT)


def _row_tile(rows, cap):
    t = cap
    while rows % t:
        t //= 2
    return t


def _seq_tile(seq_len, rows, carried):
    if carried:
        return _row_tile(rows, 256)
    return next(t for t in (704, 352, 192, 64) if seq_len % t == 0)


def _resident(shape, layer=None):
    if layer is None:
        return pl.BlockSpec(shape, lambda *_: (0,) * len(shape), pipeline_mode=pl.Buffered(1))
    return pl.BlockSpec((None,) + tuple(shape), lambda *_: (layer,) + (0,) * len(shape),
                        pipeline_mode=pl.Buffered(1))


_DIMS = {'nn': (((1,), (0,)), ((), ())),
         'nt': (((1,), (1,)), ((), ())),
         'tn': (((0,), (0,)), ((), ()))}


def _dg(a, b, kind='nn'):
    return lax.dot_general(a, b, _DIMS[kind], preferred_element_type=F32)


def _split(x, pieces):
    out = []
    for _ in range(pieces - 1):
        hi = x.astype(BF16)
        out.append(hi)
        x = x - hi.astype(F32)
    out.append(x.astype(BF16))
    return out


def _mm(a, b, kind='nn'):
    return _dg(a.astype(BF16), b.astype(BF16), kind)


def _mm3(a, b, kind='nn'):
    ah, al = _split(a, 2)
    bh, bl = _split(b, 2)
    return _dg(ah, bh, kind) + _dg(al, bh, kind) + _dg(ah, bl, kind)


def _mm_const_lhs(c, x, pieces, kind='nn'):
    cb = c.astype(BF16)
    return sum(_dg(cb, p, kind) for p in _split(x, pieces))


def _mm_const_rhs(x, c, pieces, kind='nn'):
    cb = c.astype(BF16)
    return sum(_dg(p, cb, kind) for p in _split(x, pieces))


def _bdot(a, w):
    return jnp.dot(a.astype(BF16), w, preferred_element_type=F32)


def _iota2(shape, dim):
    return lax.broadcasted_iota(jnp.int32, shape, dim)


def _tri(n, strict=False):
    r, c = _iota2((n, n), 0), _iota2((n, n), 1)
    return (c < r) if strict else (c <= r)


def _softplus(x):
    return jnp.maximum(x, 0.0) + jnp.log1p(jnp.exp(-jnp.abs(x)))


def _sigmoid(x):
    return 1.0 / (1.0 + jnp.exp(-x))


def _silu(x):
    return x * _sigmoid(x)


def _unit_lower_inverses(ms):
    n = ms[0].shape[0]
    eye = (_iota2((n, n), 0) == _iota2((n, n), 1)).astype(F32)
    xs = [eye + m for m in ms]
    ps = list(ms)
    for _ in range(int(math.log2(n)) - 1):
        ps = [_mm(p, p) for p in ps]
        xs = [x + _mm(x, p) for x, p in zip(xs, ps)]
    return xs


def _pair_blocks(x, split):
    first = _iota2(x.shape, 1) < split
    return jnp.concatenate([jnp.where(first, x, 0.0), jnp.where(first, 0.0, x)], axis=0)


def _unit_lower_inverse_pairs(ms):
    n = ms[0].shape[0]
    eye = ((_iota2((n, 2 * n), 1) & (n - 1)) == _iota2((n, 2 * n), 0)).astype(F32)
    xs = [eye + m for m in ms]
    ps = list(ms)
    bds = [_pair_blocks(p, n).astype(BF16) for p in ps]
    for _ in range(int(math.log2(n)) - 1):
        ps = [_mm(p, bd) for p, bd in zip(ps, bds)]
        bds = [_pair_blocks(p, n).astype(BF16) for p in ps]
        xs = [x + _mm(x, bd) for x, bd in zip(xs, bds)]
    return xs


def _head_sums(x, width, pieces):
    rows, cols = x.shape
    nb = cols // LANE
    shift = width.bit_length() - 1
    grp = (_iota2((LANE, LANE), 0) >> shift == _iota2((LANE, LANE), 1) >> shift).astype(F32)
    xs = jnp.concatenate([x[:, i * LANE:(i + 1) * LANE] for i in range(nb)], axis=0)
    s = _mm_const_rhs(xs, grp, pieces)
    return jnp.concatenate([s[i * rows:(i + 1) * rows] for i in range(nb)], axis=1)


def _shift_rows(cur, carry, k):
    rolled = pltpu.roll(cur, k, 0)
    head = jnp.where(_iota2((SUBLANE, 1), 0) < k, pltpu.roll(carry, k, 0), rolled[:SUBLANE])
    if cur.shape[0] == SUBLANE:
        return head
    return jnp.concatenate([head, rolled[SUBLANE:]], axis=0)


def _shift_rows_seqs(cur, carries, k):
    n = len(carries)
    rows = cur.shape[0] // n
    return jnp.concatenate([_shift_rows(cur[g * rows:(g + 1) * rows], carries[g], k) for g in range(n)], axis=0) \
        if n > 1 else _shift_rows(cur, carries[0], k)


def _seq_rows(shape, seq_rows):
    return _iota2(shape, 0) & (seq_rows - 1)


def _valid_rows(row0, rows, seq_len, n_pad):
    if seq_len & (seq_len - 1) == 0:
        t = (row0 + _iota2((rows, 1), 0)) & (seq_len - 1)
    else:
        t = lax.rem(row0, seq_len) + _iota2((rows, 1), 0)
        for _ in range(-(-rows // seq_len)):
            t = jnp.where(t >= seq_len, t - seq_len, t)
    return t >= n_pad


def _layer_norm_rows(h, g, b):
    mu = jnp.mean(h, axis=-1, keepdims=True)
    d = h - mu
    var = jnp.mean(d * d, axis=-1, keepdims=True)
    return d * lax.rsqrt(var + LN_EPS) * g + b


def _inproj_kernel(x_ref, w_ref, *out_refs, widths):
    z = _bdot(x_ref[...], w_ref[...])
    off = 0
    for o_ref, wd in zip(out_refs, widths):
        o_ref[...] = z[:, off:off + wd]
        off += wd


def _inproj(x, w, layer, widths):
    rows, d = x.shape
    n = w.shape[2]
    tm = _row_tile(rows, 512)
    return pl.pallas_call(
        functools.partial(_inproj_kernel, widths=widths),
        grid=(rows // tm,),
        in_specs=[pl.BlockSpec((tm, d), lambda i: (i, 0)), _resident((d, n), layer)],
        out_specs=[pl.BlockSpec((tm, wd), lambda i: (i, 0)) for wd in widths],
        out_shape=[jax.ShapeDtypeStruct((rows, wd), F32) for wd in widths],
        compiler_params=_params(1),
        name="inproj",
    )(x, w)


def _inproj_gdn_kernel(*refs, has_fix, full_pre):
    if has_fix:
        x_ref, w_ref, cw_ref, fix_ref, act_ref, rest_ref, pre_ref, carry_ref = refs
    else:
        x_ref, w_ref, cw_ref, act_ref, rest_ref, pre_ref, carry_ref = refs
    tm = x_ref.shape[0]
    width = act_ref.shape[1]

    @pl.when(pl.program_id(0) == 0)
    def _():
        carry_ref[...] = jnp.zeros_like(carry_ref)

    xb = x_ref[...].astype(BF16)
    step = 4 * LANE
    for lo in range(0, width, step):
        cs = slice(lo, lo + step)
        pre = jnp.dot(xb, w_ref[:, cs], preferred_element_type=F32)
        if has_fix:
            pre = pre + fix_ref[:, cs]
        carry = carry_ref[:, cs]
        conv = cw_ref[GDN_CONV - 1:GDN_CONV, cs] * pre
        for tap in range(GDN_CONV - 1):
            conv = conv + cw_ref[tap:tap + 1, cs] * _shift_rows(pre, carry, GDN_CONV - 1 - tap)
        carry_ref[:, cs] = pre[tm - SUBLANE:, :]
        act_ref[:, cs] = _silu(conv)
        if full_pre:
            pre_ref[:, cs] = pre
        else:
            pre_ref[0, :, cs] = pre[tm - SUBLANE:, :]
    rest_ref[...] = jnp.dot(xb, w_ref[:, width:], preferred_element_type=F32)


def _inproj_gdn(x, w, layer, conv_w, fix, tm, full_pre):
    rows, d = x.shape
    n = w.shape[2]
    width = conv_w.shape[1]
    nt = rows // tm
    in_specs = [pl.BlockSpec((tm, d), lambda i: (i, 0)), _resident((d, n), layer), _resident(conv_w.shape)]
    args = [x, w, conv_w]
    if fix is not None:
        in_specs.append(pl.BlockSpec((tm, width), lambda i: (i, 0)))
        args.append(fix)
    if full_pre:
        pre_spec = pl.BlockSpec((tm, width), lambda i: (i, 0))
        pre_shape = jax.ShapeDtypeStruct((rows, width), F32)
    else:
        pre_spec = pl.BlockSpec((1, SUBLANE, width), lambda i: (i, 0, 0))
        pre_shape = jax.ShapeDtypeStruct((nt, SUBLANE, width), F32)
    return pl.pallas_call(
        functools.partial(_inproj_gdn_kernel, has_fix=fix is not None, full_pre=full_pre),
        grid=(nt,),
        in_specs=in_specs,
        out_specs=[pl.BlockSpec((tm, width), lambda i: (i, 0)),
                   pl.BlockSpec((tm, n - width), lambda i: (i, 0)),
                   pre_spec],
        out_shape=[jax.ShapeDtypeStruct((rows, width), F32),
                   jax.ShapeDtypeStruct((rows, n - width), F32),
                   pre_shape],
        scratch_shapes=[pltpu.VMEM((SUBLANE, width), F32)],
        compiler_params=_params(1),
        name="inproj_gdn",
    )(*args)


def _tail_kernel(*refs, n_mixed, seq_len, n_pad, d_ff, has_fix, full_gate):
    o_refs, refs = refs[:n_mixed], refs[n_mixed:]
    if has_fix:
        (x_ref, wout_ref, g1_ref, b1_ref, wup_ref, wdn_ref, cw_ref, cb_ref, g2_ref, b2_ref, fix_ref,
         y_ref, gate_ref, carry_ref) = refs
    else:
        (x_ref, wout_ref, g1_ref, b1_ref, wup_ref, wdn_ref, cw_ref, cb_ref, g2_ref, b2_ref,
         y_ref, gate_ref, carry_ref) = refs
    i = pl.program_id(0)
    tm = x_ref.shape[0]

    @pl.when(i == 0)
    def _():
        carry_ref[...] = jnp.zeros_like(carry_ref)

    valid = _valid_rows(i * tm, tm, seq_len, n_pad)
    h = DEEPNORM_ALPHA * x_ref[...]
    off = 0
    for o_ref in o_refs:
        h = h + _bdot(o_ref[...], wout_ref[off:off + o_ref.shape[1], :])
        off += o_ref.shape[1]
    x = jnp.where(valid, _layer_norm_rows(h, g1_ref[...], b1_ref[...]), 0.0)

    hu = _bdot(x, wup_ref[...])
    gate, up = hu[:, :d_ff], hu[:, d_ff:]
    if has_fix:
        gate = gate + fix_ref[...]
    carry = carry_ref[...]
    conv = (cw_ref[0:1, :] * _shift_rows(gate, carry, 2) + cw_ref[1:2, :] * _shift_rows(gate, carry, 1)
            + cw_ref[2:3, :] * gate + cb_ref[...])
    acc = _bdot(_silu(conv) * up, wdn_ref[...])
    carry_ref[...] = gate[tm - SUBLANE:, :]
    if full_gate:
        gate_ref[...] = gate
    else:
        gate_ref[0] = gate[tm - SUBLANE:, :]
    y = _layer_norm_rows(DEEPNORM_ALPHA * x + acc, g2_ref[...], b2_ref[...])
    y_ref[...] = jnp.where(valid, y, 0.0)


def _layer_tail(mixed, x, w_out, mix_layer, ln1, w_up, w_down, layer, conv_w, conv_b, ln2, fix,
                seq_len, n_pad, tm, full_gate):
    rows, d = x.shape
    d_ff = w_down.shape[1]
    nt = rows // tm
    vec = lambda a: a.reshape(1, -1)
    row_block = lambda width: pl.BlockSpec((tm, width), lambda i: (i, 0))
    in_specs = ([row_block(o.shape[1]) for o in mixed]
                + [row_block(d), _resident((w_out.shape[1], d), mix_layer), _resident((1, d)), _resident((1, d)),
                   _resident((d, 2 * d_ff), layer), _resident((d_ff, d), layer),
                   _resident((FFN_CONV, d_ff)), _resident((1, d_ff)), _resident((1, d)), _resident((1, d))])
    args = list(mixed) + [x, w_out, vec(ln1[0]), vec(ln1[1]), w_up, w_down, conv_w, vec(conv_b),
                          vec(ln2[0]), vec(ln2[1])]
    if fix is not None:
        in_specs.append(row_block(d_ff))
        args.append(fix)
    if full_gate:
        gate_spec = row_block(d_ff)
        gate_shape = jax.ShapeDtypeStruct((rows, d_ff), F32)
    else:
        gate_spec = pl.BlockSpec((1, SUBLANE, d_ff), lambda i: (i, 0, 0))
        gate_shape = jax.ShapeDtypeStruct((nt, SUBLANE, d_ff), F32)
    return pl.pallas_call(
        functools.partial(_tail_kernel, n_mixed=len(mixed), seq_len=seq_len, n_pad=n_pad, d_ff=d_ff,
                          has_fix=fix is not None, full_gate=full_gate),
        grid=(nt,),
        in_specs=in_specs,
        out_specs=[row_block(d), gate_spec],
        out_shape=[jax.ShapeDtypeStruct((rows, d), F32), gate_shape],
        scratch_shapes=[pltpu.VMEM((SUBLANE, d_ff), F32)],
        compiler_params=_params(1),
        name="layer_tail",
    )(*args)


def _group(batch, preferred):
    g = preferred
    while batch % g:
        g //= 2
    return g


def _state_spec(layer, group, dims):
    return pl.BlockSpec((None, group) + dims, lambda bg, c: (layer, bg) + (0,) * len(dims))


def _alias_io(s_buf, n_inputs, out_index):
    if s_buf is None:
        return [], [], {}
    return [pl.BlockSpec(memory_space=pl.ANY)], [s_buf], {n_inputs: out_index}


def _stacked(ref):
    g, rows, cols = ref.shape
    return ref[...].reshape(g * rows, cols)


def _seq_masks(rows, seq_rows):
    shift = seq_rows.bit_length() - 1
    r, c = _iota2((rows, rows), 0), _iota2((rows, rows), 1)
    same = (r >> shift) == (c >> shift)
    return same, same & (c <= r)


def _gla_kernel(qk_ref, v_ref, og_ref, glo_ref, w2_ref, gb_ref, ng_ref, s0_ref, *rest, n_pad, sub):
    o_ref, s_ref, st_ref = rest[-3:]
    c = pl.program_id(1)
    group, chunk = qk_ref.shape[0], qk_ref.shape[1]
    rows = group * chunk
    nsub = chunk // sub
    chains = [(g, h) for g in range(group) for h in range(GLA_HEADS)]
    ks = [slice(h * GLA_HK, (h + 1) * GLA_HK) for h in range(GLA_HEADS)]
    vs = [slice(h * GLA_HV, (h + 1) * GLA_HV) for h in range(GLA_HEADS)]
    rs = [slice(g * chunk, (g + 1) * chunk) for g in range(group)]

    @pl.when(c == 0)
    def _():
        for i, (g, h) in enumerate(chains):
            st_ref[i] = s0_ref[g, h].T

    valid = (c * chunk + _seq_rows((rows, 1), chunk)) >= n_pad
    pre = _mm3(_stacked(glo_ref), w2_ref[...]) + gb_ref[...]
    log_a = -_softplus(-pre) * (1.0 / GLA_GATE_NORM)
    log_a = jnp.where(valid, log_a, 0.0)
    qk = _stacked(qk_ref)
    q = qk[:, :GLA_DK] * (GLA_HK ** -0.5)
    k = jnp.where(valid, qk[:, GLA_DK:], 0.0)
    same, causal = _seq_masks(rows, sub)
    b = _mm_const_lhs(causal.astype(F32), log_a, 3)
    b_tot = _mm_const_lhs(same.astype(F32), log_a, 3)
    qd = (q * jnp.exp(b)).astype(BF16)
    kd = (k * jnp.exp(-b)).astype(BF16)
    kend = (k * jnp.exp(b_tot - b)).astype(BF16)
    vb = _stacked(v_ref).astype(BF16)
    causal_c = _seq_masks(chunk, sub)[1]
    att = [jnp.where(causal_c, _dg(qd[rs[g], ks[h]], kd[rs[g], ks[h]], 'nt'), 0.0).astype(BF16) for g, h in chains]
    o = [_dg(att[i], vb[rs[g], vs[h]]) for i, (g, h) in enumerate(chains)]
    s = [st_ref[i] for i in range(len(chains))]
    o_inter = [[] for _ in chains]
    for j in range(nsub):
        for i, (g, h) in enumerate(chains):
            sl = slice(g * chunk + j * sub, g * chunk + (j + 1) * sub)
            o_inter[i].append(_dg(qd[sl, ks[h]], s[i].astype(BF16), 'nt'))
        nxt = []
        for i, (g, h) in enumerate(chains):
            sl = slice(g * chunk + j * sub, g * chunk + (j + 1) * sub)
            dec = jnp.exp(b_tot[g * chunk + j * sub:g * chunk + j * sub + 1, ks[h]])
            nxt.append(s[i] * dec + _dg(vb[sl, vs[h]], kend[sl, ks[h]], 'tn'))
        s = nxt
    og = _stacked(og_ref)
    for i, (g, h) in enumerate(chains):
        st_ref[i] = s[i]
        oh = o[i] + (jnp.concatenate(o_inter[i], axis=0) if nsub > 1 else o_inter[i][0])
        oh = oh * lax.rsqrt(jnp.mean(oh * oh, axis=-1, keepdims=True) + NORM_EPS) * ng_ref[:, vs[h]]
        o_ref[g, :, vs[h]] = oh * _silu(og[rs[g], vs[h]])

    @pl.when(c == pl.num_programs(1) - 1)
    def _():
        for i, (g, h) in enumerate(chains):
            s_ref[g, h] = st_ref[i].T


def _gla(z, w2, gate_b, norm_g, s_all, layer, s_buf, n_pad, chunk, group):
    batch, seq_len, _ = z.shape
    nc = seq_len // chunk
    sub = math.gcd(chunk, GLA_SUBCHUNK)
    row = lambda blk: (lambda bg, c: (bg, c, blk))
    const = lambda bg, c: (0, 0)
    st = _state_spec(layer, group, (GLA_HEADS, GLA_HK, GLA_HV))
    w2p = jnp.zeros((LANE, GLA_DK), F32).at[:GLA_GATE_RANK].set(w2)
    alias_specs, alias_args, aliases = _alias_io(s_buf, 8, 1)
    return pl.pallas_call(
        functools.partial(_gla_kernel, n_pad=n_pad, sub=sub),
        grid=(batch // group, nc),
        in_specs=[pl.BlockSpec((group, chunk, 2 * GLA_DK), row(0)),
                  pl.BlockSpec((group, chunk, GLA_DV), row(1)),
                  pl.BlockSpec((group, chunk, GLA_DV), row(2)),
                  pl.BlockSpec((group, chunk, LANE), row(3 * 512 // LANE)),
                  pl.BlockSpec((LANE, GLA_DK), const),
                  pl.BlockSpec((1, GLA_DK), const),
                  pl.BlockSpec((1, GLA_DV), const),
                  st] + alias_specs,
        out_specs=[pl.BlockSpec((group, chunk, GLA_DV), row(0)), st],
        out_shape=[jax.ShapeDtypeStruct((batch, seq_len, GLA_DV), F32),
                   jax.ShapeDtypeStruct(s_all.shape, F32)],
        scratch_shapes=[pltpu.VMEM((group * GLA_HEADS, GLA_HV, GLA_HK), F32)],
        input_output_aliases=aliases,
        compiler_params=_params(2),
        name="gla",
    )(z, z, z, z, w2p, gate_b.reshape(1, GLA_DK), norm_g.reshape(1, GLA_DV), s_all, *alias_args)


def _rwkv_kernel(z_ref, mu_ref, w0_ref, w2_ref, a0_ref, a2_ref, g2_ref, kk_ref, ka_ref, rk_ref, lg_ref, lb_ref,
                 s0_ref, *rest, n_pad):
    y_ref, s_ref, st_ref, carry_ref = rest[-4:]
    c = pl.program_id(1)
    group, chunk = z_ref.shape[0], z_ref.shape[1]
    dim = RWKV_DIM
    pairs = RWKV_HEADS // 2

    @pl.when(c == 0)
    def _():
        zero = jnp.zeros((RWKV_HEAD, RWKV_HEAD), F32)
        for g in range(group):
            for q in range(pairs):
                top = jnp.concatenate([s0_ref[g, 2 * q], zero], axis=1)
                bottom = jnp.concatenate([zero, s0_ref[g, 2 * q + 1]], axis=1)
                st_ref[g * pairs + q] = jnp.concatenate([top, bottom], axis=0)
        carry_ref[...] = jnp.zeros_like(carry_ref)

    def prep(g0, g1):
        rows = (g1 - g0) * chunk
        valid = (c * chunk + _seq_rows((rows, 1), chunk)) >= n_pad
        z = z_ref[g0:g1].reshape(rows, z_ref.shape[2])
        z_prev = _shift_rows_seqs(z, [carry_ref[g] for g in range(g0, g1)], 1)
        for g in range(g0, g1):
            carry_ref[g] = z[(g - g0 + 1) * chunk - SUBLANE:(g - g0 + 1) * chunk]
        z = z + mu_ref[...] * (z_prev - z)
        r, kr, vr = z[:, :dim], z[:, dim:2 * dim], z[:, 2 * dim:3 * dim]
        w_lo = z[:, 3 * dim:3 * dim + 64]
        a_lo = z[:, 3 * dim + 64:3 * dim + 128]
        g_lo = z[:, 3 * dim + 128:]
        w_raw = w0_ref[...] + _mm3(jnp.tanh(w_lo), w2_ref[...])
        log_w = -jnp.exp(-_softplus(-w_raw) - 0.5)
        a_lr = _sigmoid(a0_ref[...] + _mm3(a_lo, a2_ref[...]))
        gate = _mm3(_sigmoid(g_lo), g2_ref[...])
        kx = kr * kk_ref[...]
        kk = kx * lax.rsqrt(_head_sums(kx * kx, RWKV_HEAD, 2) + NORM_EPS)
        k_mod = kr * (1.0 + (a_lr - 1.0) * ka_ref[...])
        bonus = _head_sums(r * k_mod * rk_ref[...], RWKV_HEAD, 2) * vr
        log_w = jnp.where(valid, log_w, 0.0)
        k_in = jnp.where(valid, k_mod, 0.0)
        b_in = jnp.where(valid, kk * a_lr, 0.0)
        same, causal = _seq_masks(rows, chunk)
        c_incl = _mm_const_lhs(causal.astype(F32), log_w, 3)
        c_tot = _mm_const_lhs(same.astype(F32), log_w, 3)
        e_neg = jnp.exp(-c_incl)
        e_end = jnp.exp(c_tot - c_incl)
        return dict(a_t=-kk * jnp.exp(c_incl - log_w), r_t=r * jnp.exp(c_incl),
                    bt=b_in * e_neg, kt=k_in * e_neg,
                    b_e=b_in * e_end, k_e=k_in * e_end, vr=vr, c_tot=c_tot,
                    bonus=bonus, gate=gate)

    def chains(p, g0, g1):
        ch = [(g, q) for g in range(g1 - g0) for q in range(pairs)]
        rs = [slice(g * chunk, (g + 1) * chunk) for g in range(g1 - g0)]
        pb = [slice(q * LANE, (q + 1) * LANE) for q in range(pairs)]
        n = len(ch)
        slot = [(g0 + g) * pairs + q for g, q in ch]
        col = _iota2((chunk, 2 * chunk), 1) & (chunk - 1)
        row = _iota2((chunk, 2 * chunk), 0)
        strict2, incl2 = col < row, col <= row
        blk = lambda x, g, q: _pair_blocks(x[rs[g], pb[q]], RWKV_HEAD).astype(BF16)
        s = [st_ref[i] for i in slot]
        ar = [jnp.concatenate([p['a_t'][rs[g], pb[q]], p['r_t'][rs[g], pb[q]]], axis=0).astype(BF16) for g, q in ch]
        vbd = [blk(p['vr'], g, q) for g, q in ch]
        g_b = [_dg(ar[i], blk(p['bt'], g, q), 'nt') for i, (g, q) in enumerate(ch)]
        g_k = [_dg(ar[i], blk(p['kt'], g, q), 'nt') for i, (g, q) in enumerate(ch)]
        g_s = [_dg(ar[i], s[i].astype(BF16), 'nt') for i in range(n)]
        m_ab = [jnp.where(strict2, g_b[i][:chunk], 0.0) for i in range(n)]
        rhs = [g_s[i][:chunk] + _mm(jnp.where(strict2, g_k[i][:chunk], 0.0), vbd[i]) for i in range(n)]
        t_inv = _unit_lower_inverse_pairs(m_ab)
        u = [_mm(t_inv[i], _pair_blocks(rhs[i], RWKV_HEAD)) for i in range(n)]
        ys = [g_s[i][chunk:] + _mm(jnp.where(incl2, g_b[i][chunk:], 0.0), _pair_blocks(u[i], RWKV_HEAD))
              + _mm(jnp.where(incl2, g_k[i][chunk:], 0.0), vbd[i]) for i in range(n)]
        same_head = (_iota2((LANE, LANE), 0) < RWKV_HEAD) == (_iota2((LANE, LANE), 1) < RWKV_HEAD)
        for i, (g, q) in enumerate(ch):
            uv = jnp.concatenate([u[i], p['vr'][rs[g], pb[q]]], axis=0)
            bk = jnp.concatenate([p['b_e'][rs[g], pb[q]], p['k_e'][rs[g], pb[q]]], axis=0)
            w_end = jnp.exp(p['c_tot'][g * chunk:g * chunk + 1, pb[q]])
            st_ref[slot[i]] = s[i] * w_end + jnp.where(same_head, _mm(uv, bk, 'tn'), 0.0)
        per_seq = [jnp.concatenate(ys[g * pairs:(g + 1) * pairs], axis=1) for g in range(g1 - g0)]
        return jnp.concatenate(per_seq, axis=0) if g1 - g0 > 1 else per_seq[0]

    def finish(p, y, g0, g1):
        mean = _head_sums(y, RWKV_HEAD, 2) * (1.0 / RWKV_HEAD)
        d = y - mean
        var = _head_sums(d * d, RWKV_HEAD, 2) * (1.0 / RWKV_HEAD)
        y = d * lax.rsqrt(var + RWKV_GN_EPS) * lg_ref[...] + lb_ref[...]
        y_ref[g0:g1] = ((y + p['bonus']) * p['gate']).reshape(g1 - g0, chunk, dim)

    p = prep(0, group)
    finish(p, chains(p, 0, group), 0, group)

    @pl.when(c == pl.num_programs(1) - 1)
    def _():
        for g in range(group):
            for q in range(pairs):
                s_ref[g, 2 * q] = st_ref[g * pairs + q, :RWKV_HEAD, :RWKV_HEAD]
                s_ref[g, 2 * q + 1] = st_ref[g * pairs + q, RWKV_HEAD:, RWKV_HEAD:]


def _rwkv(z, p, s_all, layer, s_buf, n_pad, chunk, group):
    batch, seq_len, _ = z.shape
    nc = seq_len // chunk
    dim = RWKV_DIM
    row = lambda bg, c: (bg, c, 0)
    const = lambda bg, c: (0, 0)
    st = _state_spec(layer, group, (RWKV_HEADS, RWKV_HEAD, RWKV_HEAD))
    vec = lambda a: a.reshape(1, -1)
    small = [vec(p['mu']), vec(p['w0']), p['w2'], vec(p['a0']), p['a2'], p['g2'], vec(p['k_k']), vec(p['k_a']),
             vec(p['r_k']), vec(p['ln_g']), vec(p['ln_b'])]
    alias_specs, alias_args, aliases = _alias_io(s_buf, len(small) + 2, 1)
    return pl.pallas_call(
        functools.partial(_rwkv_kernel, n_pad=n_pad),
        grid=(batch // group, nc),
        in_specs=([pl.BlockSpec((group, chunk, RWKV_COLS), row)]
                  + [pl.BlockSpec(a.shape, const) for a in small]
                  + [st] + alias_specs),
        out_specs=[pl.BlockSpec((group, chunk, dim), row), st],
        out_shape=[jax.ShapeDtypeStruct((batch, seq_len, dim), F32),
                   jax.ShapeDtypeStruct(s_all.shape, F32)],
        scratch_shapes=[pltpu.VMEM((group * RWKV_HEADS // 2, LANE, LANE), F32),
                        pltpu.VMEM((group, SUBLANE, RWKV_COLS), F32)],
        input_output_aliases=aliases,
        compiler_params=_params(2),
        name="rwkv7",
    )(z, *small, s_all, *alias_args)


def _gdn_kernel(qkv_ref, rest_ref, alog_ref, dt_ref, ng_ref, s0_ref, *rest, n_pad):
    o_ref, s_ref, st_ref = rest[-3:]
    c = pl.program_id(1)
    group, chunk = qkv_ref.shape[0], qkv_ref.shape[1]
    rows = group * chunk
    dim = GDN_DIM
    chains = [(g, h) for g in range(group) for h in range(GDN_HEADS)]
    hs = [slice(h * GDN_HEAD, (h + 1) * GDN_HEAD) for h in range(GDN_HEADS)]
    rs = [slice(g * chunk, (g + 1) * chunk) for g in range(group)]

    @pl.when(c == 0)
    def _():
        for i, (g, h) in enumerate(chains):
            st_ref[i] = s0_ref[g, h]

    valid = (c * chunk + _seq_rows((rows, 1), chunk)) >= n_pad
    qkv = _stacked(qkv_ref)
    rest = _stacked(rest_ref)
    zg = rest[:, :dim]
    lo = rest[:, dim:]
    beta_all = _sigmoid(lo)
    g_all = -jnp.exp(alog_ref[...]) * _softplus(lo + dt_ref[...])
    g_all = jnp.where(valid, g_all, 0.0)

    incl = _tri(chunk)
    strict = _tri(chunk, strict=True)
    causal = _seq_masks(rows, chunk)[1]
    gam_all = _mm_const_lhs(causal.astype(F32), g_all, 3)
    shift = chunk.bit_length() - 1
    pick = (_iota2((GDN_HEADS * chunk, LANE), 1)
            == GDN_HEADS + (_iota2((GDN_HEADS * chunk, LANE), 0) >> shift)).astype(F32)
    gam_cols = [_mm_const_lhs(pick, gam_all[rs[g]], 3, 'nt') for g in range(group)]
    q_sq = _head_sums(jnp.square(qkv[:, :dim]), GDN_HEAD, 2)
    k_sq = _head_sums(jnp.square(qkv[:, dim:2 * dim]), GDN_HEAD, 2)
    q_all = qkv[:, :dim] * lax.rsqrt(q_sq + NORM_EPS) * (GDN_HEAD ** -0.5)
    k_all = jnp.where(valid, qkv[:, dim:2 * dim] * lax.rsqrt(k_sq + NORM_EPS), 0.0)
    v_all = qkv[:, 2 * dim:]
    n = len(chains)
    gam = [gam_all[rs[g], GDN_HEADS + h:GDN_HEADS + h + 1] for g, h in chains]
    beta = [beta_all[rs[g], h:h + 1] for g, h in chains]
    kq = [jnp.concatenate([k_all[rs[g], hs[h]], q_all[rs[g], hs[h]]], axis=0).astype(BF16) for g, h in chains]
    dec = [jnp.where(incl, jnp.exp(jnp.where(incl, gam[i] - gam_cols[g][h * chunk:(h + 1) * chunk], 0.0)), 0.0)
           for i, (g, h) in enumerate(chains)]
    gram = [_dg(kq[i], k_all[rs[g], hs[h]].astype(BF16), 'nt') for i, (g, h) in enumerate(chains)]
    m = [jnp.where(strict, gram[i][:chunk] * dec[i], 0.0) * (-beta[i]) for i in range(n)]
    t_inv = _unit_lower_inverses(m)
    e_gam = [jnp.exp(gam[i]) for i in range(n)]
    rhs = [jnp.concatenate([v_all[rs[g], hs[h]] * beta[i], k_all[rs[g], hs[h]] * (beta[i] * e_gam[i])], axis=1)
           for i, (g, h) in enumerate(chains)]
    uw = [_mm(t_inv[i], rhs[i]) for i in range(n)]
    s = [st_ref[i] for i in range(n)]
    sb = [s[i].astype(BF16) for i in range(n)]
    delta = [uw[i][:, :GDN_HEAD] - _mm(uw[i][:, GDN_HEAD:], sb[i]) for i in range(n)]
    o = [_mm(q_all[rs[g], hs[h]] * e_gam[i], sb[i]) + _mm(gram[i][chunk:] * dec[i], delta[i])
         for i, (g, h) in enumerate(chains)]
    for i, (g, h) in enumerate(chains):
        g_end = gam[i][chunk - 1:chunk, :]
        st_ref[i] = s[i] * jnp.exp(g_end) + _mm(k_all[rs[g], hs[h]] * jnp.exp(g_end - gam[i]), delta[i], 'tn')
        oh = o[i] * lax.rsqrt(jnp.mean(o[i] * o[i], axis=-1, keepdims=True) + NORM_EPS) * ng_ref[...]
        o_ref[g, :, hs[h]] = oh * _silu(zg[rs[g], hs[h]])

    @pl.when(c == pl.num_programs(1) - 1)
    def _():
        for i, (g, h) in enumerate(chains):
            s_ref[g, h] = st_ref[i]


def _gdn(qkv, rest, a_log, dt_bias, norm_g, s_all, layer, s_buf, n_pad, chunk, group):
    batch, seq_len, _ = qkv.shape
    nc = seq_len // chunk
    dim = GDN_DIM
    row = lambda bg, c: (bg, c, 0)
    const = lambda bg, c: (0, 0)
    st = _state_spec(layer, group, (GDN_HEADS, GDN_HEAD, GDN_HEAD))
    pad_lo = lambda a: jnp.zeros((1, LANE), F32).at[0, GDN_HEADS:2 * GDN_HEADS].set(a)
    alias_specs, alias_args, aliases = _alias_io(s_buf, 6, 1)
    return pl.pallas_call(
        functools.partial(_gdn_kernel, n_pad=n_pad),
        grid=(batch // group, nc),
        in_specs=[pl.BlockSpec((group, chunk, 3 * dim), row),
                  pl.BlockSpec((group, chunk, dim + LANE), row),
                  pl.BlockSpec((1, LANE), const),
                  pl.BlockSpec((1, LANE), const),
                  pl.BlockSpec((1, GDN_HEAD), const),
                  st] + alias_specs,
        out_specs=[pl.BlockSpec((group, chunk, dim), row), st],
        out_shape=[jax.ShapeDtypeStruct((batch, seq_len, dim), F32),
                   jax.ShapeDtypeStruct(s_all.shape, F32)],
        scratch_shapes=[pltpu.VMEM((group * GDN_HEADS, GDN_HEAD, GDN_HEAD), F32)],
        input_output_aliases=aliases,
        compiler_params=_params(2),
        name="gated_deltanet",
    )(qkv, rest, pad_lo(a_log), pad_lo(dt_bias), norm_g.reshape(1, GDN_HEAD), s_all, *alias_args)


def _pad_rows_fix(buf, seq_len, n_pad):
    b, w, c = buf.shape
    return jnp.zeros((b, seq_len, c), F32).at[:, n_pad - w:n_pad].set(buf).reshape(b * seq_len, c)


def _trunk(x, states, wts, batch, seq_len, n_pad, chunk, group, carried):
    st_gla, st_rwkv, st_shift, st_gdn, st_gdn_conv, st_ffn_conv = states
    d = x.shape[1]
    rows = batch * seq_len
    tm = _seq_tile(seq_len, rows, carried)
    tile_tail = lambda a, w: a.reshape(batch, seq_len // tm, SUBLANE, -1)[:, -1, SUBLANE - w:]
    seq_tail = lambda a, w: a.reshape(batch, seq_len, -1)[:, seq_len - w:]
    conv_rows = seq_tail if carried else tile_tail
    seqs = lambda a: a.reshape(batch, seq_len, a.shape[-1])
    flat = lambda a: a.reshape(rows, a.shape[-1])
    new_gla = new_rwkv = new_gdn = None
    new = {k: [] for k in ('shift', 'gdn_conv', 'ffn_conv')}
    for l in range(DEPTH):
        i = l // 2
        if l % 2 == 0:
            new['shift'].append(seqs(x)[:, -1])
            x_in = x
            if carried:
                x_in = flat(seqs(x).at[:, n_pad - 1].set(st_shift[i]))
            z_gla, z_rwkv = _inproj(x_in, wts['w_in_ab'], i, (GLA_Z, RWKV_COLS))
            o_gla, new_gla = _gla(seqs(z_gla), wts['gla_gate_w2'][i], wts['gla_gate_b'][i], wts['gla_norm_g'][i],
                                  st_gla, i, new_gla, n_pad, chunk, group)
            y_rwkv, new_rwkv = _rwkv(seqs(z_rwkv), {k: wts['rwkv_' + k][i] for k in
                                                    ('mu', 'w0', 'w2', 'a0', 'a2', 'g2', 'k_k', 'k_a', 'r_k', 'ln_g',
                                                     'ln_b')},
                                     st_rwkv, i, new_rwkv, n_pad, chunk, group)
            mixed = (flat(o_gla), flat(y_rwkv))
            w_out = wts['w_out_ab']
        else:
            fix = _pad_rows_fix(st_gdn_conv[i], seq_len, n_pad) if carried else None
            qkv, rest, pre = _inproj_gdn(x, wts['w_in_c'], i, wts['gdn_conv_w'][i], fix, tm, carried)
            new['gdn_conv'].append(conv_rows(pre, GDN_CONV - 1))
            o, new_gdn = _gdn(seqs(qkv), seqs(rest), wts['gdn_A_log'][i], wts['gdn_dt_bias'][i],
                              wts['gdn_norm_g'][i], st_gdn, i, new_gdn, n_pad, chunk, group)
            mixed = (flat(o),)
            w_out = wts['w_out_c']
        fix = _pad_rows_fix(st_ffn_conv[l], seq_len, n_pad) if carried else None
        x, gate = _layer_tail(mixed, x, w_out, i, (wts['ln_mix_g'][l], wts['ln_mix_b'][l]),
                              wts['w_up'], wts['w_down'], l, wts['ffn_conv_w'][l], wts['ffn_conv_b'][l],
                              (wts['ln_ffn_g'][l], wts['ln_ffn_b'][l]), fix, seq_len, n_pad, tm, carried)
        new['ffn_conv'].append(conv_rows(gate, FFN_CONV - 1))
    return (x, new_gla, new_rwkv, jnp.stack(new['shift']), new_gdn, jnp.stack(new['gdn_conv']),
            jnp.stack(new['ffn_conv']))


def kernel(x_prompt, x_sample, state_gla, state_rwkv, state_rwkv_shift, state_gdn, state_gdn_conv, state_ffn_conv, meta_tokens, w_in_ab, gla_gate_w2, gla_gate_b, gla_norm_g, rwkv_mu, rwkv_w0, rwkv_w2, rwkv_a0, rwkv_a2, rwkv_g2, rwkv_k_k, rwkv_k_a, rwkv_r_k, rwkv_ln_g, rwkv_ln_b, w_out_ab, w_in_c, gdn_conv_w, gdn_A_log, gdn_dt_bias, gdn_norm_g, w_out_c, w_up, ffn_conv_w, ffn_conv_b, w_down, ln_mix_g, ln_mix_b, ln_ffn_g, ln_ffn_b):
    d = x_prompt.shape[-1]
    n_ab, n_c = w_in_ab.shape[0], w_in_c.shape[0]
    gla_cols = 2 * GLA_DK + 2 * GLA_DV + GLA_GATE_RANK
    lo0 = 2 * GLA_DK + GLA_DV
    w_ab = jnp.concatenate([w_in_ab[:, :, :lo0], w_in_ab[:, :, lo0 + GLA_GATE_RANK:gla_cols],
                            w_in_ab[:, :, lo0:lo0 + GLA_GATE_RANK],
                            jnp.zeros((n_ab, d, LANE - GLA_GATE_RANK), w_in_ab.dtype),
                            w_in_ab[:, :, gla_cols:]], axis=2).astype(BF16)
    w_c = jnp.concatenate([w_in_c, jnp.zeros((n_c, d, LANE - 2 * GDN_HEADS), w_in_c.dtype)], axis=2).astype(BF16)
    wts = {
        'w_in_ab': w_ab, 'gla_gate_w2': gla_gate_w2, 'gla_gate_b': gla_gate_b, 'gla_norm_g': gla_norm_g,
        'rwkv_mu': rwkv_mu, 'rwkv_w0': rwkv_w0, 'rwkv_w2': rwkv_w2, 'rwkv_a0': rwkv_a0, 'rwkv_a2': rwkv_a2,
        'rwkv_g2': rwkv_g2, 'rwkv_k_k': rwkv_k_k, 'rwkv_k_a': rwkv_k_a, 'rwkv_r_k': rwkv_r_k,
        'rwkv_ln_g': rwkv_ln_g, 'rwkv_ln_b': rwkv_ln_b, 'w_out_ab': w_out_ab.astype(BF16),
        'w_in_c': w_c, 'gdn_conv_w': gdn_conv_w, 'gdn_A_log': gdn_A_log, 'gdn_dt_bias': gdn_dt_bias,
        'gdn_norm_g': gdn_norm_g, 'w_out_c': w_out_c.astype(BF16),
        'w_up': w_up.astype(BF16), 'ffn_conv_w': ffn_conv_w, 'ffn_conv_b': ffn_conv_b,
        'w_down': w_down.astype(BF16),
        'ln_mix_g': ln_mix_g, 'ln_mix_b': ln_mix_b, 'ln_ffn_g': ln_ffn_g, 'ln_ffn_b': ln_ffn_b,
    }

    bp, seq, _ = x_prompt.shape
    tp = -(-(N_META + seq + SUBLANE) // PROMPT_CHUNK) * PROMPT_CHUNK
    pad_p = tp - N_META - seq
    meta = jnp.broadcast_to(meta_tokens.astype(F32)[None], (bp, N_META, d))
    xp = jnp.concatenate([jnp.zeros((bp, pad_p, d), F32), meta, x_prompt], axis=1).reshape(bp * tp, d)
    zero_state = lambda s: jnp.zeros((s.shape[0], bp) + s.shape[2:], F32)
    p_out = _trunk(xp, tuple(zero_state(s) for s in (state_gla, state_rwkv, state_rwkv_shift, state_gdn,
                                                      state_gdn_conv, state_ffn_conv)),
                   wts, bp, tp, pad_p, PROMPT_CHUNK, _group(bp, PROMPT_GROUP), carried=False)
    y_prompt = p_out[0].reshape(bp, tp, d)[:, pad_p + N_META:]

    bs, ts, _ = x_sample.shape
    tsp = -(-(ts + GDN_CONV - 1) // SUBLANE) * SUBLANE
    pad_s = tsp - ts
    xs = jnp.concatenate([jnp.zeros((bs, pad_s, d), F32), x_sample], axis=1).reshape(bs * tsp, d)
    s_out = _trunk(xs, (state_gla, state_rwkv, state_rwkv_shift, state_gdn, state_gdn_conv, state_ffn_conv),
                   wts, bs, tsp, pad_s, tsp, _group(bs, SAMPLE_GROUP), carried=True)
    y_sample = s_out[0].reshape(bs, tsp, d)[:, pad_s:]
    return (y_prompt, y_sample) + tuple(p_out[1:]) + tuple(s_out[1:])
```

```python
import functools
import math

import jax
import jax.numpy as jnp
from jax import lax
from jax.experimental import pallas as pl
from jax.experimental.pallas import tpu as pltpu

F32 = jnp.float32
BF16 = jnp.bfloat16

N_META = 16
GLA_HEADS, GLA_HK, GLA_HV = 4, 64, 128
GLA_DK, GLA_DV = GLA_HEADS * GLA_HK, GLA_HEADS * GLA_HV
GLA_GATE_RANK = 16
GLA_GATE_NORM = 16.0
GLA_SUBCHUNK = 16
RWKV_HEADS, RWKV_HEAD = 8, 64
RWKV_DIM = RWKV_HEADS * RWKV_HEAD
RWKV_COLS = 3 * RWKV_DIM + 64 + 64 + 128
RWKV_GN_EPS = 64e-5
GDN_HEADS, GDN_HEAD = 8, 128
GDN_DIM = GDN_HEADS * GDN_HEAD
GDN_CONV = 4
FFN_CONV = 3
LN_EPS = 1e-5
NORM_EPS = 1e-6
DEPTH = 4
DEEPNORM_ALPHA = (2.0 * DEPTH) ** 0.25

LANE = 128
SUBLANE = 8
PROMPT_CHUNK = 64
PROMPT_GROUP = 4
SAMPLE_GROUP = 8
VMEM_LIMIT = 56 * 1024 * 1024

GLA_Z = 3 * 512 + LANE


def _params(n_axes):
    return pltpu.CompilerParams(dimension_semantics=("arbitrary",) * n_axes,
                                vmem_limit_bytes=VMEM_LIMIT)


def _row_tile(rows, cap):
    t = cap
    while rows % t:
        t //= 2
    return t


def _seq_tile(seq_len, rows, carried):
    if carried:
        return _row_tile(rows, 256)
    return next(t for t in (704, 352, 192, 64) if seq_len % t == 0)


def _resident(shape, layer=None):
    if layer is None:
        return pl.BlockSpec(shape, lambda *_: (0,) * len(shape), pipeline_mode=pl.Buffered(1))
    return pl.BlockSpec((None,) + tuple(shape), lambda *_: (layer,) + (0,) * len(shape),
                        pipeline_mode=pl.Buffered(1))


_DIMS = {'nn': (((1,), (0,)), ((), ())),
         'nt': (((1,), (1,)), ((), ())),
         'tn': (((0,), (0,)), ((), ()))}


def _dg(a, b, kind='nn'):
    return lax.dot_general(a, b, _DIMS[kind], preferred_element_type=F32)


def _split(x, pieces):
    out = []
    for _ in range(pieces - 1):
        hi = x.astype(BF16)
        out.append(hi)
        x = x - hi.astype(F32)
    out.append(x.astype(BF16))
    return out


def _mm(a, b, kind='nn'):
    return _dg(a.astype(BF16), b.astype(BF16), kind)


def _mm3(a, b, kind='nn'):
    ah, al = _split(a, 2)
    bh, bl = _split(b, 2)
    return _dg(ah, bh, kind) + _dg(al, bh, kind) + _dg(ah, bl, kind)


def _mm_const_lhs(c, x, pieces, kind='nn'):
    cb = c.astype(BF16)
    return sum(_dg(cb, p, kind) for p in _split(x, pieces))


def _mm_const_rhs(x, c, pieces, kind='nn'):
    cb = c.astype(BF16)
    return sum(_dg(p, cb, kind) for p in _split(x, pieces))


def _bdot(a, w):
    return jnp.dot(a.astype(BF16), w, preferred_element_type=F32)


def _iota2(shape, dim):
    return lax.broadcasted_iota(jnp.int32, shape, dim)


def _tri(n, strict=False):
    r, c = _iota2((n, n), 0), _iota2((n, n), 1)
    return (c < r) if strict else (c <= r)


def _softplus(x):
    return jnp.maximum(x, 0.0) + jnp.log1p(jnp.exp(-jnp.abs(x)))


def _sigmoid(x):
    return 1.0 / (1.0 + jnp.exp(-x))


def _silu(x):
    return x * _sigmoid(x)


def _unit_lower_inverses(ms):
    n = ms[0].shape[0]
    eye = (_iota2((n, n), 0) == _iota2((n, n), 1)).astype(F32)
    xs = [eye + m for m in ms]
    ps = list(ms)
    for _ in range(int(math.log2(n)) - 1):
        ps = [_mm(p, p) for p in ps]
        xs = [x + _mm(x, p) for x, p in zip(xs, ps)]
    return xs


def _pair_blocks(x, split):
    first = _iota2(x.shape, 1) < split
    return jnp.concatenate([jnp.where(first, x, 0.0), jnp.where(first, 0.0, x)], axis=0)


def _unit_lower_inverse_pairs(ms):
    n = ms[0].shape[0]
    eye = ((_iota2((n, 2 * n), 1) & (n - 1)) == _iota2((n, 2 * n), 0)).astype(F32)
    xs = [eye + m for m in ms]
    ps = list(ms)
    bds = [_pair_blocks(p, n).astype(BF16) for p in ps]
    for _ in range(int(math.log2(n)) - 1):
        ps = [_mm(p, bd) for p, bd in zip(ps, bds)]
        bds = [_pair_blocks(p, n).astype(BF16) for p in ps]
        xs = [x + _mm(x, bd) for x, bd in zip(xs, bds)]
    return xs


def _head_sums(x, width, pieces):
    rows, cols = x.shape
    nb = cols // LANE
    shift = width.bit_length() - 1
    grp = (_iota2((LANE, LANE), 0) >> shift == _iota2((LANE, LANE), 1) >> shift).astype(F32)
    xs = jnp.concatenate([x[:, i * LANE:(i + 1) * LANE] for i in range(nb)], axis=0)
    s = _mm_const_rhs(xs, grp, pieces)
    return jnp.concatenate([s[i * rows:(i + 1) * rows] for i in range(nb)], axis=1)


def _shift_rows(cur, carry, k):
    rolled = pltpu.roll(cur, k, 0)
    head = jnp.where(_iota2((SUBLANE, 1), 0) < k, pltpu.roll(carry, k, 0), rolled[:SUBLANE])
    if cur.shape[0] == SUBLANE:
        return head
    return jnp.concatenate([head, rolled[SUBLANE:]], axis=0)


def _shift_rows_seqs(cur, carries, k):
    n = len(carries)
    rows = cur.shape[0] // n
    return jnp.concatenate([_shift_rows(cur[g * rows:(g + 1) * rows], carries[g], k) for g in range(n)], axis=0) \
        if n > 1 else _shift_rows(cur, carries[0], k)


def _seq_rows(shape, seq_rows):
    return _iota2(shape, 0) & (seq_rows - 1)


def _valid_rows(row0, rows, seq_len, n_pad):
    if seq_len & (seq_len - 1) == 0:
        t = (row0 + _iota2((rows, 1), 0)) & (seq_len - 1)
    else:
        t = lax.rem(row0, seq_len) + _iota2((rows, 1), 0)
        for _ in range(-(-rows // seq_len)):
            t = jnp.where(t >= seq_len, t - seq_len, t)
    return t >= n_pad


def _layer_norm_rows(h, g, b):
    mu = jnp.mean(h, axis=-1, keepdims=True)
    d = h - mu
    var = jnp.mean(d * d, axis=-1, keepdims=True)
    return d * lax.rsqrt(var + LN_EPS) * g + b


def _inproj_kernel(x_ref, w_ref, *out_refs, widths):
    z = _bdot(x_ref[...], w_ref[...])
    off = 0
    for o_ref, wd in zip(out_refs, widths):
        o_ref[...] = z[:, off:off + wd]
        off += wd


def _inproj(x, w, layer, widths):
    rows, d = x.shape
    n = w.shape[2]
    tm = _row_tile(rows, 512)
    return pl.pallas_call(
        functools.partial(_inproj_kernel, widths=widths),
        grid=(rows // tm,),
        in_specs=[pl.BlockSpec((tm, d), lambda i: (i, 0)), _resident((d, n), layer)],
        out_specs=[pl.BlockSpec((tm, wd), lambda i: (i, 0)) for wd in widths],
        out_shape=[jax.ShapeDtypeStruct((rows, wd), F32) for wd in widths],
        compiler_params=_params(1),
        name="inproj",
    )(x, w)


def _inproj_gdn_kernel(*refs, has_fix, full_pre):
    if has_fix:
        x_ref, w_ref, cw_ref, fix_ref, act_ref, rest_ref, pre_ref, carry_ref = refs
    else:
        x_ref, w_ref, cw_ref, act_ref, rest_ref, pre_ref, carry_ref = refs
    tm = x_ref.shape[0]
    width = act_ref.shape[1]

    @pl.when(pl.program_id(0) == 0)
    def _():
        carry_ref[...] = jnp.zeros_like(carry_ref)

    xb = x_ref[...].astype(BF16)
    step = 4 * LANE
    for lo in range(0, width, step):
        cs = slice(lo, lo + step)
        pre = jnp.dot(xb, w_ref[:, cs], preferred_element_type=F32)
        if has_fix:
            pre = pre + fix_ref[:, cs]
        carry = carry_ref[:, cs]
        conv = cw_ref[GDN_CONV - 1:GDN_CONV, cs] * pre
        for tap in range(GDN_CONV - 1):
            conv = conv + cw_ref[tap:tap + 1, cs] * _shift_rows(pre, carry, GDN_CONV - 1 - tap)
        carry_ref[:, cs] = pre[tm - SUBLANE:, :]
        act_ref[:, cs] = _silu(conv)
        if full_pre:
            pre_ref[:, cs] = pre
        else:
            pre_ref[0, :, cs] = pre[tm - SUBLANE:, :]
    rest_ref[...] = jnp.dot(xb, w_ref[:, width:], preferred_element_type=F32)


def _inproj_gdn(x, w, layer, conv_w, fix, tm, full_pre):
    rows, d = x.shape
    n = w.shape[2]
    width = conv_w.shape[1]
    nt = rows // tm
    in_specs = [pl.BlockSpec((tm, d), lambda i: (i, 0)), _resident((d, n), layer), _resident(conv_w.shape)]
    args = [x, w, conv_w]
    if fix is not None:
        in_specs.append(pl.BlockSpec((tm, width), lambda i: (i, 0)))
        args.append(fix)
    if full_pre:
        pre_spec = pl.BlockSpec((tm, width), lambda i: (i, 0))
        pre_shape = jax.ShapeDtypeStruct((rows, width), F32)
    else:
        pre_spec = pl.BlockSpec((1, SUBLANE, width), lambda i: (i, 0, 0))
        pre_shape = jax.ShapeDtypeStruct((nt, SUBLANE, width), F32)
    return pl.pallas_call(
        functools.partial(_inproj_gdn_kernel, has_fix=fix is not None, full_pre=full_pre),
        grid=(nt,),
        in_specs=in_specs,
        out_specs=[pl.BlockSpec((tm, width), lambda i: (i, 0)),
                   pl.BlockSpec((tm, n - width), lambda i: (i, 0)),
                   pre_spec],
        out_shape=[jax.ShapeDtypeStruct((rows, width), F32),
                   jax.ShapeDtypeStruct((rows, n - width), F32),
                   pre_shape],
        scratch_shapes=[pltpu.VMEM((SUBLANE, width), F32)],
        compiler_params=_params(1),
        name="inproj_gdn",
    )(*args)


def _tail_kernel(*refs, n_mixed, seq_len, n_pad, d_ff, has_fix, full_gate):
    o_refs, refs = refs[:n_mixed], refs[n_mixed:]
    if has_fix:
        (x_ref, wout_ref, g1_ref, b1_ref, wup_ref, wdn_ref, cw_ref, cb_ref, g2_ref, b2_ref, fix_ref,
         y_ref, gate_ref, carry_ref) = refs
    else:
        (x_ref, wout_ref, g1_ref, b1_ref, wup_ref, wdn_ref, cw_ref, cb_ref, g2_ref, b2_ref,
         y_ref, gate_ref, carry_ref) = refs
    i = pl.program_id(0)
    tm = x_ref.shape[0]

    @pl.when(i == 0)
    def _():
        carry_ref[...] = jnp.zeros_like(carry_ref)

    valid = _valid_rows(i * tm, tm, seq_len, n_pad)
    h = DEEPNORM_ALPHA * x_ref[...]
    off = 0
    for o_ref in o_refs:
        h = h + _bdot(o_ref[...], wout_ref[off:off + o_ref.shape[1], :])
        off += o_ref.shape[1]
    x = jnp.where(valid, _layer_norm_rows(h, g1_ref[...], b1_ref[...]), 0.0)

    hu = _bdot(x, wup_ref[...])
    gate, up = hu[:, :d_ff], hu[:, d_ff:]
    if has_fix:
        gate = gate + fix_ref[...]
    carry = carry_ref[...]
    conv = (cw_ref[0:1, :] * _shift_rows(gate, carry, 2) + cw_ref[1:2, :] * _shift_rows(gate, carry, 1)
            + cw_ref[2:3, :] * gate + cb_ref[...])
    acc = _bdot(_silu(conv) * up, wdn_ref[...])
    carry_ref[...] = gate[tm - SUBLANE:, :]
    if full_gate:
        gate_ref[...] = gate
    else:
        gate_ref[0] = gate[tm - SUBLANE:, :]
    y = _layer_norm_rows(DEEPNORM_ALPHA * x + acc, g2_ref[...], b2_ref[...])
    y_ref[...] = jnp.where(valid, y, 0.0)


def _layer_tail(mixed, x, w_out, mix_layer, ln1, w_up, w_down, layer, conv_w, conv_b, ln2, fix,
                seq_len, n_pad, tm, full_gate):
    rows, d = x.shape
    d_ff = w_down.shape[1]
    nt = rows // tm
    vec = lambda a: a.reshape(1, -1)
    row_block = lambda width: pl.BlockSpec((tm, width), lambda i: (i, 0))
    in_specs = ([row_block(o.shape[1]) for o in mixed]
                + [row_block(d), _resident((w_out.shape[1], d), mix_layer), _resident((1, d)), _resident((1, d)),
                   _resident((d, 2 * d_ff), layer), _resident((d_ff, d), layer),
                   _resident((FFN_CONV, d_ff)), _resident((1, d_ff)), _resident((1, d)), _resident((1, d))])
    args = list(mixed) + [x, w_out, vec(ln1[0]), vec(ln1[1]), w_up, w_down, conv_w, vec(conv_b),
                          vec(ln2[0]), vec(ln2[1])]
    if fix is not None:
        in_specs.append(row_block(d_ff))
        args.append(fix)
    if full_gate:
        gate_spec = row_block(d_ff)
        gate_shape = jax.ShapeDtypeStruct((rows, d_ff), F32)
    else:
        gate_spec = pl.BlockSpec((1, SUBLANE, d_ff), lambda i: (i, 0, 0))
        gate_shape = jax.ShapeDtypeStruct((nt, SUBLANE, d_ff), F32)
    return pl.pallas_call(
        functools.partial(_tail_kernel, n_mixed=len(mixed), seq_len=seq_len, n_pad=n_pad, d_ff=d_ff,
                          has_fix=fix is not None, full_gate=full_gate),
        grid=(nt,),
        in_specs=in_specs,
        out_specs=[row_block(d), gate_spec],
        out_shape=[jax.ShapeDtypeStruct((rows, d), F32), gate_shape],
        scratch_shapes=[pltpu.VMEM((SUBLANE, d_ff), F32)],
        compiler_params=_params(1),
        name="layer_tail",
    )(*args)


def _group(batch, preferred):
    g = preferred
    while batch % g:
        g //= 2
    return g


def _state_spec(layer, group, dims):
    return pl.BlockSpec((None, group) + dims, lambda bg, c: (layer, bg) + (0,) * len(dims))


def _new_state_spec(s_all, s_buf, layer, group, dims):
    if s_buf is not None:
        return _state_spec(layer, group, dims)
    assert layer == 0
    return pl.BlockSpec((s_all.shape[0], group) + dims, lambda bg, c: (0, bg) + (0,) * len(dims))


def _store_state(s_ref, g, h, value):
    if len(s_ref.shape) == 5:
        s_ref[0, g, h] = value
    else:
        s_ref[g, h] = value


def _clear_later_layers(s_ref):
    if len(s_ref.shape) == 5:
        s_ref[1:] = jnp.zeros((s_ref.shape[0] - 1,) + tuple(s_ref.shape[1:]), F32)


def _alias_io(s_buf, n_inputs, out_index):
    if s_buf is None:
        return [], [], {}
    return [pl.BlockSpec(memory_space=pl.ANY)], [s_buf], {n_inputs: out_index}


def _stacked(ref):
    g, rows, cols = ref.shape
    return ref[...].reshape(g * rows, cols)


def _seq_masks(rows, seq_rows):
    shift = seq_rows.bit_length() - 1
    r, c = _iota2((rows, rows), 0), _iota2((rows, rows), 1)
    same = (r >> shift) == (c >> shift)
    return same, same & (c <= r)


def _gla_kernel(qk_ref, v_ref, og_ref, glo_ref, w2_ref, gb_ref, ng_ref, s0_ref, *rest, n_pad, sub):
    o_ref, s_ref, st_ref = rest[-3:]
    c = pl.program_id(1)
    group, chunk = qk_ref.shape[0], qk_ref.shape[1]
    rows = group * chunk
    nsub = chunk // sub
    chains = [(g, h) for g in range(group) for h in range(GLA_HEADS)]
    ks = [slice(h * GLA_HK, (h + 1) * GLA_HK) for h in range(GLA_HEADS)]
    vs = [slice(h * GLA_HV, (h + 1) * GLA_HV) for h in range(GLA_HEADS)]
    rs = [slice(g * chunk, (g + 1) * chunk) for g in range(group)]

    @pl.when(c == 0)
    def _():
        for i, (g, h) in enumerate(chains):
            st_ref[i] = s0_ref[g, h].T

    valid = (c * chunk + _seq_rows((rows, 1), chunk)) >= n_pad
    pre = _mm3(_stacked(glo_ref), w2_ref[...]) + gb_ref[...]
    log_a = -_softplus(-pre) * (1.0 / GLA_GATE_NORM)
    log_a = jnp.where(valid, log_a, 0.0)
    qk = _stacked(qk_ref)
    q = qk[:, :GLA_DK] * (GLA_HK ** -0.5)
    k = jnp.where(valid, qk[:, GLA_DK:], 0.0)
    same, causal = _seq_masks(rows, sub)
    b = _mm_const_lhs(causal.astype(F32), log_a, 3)
    b_tot = _mm_const_lhs(same.astype(F32), log_a, 3)
    qd = (q * jnp.exp(b)).astype(BF16)
    kd = (k * jnp.exp(-b)).astype(BF16)
    kend = (k * jnp.exp(b_tot - b)).astype(BF16)
    vb = _stacked(v_ref).astype(BF16)
    causal_c = _seq_masks(chunk, sub)[1]
    att = [jnp.where(causal_c, _dg(qd[rs[g], ks[h]], kd[rs[g], ks[h]], 'nt'), 0.0).astype(BF16) for g, h in chains]
    o = [_dg(att[i], vb[rs[g], vs[h]]) for i, (g, h) in enumerate(chains)]
    s = [st_ref[i] for i in range(len(chains))]
    o_inter = [[] for _ in chains]
    for j in range(nsub):
        for i, (g, h) in enumerate(chains):
            sl = slice(g * chunk + j * sub, g * chunk + (j + 1) * sub)
            o_inter[i].append(_dg(qd[sl, ks[h]], s[i].astype(BF16), 'nt'))
        nxt = []
        for i, (g, h) in enumerate(chains):
            sl = slice(g * chunk + j * sub, g * chunk + (j + 1) * sub)
            dec = jnp.exp(b_tot[g * chunk + j * sub:g * chunk + j * sub + 1, ks[h]])
            nxt.append(s[i] * dec + _dg(vb[sl, vs[h]], kend[sl, ks[h]], 'tn'))
        s = nxt
    og = _stacked(og_ref)
    for i, (g, h) in enumerate(chains):
        st_ref[i] = s[i]
        oh = o[i] + (jnp.concatenate(o_inter[i], axis=0) if nsub > 1 else o_inter[i][0])
        oh = oh * lax.rsqrt(jnp.mean(oh * oh, axis=-1, keepdims=True) + NORM_EPS) * ng_ref[:, vs[h]]
        o_ref[g, :, vs[h]] = oh * _silu(og[rs[g], vs[h]])

    @pl.when(c == pl.num_programs(1) - 1)
    def _():
        for i, (g, h) in enumerate(chains):
            _store_state(s_ref, g, h, st_ref[i].T)
        _clear_later_layers(s_ref)


def _gla(z, w2, gate_b, norm_g, s_all, layer, s_buf, n_pad, chunk, group):
    batch, seq_len, _ = z.shape
    nc = seq_len // chunk
    sub = math.gcd(chunk, GLA_SUBCHUNK)
    row = lambda blk: (lambda bg, c: (bg, c, blk))
    const = lambda bg, c: (0, 0)
    st = _state_spec(layer, group, (GLA_HEADS, GLA_HK, GLA_HV))
    w2p = jnp.zeros((LANE, GLA_DK), F32).at[:GLA_GATE_RANK].set(w2)
    alias_specs, alias_args, aliases = _alias_io(s_buf, 8, 1)
    return pl.pallas_call(
        functools.partial(_gla_kernel, n_pad=n_pad, sub=sub),
        grid=(batch // group, nc),
        in_specs=[pl.BlockSpec((group, chunk, 2 * GLA_DK), row(0)),
                  pl.BlockSpec((group, chunk, GLA_DV), row(1)),
                  pl.BlockSpec((group, chunk, GLA_DV), row(2)),
                  pl.BlockSpec((group, chunk, LANE), row(3 * 512 // LANE)),
                  pl.BlockSpec((LANE, GLA_DK), const),
                  pl.BlockSpec((1, GLA_DK), const),
                  pl.BlockSpec((1, GLA_DV), const),
                  st] + alias_specs,
        out_specs=[pl.BlockSpec((group, chunk, GLA_DV), row(0)),
                   _new_state_spec(s_all, s_buf, layer, group, (GLA_HEADS, GLA_HK, GLA_HV))],
        out_shape=[jax.ShapeDtypeStruct((batch, seq_len, GLA_DV), F32),
                   jax.ShapeDtypeStruct(s_all.shape, F32)],
        scratch_shapes=[pltpu.VMEM((group * GLA_HEADS, GLA_HV, GLA_HK), F32)],
        input_output_aliases=aliases,
        compiler_params=_params(2),
        name="gla",
    )(z, z, z, z, w2p, gate_b.reshape(1, GLA_DK), norm_g.reshape(1, GLA_DV), s_all, *alias_args)


def _rwkv_kernel(z_ref, mu_ref, w0_ref, w2_ref, a0_ref, a2_ref, g2_ref, kk_ref, ka_ref, rk_ref, lg_ref, lb_ref,
                 s0_ref, *rest, n_pad):
    y_ref, s_ref, st_ref, carry_ref = rest[-4:]
    c = pl.program_id(1)
    group, chunk = z_ref.shape[0], z_ref.shape[1]
    dim = RWKV_DIM
    pairs = RWKV_HEADS // 2

    @pl.when(c == 0)
    def _():
        zero = jnp.zeros((RWKV_HEAD, RWKV_HEAD), F32)
        for g in range(group):
            for q in range(pairs):
                top = jnp.concatenate([s0_ref[g, 2 * q], zero], axis=1)
                bottom = jnp.concatenate([zero, s0_ref[g, 2 * q + 1]], axis=1)
                st_ref[g * pairs + q] = jnp.concatenate([top, bottom], axis=0)
        carry_ref[...] = jnp.zeros_like(carry_ref)

    def prep(g0, g1):
        rows = (g1 - g0) * chunk
        valid = (c * chunk + _seq_rows((rows, 1), chunk)) >= n_pad
        z = z_ref[g0:g1].reshape(rows, z_ref.shape[2])
        z_prev = _shift_rows_seqs(z, [carry_ref[g] for g in range(g0, g1)], 1)
        for g in range(g0, g1):
            carry_ref[g] = z[(g - g0 + 1) * chunk - SUBLANE:(g - g0 + 1) * chunk]
        z = z + mu_ref[...] * (z_prev - z)
        r, kr, vr = z[:, :dim], z[:, dim:2 * dim], z[:, 2 * dim:3 * dim]
        w_lo = z[:, 3 * dim:3 * dim + 64]
        a_lo = z[:, 3 * dim + 64:3 * dim + 128]
        g_lo = z[:, 3 * dim + 128:]
        w_raw = w0_ref[...] + _mm3(jnp.tanh(w_lo), w2_ref[...])
        log_w = -jnp.exp(-_softplus(-w_raw) - 0.5)
        a_lr = _sigmoid(a0_ref[...] + _mm3(a_lo, a2_ref[...]))
        gate = _mm3(_sigmoid(g_lo), g2_ref[...])
        kx = kr * kk_ref[...]
        kk = kx * lax.rsqrt(_head_sums(kx * kx, RWKV_HEAD, 2) + NORM_EPS)
        k_mod = kr * (1.0 + (a_lr - 1.0) * ka_ref[...])
        bonus = _head_sums(r * k_mod * rk_ref[...], RWKV_HEAD, 2) * vr
        log_w = jnp.where(valid, log_w, 0.0)
        k_in = jnp.where(valid, k_mod, 0.0)
        b_in = jnp.where(valid, kk * a_lr, 0.0)
        same, causal = _seq_masks(rows, chunk)
        c_incl = _mm_const_lhs(causal.astype(F32), log_w, 3)
        c_tot = _mm_const_lhs(same.astype(F32), log_w, 3)
        e_neg = jnp.exp(-c_incl)
        e_end = jnp.exp(c_tot - c_incl)
        return dict(a_t=-kk * jnp.exp(c_incl - log_w), r_t=r * jnp.exp(c_incl),
                    bt=b_in * e_neg, kt=k_in * e_neg,
                    b_e=b_in * e_end, k_e=k_in * e_end, vr=vr, c_tot=c_tot,
                    bonus=bonus, gate=gate)

    def chains(p, g0, g1):
        ch = [(g, q) for g in range(g1 - g0) for q in range(pairs)]
        rs = [slice(g * chunk, (g + 1) * chunk) for g in range(g1 - g0)]
        pb = [slice(q * LANE, (q + 1) * LANE) for q in range(pairs)]
        n = len(ch)
        slot = [(g0 + g) * pairs + q for g, q in ch]
        col = _iota2((chunk, 2 * chunk), 1) & (chunk - 1)
        row = _iota2((chunk, 2 * chunk), 0)
        strict2, incl2 = col < row, col <= row
        blk = lambda x, g, q: _pair_blocks(x[rs[g], pb[q]], RWKV_HEAD).astype(BF16)
        s = [st_ref[i] for i in slot]
        ar = [jnp.concatenate([p['a_t'][rs[g], pb[q]], p['r_t'][rs[g], pb[q]]], axis=0).astype(BF16) for g, q in ch]
        vbd = [blk(p['vr'], g, q) for g, q in ch]
        g_b = [_dg(ar[i], blk(p['bt'], g, q), 'nt') for i, (g, q) in enumerate(ch)]
        g_k = [_dg(ar[i], blk(p['kt'], g, q), 'nt') for i, (g, q) in enumerate(ch)]
        g_s = [_dg(ar[i], s[i].astype(BF16), 'nt') for i in range(n)]
        m_ab = [jnp.where(strict2, g_b[i][:chunk], 0.0) for i in range(n)]
        rhs = [g_s[i][:chunk] + _mm(jnp.where(strict2, g_k[i][:chunk], 0.0), vbd[i]) for i in range(n)]
        t_inv = _unit_lower_inverse_pairs(m_ab)
        u = [_mm(t_inv[i], _pair_blocks(rhs[i], RWKV_HEAD)) for i in range(n)]
        ys = [g_s[i][chunk:] + _mm(jnp.where(incl2, g_b[i][chunk:], 0.0), _pair_blocks(u[i], RWKV_HEAD))
              + _mm(jnp.where(incl2, g_k[i][chunk:], 0.0), vbd[i]) for i in range(n)]
        same_head = (_iota2((LANE, LANE), 0) < RWKV_HEAD) == (_iota2((LANE, LANE), 1) < RWKV_HEAD)
        for i, (g, q) in enumerate(ch):
            uv = jnp.concatenate([u[i], p['vr'][rs[g], pb[q]]], axis=0)
            bk = jnp.concatenate([p['b_e'][rs[g], pb[q]], p['k_e'][rs[g], pb[q]]], axis=0)
            w_end = jnp.exp(p['c_tot'][g * chunk:g * chunk + 1, pb[q]])
            st_ref[slot[i]] = s[i] * w_end + jnp.where(same_head, _mm(uv, bk, 'tn'), 0.0)
        per_seq = [jnp.concatenate(ys[g * pairs:(g + 1) * pairs], axis=1) for g in range(g1 - g0)]
        return jnp.concatenate(per_seq, axis=0) if g1 - g0 > 1 else per_seq[0]

    def finish(p, y, g0, g1):
        mean = _head_sums(y, RWKV_HEAD, 2) * (1.0 / RWKV_HEAD)
        d = y - mean
        var = _head_sums(d * d, RWKV_HEAD, 2) * (1.0 / RWKV_HEAD)
        y = d * lax.rsqrt(var + RWKV_GN_EPS) * lg_ref[...] + lb_ref[...]
        y_ref[g0:g1] = ((y + p['bonus']) * p['gate']).reshape(g1 - g0, chunk, dim)

    p = prep(0, group)
    finish(p, chains(p, 0, group), 0, group)

    @pl.when(c == pl.num_programs(1) - 1)
    def _():
        for g in range(group):
            for q in range(pairs):
                _store_state(s_ref, g, 2 * q, st_ref[g * pairs + q, :RWKV_HEAD, :RWKV_HEAD])
                _store_state(s_ref, g, 2 * q + 1, st_ref[g * pairs + q, RWKV_HEAD:, RWKV_HEAD:])
        _clear_later_layers(s_ref)


def _rwkv(z, p, s_all, layer, s_buf, n_pad, chunk, group):
    batch, seq_len, _ = z.shape
    nc = seq_len // chunk
    dim = RWKV_DIM
    row = lambda bg, c: (bg, c, 0)
    const = lambda bg, c: (0, 0)
    st = _state_spec(layer, group, (RWKV_HEADS, RWKV_HEAD, RWKV_HEAD))
    vec = lambda a: a.reshape(1, -1)
    small = [vec(p['mu']), vec(p['w0']), p['w2'], vec(p['a0']), p['a2'], p['g2'], vec(p['k_k']), vec(p['k_a']),
             vec(p['r_k']), vec(p['ln_g']), vec(p['ln_b'])]
    alias_specs, alias_args, aliases = _alias_io(s_buf, len(small) + 2, 1)
    return pl.pallas_call(
        functools.partial(_rwkv_kernel, n_pad=n_pad),
        grid=(batch // group, nc),
        in_specs=([pl.BlockSpec((group, chunk, RWKV_COLS), row)]
                  + [pl.BlockSpec(a.shape, const) for a in small]
                  + [st] + alias_specs),
        out_specs=[pl.BlockSpec((group, chunk, dim), row),
                   _new_state_spec(s_all, s_buf, layer, group, (RWKV_HEADS, RWKV_HEAD, RWKV_HEAD))],
        out_shape=[jax.ShapeDtypeStruct((batch, seq_len, dim), F32),
                   jax.ShapeDtypeStruct(s_all.shape, F32)],
        scratch_shapes=[pltpu.VMEM((group * RWKV_HEADS // 2, LANE, LANE), F32),
                        pltpu.VMEM((group, SUBLANE, RWKV_COLS), F32)],
        input_output_aliases=aliases,
        compiler_params=_params(2),
        name="rwkv7",
    )(z, *small, s_all, *alias_args)


def _gdn_kernel(qkv_ref, rest_ref, alog_ref, dt_ref, ng_ref, s0_ref, *rest, n_pad):
    o_ref, s_ref, st_ref = rest[-3:]
    c = pl.program_id(1)
    group, chunk = qkv_ref.shape[0], qkv_ref.shape[1]
    rows = group * chunk
    dim = GDN_DIM
    chains = [(g, h) for g in range(group) for h in range(GDN_HEADS)]
    hs = [slice(h * GDN_HEAD, (h + 1) * GDN_HEAD) for h in range(GDN_HEADS)]
    rs = [slice(g * chunk, (g + 1) * chunk) for g in range(group)]

    @pl.when(c == 0)
    def _():
        for i, (g, h) in enumerate(chains):
            st_ref[i] = s0_ref[g, h]

    valid = (c * chunk + _seq_rows((rows, 1), chunk)) >= n_pad
    qkv = _stacked(qkv_ref)
    rest = _stacked(rest_ref)
    zg = rest[:, :dim]
    lo = rest[:, dim:]
    beta_all = _sigmoid(lo)
    g_all = -jnp.exp(alog_ref[...]) * _softplus(lo + dt_ref[...])
    g_all = jnp.where(valid, g_all, 0.0)

    incl = _tri(chunk)
    strict = _tri(chunk, strict=True)
    causal = _seq_masks(rows, chunk)[1]
    gam_all = _mm_const_lhs(causal.astype(F32), g_all, 3)
    shift = chunk.bit_length() - 1
    pick = (_iota2((GDN_HEADS * chunk, LANE), 1)
            == GDN_HEADS + (_iota2((GDN_HEADS * chunk, LANE), 0) >> shift)).astype(F32)
    gam_cols = [_mm_const_lhs(pick, gam_all[rs[g]], 3, 'nt') for g in range(group)]
    q_sq = _head_sums(jnp.square(qkv[:, :dim]), GDN_HEAD, 2)
    k_sq = _head_sums(jnp.square(qkv[:, dim:2 * dim]), GDN_HEAD, 2)
    q_all = qkv[:, :dim] * lax.rsqrt(q_sq + NORM_EPS) * (GDN_HEAD ** -0.5)
    k_all = jnp.where(valid, qkv[:, dim:2 * dim] * lax.rsqrt(k_sq + NORM_EPS), 0.0)
    v_all = qkv[:, 2 * dim:]
    n = len(chains)
    gam = [gam_all[rs[g], GDN_HEADS + h:GDN_HEADS + h + 1] for g, h in chains]
    beta = [beta_all[rs[g], h:h + 1] for g, h in chains]
    kq = [jnp.concatenate([k_all[rs[g], hs[h]], q_all[rs[g], hs[h]]], axis=0).astype(BF16) for g, h in chains]
    dec = [jnp.where(incl, jnp.exp(jnp.where(incl, gam[i] - gam_cols[g][h * chunk:(h + 1) * chunk], 0.0)), 0.0)
           for i, (g, h) in enumerate(chains)]
    gram = [_dg(kq[i], k_all[rs[g], hs[h]].astype(BF16), 'nt') for i, (g, h) in enumerate(chains)]
    m = [jnp.where(strict, gram[i][:chunk] * dec[i], 0.0) * (-beta[i]) for i in range(n)]
    t_inv = _unit_lower_inverses(m)
    e_gam = [jnp.exp(gam[i]) for i in range(n)]
    rhs = [jnp.concatenate([v_all[rs[g], hs[h]] * beta[i], k_all[rs[g], hs[h]] * (beta[i] * e_gam[i])], axis=1)
           for i, (g, h) in enumerate(chains)]
    uw = [_mm(t_inv[i], rhs[i]) for i in range(n)]
    s = [st_ref[i] for i in range(n)]
    sb = [s[i].astype(BF16) for i in range(n)]
    delta = [uw[i][:, :GDN_HEAD] - _mm(uw[i][:, GDN_HEAD:], sb[i]) for i in range(n)]
    o = [_mm(q_all[rs[g], hs[h]] * e_gam[i], sb[i]) + _mm(gram[i][chunk:] * dec[i], delta[i])
         for i, (g, h) in enumerate(chains)]
    for i, (g, h) in enumerate(chains):
        g_end = gam[i][chunk - 1:chunk, :]
        st_ref[i] = s[i] * jnp.exp(g_end) + _mm(k_all[rs[g], hs[h]] * jnp.exp(g_end - gam[i]), delta[i], 'tn')
        oh = o[i] * lax.rsqrt(jnp.mean(o[i] * o[i], axis=-1, keepdims=True) + NORM_EPS) * ng_ref[...]
        o_ref[g, :, hs[h]] = oh * _silu(zg[rs[g], hs[h]])

    @pl.when(c == pl.num_programs(1) - 1)
    def _():
        for i, (g, h) in enumerate(chains):
            _store_state(s_ref, g, h, st_ref[i])
        _clear_later_layers(s_ref)


def _gdn(qkv, rest, a_log, dt_bias, norm_g, s_all, layer, s_buf, n_pad, chunk, group):
    batch, seq_len, _ = qkv.shape
    nc = seq_len // chunk
    dim = GDN_DIM
    row = lambda bg, c: (bg, c, 0)
    const = lambda bg, c: (0, 0)
    st = _state_spec(layer, group, (GDN_HEADS, GDN_HEAD, GDN_HEAD))
    pad_lo = lambda a: jnp.zeros((1, LANE), F32).at[0, GDN_HEADS:2 * GDN_HEADS].set(a)
    alias_specs, alias_args, aliases = _alias_io(s_buf, 6, 1)
    return pl.pallas_call(
        functools.partial(_gdn_kernel, n_pad=n_pad),
        grid=(batch // group, nc),
        in_specs=[pl.BlockSpec((group, chunk, 3 * dim), row),
                  pl.BlockSpec((group, chunk, dim + LANE), row),
                  pl.BlockSpec((1, LANE), const),
                  pl.BlockSpec((1, LANE), const),
                  pl.BlockSpec((1, GDN_HEAD), const),
                  st] + alias_specs,
        out_specs=[pl.BlockSpec((group, chunk, dim), row),
                   _new_state_spec(s_all, s_buf, layer, group, (GDN_HEADS, GDN_HEAD, GDN_HEAD))],
        out_shape=[jax.ShapeDtypeStruct((batch, seq_len, dim), F32),
                   jax.ShapeDtypeStruct(s_all.shape, F32)],
        scratch_shapes=[pltpu.VMEM((group * GDN_HEADS, GDN_HEAD, GDN_HEAD), F32)],
        input_output_aliases=aliases,
        compiler_params=_params(2),
        name="gated_deltanet",
    )(qkv, rest, pad_lo(a_log), pad_lo(dt_bias), norm_g.reshape(1, GDN_HEAD), s_all, *alias_args)


def _pad_rows_fix(buf, seq_len, n_pad):
    b, w, c = buf.shape
    return jnp.zeros((b, seq_len, c), F32).at[:, n_pad - w:n_pad].set(buf).reshape(b * seq_len, c)


def _trunk(x, states, wts, batch, seq_len, n_pad, chunk, group, carried):
    st_gla, st_rwkv, st_shift, st_gdn, st_gdn_conv, st_ffn_conv = states
    d = x.shape[1]
    rows = batch * seq_len
    tm = _seq_tile(seq_len, rows, carried)
    tile_tail = lambda a, w: a.reshape(batch, seq_len // tm, SUBLANE, -1)[:, -1, SUBLANE - w:]
    seq_tail = lambda a, w: a.reshape(batch, seq_len, -1)[:, seq_len - w:]
    conv_rows = seq_tail if carried else tile_tail
    seqs = lambda a: a.reshape(batch, seq_len, a.shape[-1])
    flat = lambda a: a.reshape(rows, a.shape[-1])
    new_gla = new_rwkv = new_gdn = None
    new = {k: [] for k in ('shift', 'gdn_conv', 'ffn_conv')}
    for l in range(DEPTH):
        i = l // 2
        if l % 2 == 0:
            new['shift'].append(seqs(x)[:, -1])
            x_in = x
            if carried:
                x_in = flat(seqs(x).at[:, n_pad - 1].set(st_shift[i]))
            z_gla, z_rwkv = _inproj(x_in, wts['w_in_ab'], i, (GLA_Z, RWKV_COLS))
            o_gla, new_gla = _gla(seqs(z_gla), wts['gla_gate_w2'][i], wts['gla_gate_b'][i], wts['gla_norm_g'][i],
                                  st_gla, i, new_gla, n_pad, chunk, group)
            y_rwkv, new_rwkv = _rwkv(seqs(z_rwkv), {k: wts['rwkv_' + k][i] for k in
                                                    ('mu', 'w0', 'w2', 'a0', 'a2', 'g2', 'k_k', 'k_a', 'r_k', 'ln_g',
                                                     'ln_b')},
                                     st_rwkv, i, new_rwkv, n_pad, chunk, group)
            mixed = (flat(o_gla), flat(y_rwkv))
            w_out = wts['w_out_ab']
        else:
            fix = _pad_rows_fix(st_gdn_conv[i], seq_len, n_pad) if carried else None
            qkv, rest, pre = _inproj_gdn(x, wts['w_in_c'], i, wts['gdn_conv_w'][i], fix, tm, carried)
            new['gdn_conv'].append(conv_rows(pre, GDN_CONV - 1))
            o, new_gdn = _gdn(seqs(qkv), seqs(rest), wts['gdn_A_log'][i], wts['gdn_dt_bias'][i],
                              wts['gdn_norm_g'][i], st_gdn, i, new_gdn, n_pad, chunk, group)
            mixed = (flat(o),)
            w_out = wts['w_out_c']
        fix = _pad_rows_fix(st_ffn_conv[l], seq_len, n_pad) if carried else None
        x, gate = _layer_tail(mixed, x, w_out, i, (wts['ln_mix_g'][l], wts['ln_mix_b'][l]),
                              wts['w_up'], wts['w_down'], l, wts['ffn_conv_w'][l], wts['ffn_conv_b'][l],
                              (wts['ln_ffn_g'][l], wts['ln_ffn_b'][l]), fix, seq_len, n_pad, tm, carried)
        new['ffn_conv'].append(conv_rows(gate, FFN_CONV - 1))
    return (x, new_gla, new_rwkv, jnp.stack(new['shift']), new_gdn, jnp.stack(new['gdn_conv']),
            jnp.stack(new['ffn_conv']))


def kernel(x_prompt, x_sample, state_gla, state_rwkv, state_rwkv_shift, state_gdn, state_gdn_conv, state_ffn_conv, meta_tokens, w_in_ab, gla_gate_w2, gla_gate_b, gla_norm_g, rwkv_mu, rwkv_w0, rwkv_w2, rwkv_a0, rwkv_a2, rwkv_g2, rwkv_k_k, rwkv_k_a, rwkv_r_k, rwkv_ln_g, rwkv_ln_b, w_out_ab, w_in_c, gdn_conv_w, gdn_A_log, gdn_dt_bias, gdn_norm_g, w_out_c, w_up, ffn_conv_w, ffn_conv_b, w_down, ln_mix_g, ln_mix_b, ln_ffn_g, ln_ffn_b):
    d = x_prompt.shape[-1]
    n_ab, n_c = w_in_ab.shape[0], w_in_c.shape[0]
    gla_cols = 2 * GLA_DK + 2 * GLA_DV + GLA_GATE_RANK
    lo0 = 2 * GLA_DK + GLA_DV
    w_ab = jnp.concatenate([w_in_ab[:, :, :lo0], w_in_ab[:, :, lo0 + GLA_GATE_RANK:gla_cols],
                            w_in_ab[:, :, lo0:lo0 + GLA_GATE_RANK],
                            jnp.zeros((n_ab, d, LANE - GLA_GATE_RANK), w_in_ab.dtype),
                            w_in_ab[:, :, gla_cols:]], axis=2).astype(BF16)
    w_c = jnp.concatenate([w_in_c, jnp.zeros((n_c, d, LANE - 2 * GDN_HEADS), w_in_c.dtype)], axis=2).astype(BF16)
    wts = {
        'w_in_ab': w_ab, 'gla_gate_w2': gla_gate_w2, 'gla_gate_b': gla_gate_b, 'gla_norm_g': gla_norm_g,
        'rwkv_mu': rwkv_mu, 'rwkv_w0': rwkv_w0, 'rwkv_w2': rwkv_w2, 'rwkv_a0': rwkv_a0, 'rwkv_a2': rwkv_a2,
        'rwkv_g2': rwkv_g2, 'rwkv_k_k': rwkv_k_k, 'rwkv_k_a': rwkv_k_a, 'rwkv_r_k': rwkv_r_k,
        'rwkv_ln_g': rwkv_ln_g, 'rwkv_ln_b': rwkv_ln_b, 'w_out_ab': w_out_ab.astype(BF16),
        'w_in_c': w_c, 'gdn_conv_w': gdn_conv_w, 'gdn_A_log': gdn_A_log, 'gdn_dt_bias': gdn_dt_bias,
        'gdn_norm_g': gdn_norm_g, 'w_out_c': w_out_c.astype(BF16),
        'w_up': w_up.astype(BF16), 'ffn_conv_w': ffn_conv_w, 'ffn_conv_b': ffn_conv_b,
        'w_down': w_down.astype(BF16),
        'ln_mix_g': ln_mix_g, 'ln_mix_b': ln_mix_b, 'ln_ffn_g': ln_ffn_g, 'ln_ffn_b': ln_ffn_b,
    }

    bp, seq, _ = x_prompt.shape
    tp = -(-(N_META + seq + SUBLANE) // PROMPT_CHUNK) * PROMPT_CHUNK
    pad_p = tp - N_META - seq
    meta = jnp.broadcast_to(meta_tokens.astype(F32)[None], (bp, N_META, d))
    xp = jnp.concatenate([jnp.zeros((bp, pad_p, d), F32), meta, x_prompt], axis=1).reshape(bp * tp, d)
    zero_state = lambda s: jnp.zeros((s.shape[0], bp) + s.shape[2:], F32)
    p_out = _trunk(xp, tuple(zero_state(s) for s in (state_gla, state_rwkv, state_rwkv_shift, state_gdn,
                                                      state_gdn_conv, state_ffn_conv)),
                   wts, bp, tp, pad_p, PROMPT_CHUNK, _group(bp, PROMPT_GROUP), carried=False)
    y_prompt = p_out[0].reshape(bp, tp, d)[:, pad_p + N_META:]

    bs, ts, _ = x_sample.shape
    tsp = -(-(ts + GDN_CONV - 1) // SUBLANE) * SUBLANE
    pad_s = tsp - ts
    xs = jnp.concatenate([jnp.zeros((bs, pad_s, d), F32), x_sample], axis=1).reshape(bs * tsp, d)
    s_out = _trunk(xs, (state_gla, state_rwkv, state_rwkv_shift, state_gdn, state_gdn_conv, state_ffn_conv),
                   wts, bs, tsp, pad_s, tsp, _group(bs, SAMPLE_GROUP), carried=True)
    y_sample = s_out[0].reshape(bs, tsp, d)[:, pad_s:]
    return (y_prompt, y_sample) + tuple(p_out[1:]) + tuple(s_out[1:])
```

```python
import functools
import math

import jax
import jax.numpy as jnp
from jax import lax
from jax.experimental import pallas as pl
from jax.experimental.pallas import tpu as pltpu

F32 = jnp.float32
BF16 = jnp.bfloat16

N_META = 16
GLA_HEADS, GLA_HK, GLA_HV = 4, 64, 128
GLA_DK, GLA_DV = GLA_HEADS * GLA_HK, GLA_HEADS * GLA_HV
GLA_GATE_RANK = 16
GLA_GATE_NORM = 16.0
GLA_SUBCHUNK = 16
RWKV_HEADS, RWKV_HEAD = 8, 64
RWKV_DIM = RWKV_HEADS * RWKV_HEAD
RWKV_COLS = 3 * RWKV_DIM + 64 + 64 + 128
RWKV_GN_EPS = 64e-5
GDN_HEADS, GDN_HEAD = 8, 128
GDN_DIM = GDN_HEADS * GDN_HEAD
GDN_CONV = 4
FFN_CONV = 3
LN_EPS = 1e-5
NORM_EPS = 1e-6
DEPTH = 4
DEEPNORM_ALPHA = (2.0 * DEPTH) ** 0.25

LANE = 128
SUBLANE = 8
PROMPT_CHUNK = 64
PROMPT_GROUP = 4
SAMPLE_GROUP = 8
VMEM_LIMIT = 56 * 1024 * 1024

GLA_Z = 3 * 512 + LANE


def _params(n_axes):
    return pltpu.CompilerParams(dimension_semantics=("arbitrary",) * n_axes,
                                vmem_limit_bytes=VMEM_LIMIT)


def _row_tile(rows, cap):
    t = cap
    while rows % t:
        t //= 2
    return t


def _seq_tile(seq_len, rows, carried):
    if carried:
        return _row_tile(rows, 256)
    return next(t for t in (704, 352, 192, 64) if seq_len % t == 0)


def _resident(shape, layer=None):
    if layer is None:
        return pl.BlockSpec(shape, lambda *_: (0,) * len(shape), pipeline_mode=pl.Buffered(1))
    return pl.BlockSpec((None,) + tuple(shape), lambda *_: (layer,) + (0,) * len(shape),
                        pipeline_mode=pl.Buffered(1))


_DIMS = {'nn': (((1,), (0,)), ((), ())),
         'nt': (((1,), (1,)), ((), ())),
         'tn': (((0,), (0,)), ((), ()))}


def _dg(a, b, kind='nn'):
    return lax.dot_general(a, b, _DIMS[kind], preferred_element_type=F32)


def _split(x, pieces):
    out = []
    for _ in range(pieces - 1):
        hi = x.astype(BF16)
        out.append(hi)
        x = x - hi.astype(F32)
    out.append(x.astype(BF16))
    return out


def _mm(a, b, kind='nn'):
    return _dg(a.astype(BF16), b.astype(BF16), kind)


def _mm3(a, b, kind='nn'):
    ah, al = _split(a, 2)
    bh, bl = _split(b, 2)
    return _dg(ah, bh, kind) + _dg(al, bh, kind) + _dg(ah, bl, kind)


def _mm_const_lhs(c, x, pieces, kind='nn'):
    cb = c.astype(BF16)
    return sum(_dg(cb, p, kind) for p in _split(x, pieces))


def _mm_const_rhs(x, c, pieces, kind='nn'):
    cb = c.astype(BF16)
    return sum(_dg(p, cb, kind) for p in _split(x, pieces))


def _bdot(a, w):
    return jnp.dot(a.astype(BF16), w, preferred_element_type=F32)


def _iota2(shape, dim):
    return lax.broadcasted_iota(jnp.int32, shape, dim)


def _tri(n, strict=False):
    r, c = _iota2((n, n), 0), _iota2((n, n), 1)
    return (c < r) if strict else (c <= r)


def _softplus(x):
    return jnp.maximum(x, 0.0) + jnp.log1p(jnp.exp(-jnp.abs(x)))


def _sigmoid(x):
    return 1.0 / (1.0 + jnp.exp(-x))


def _silu(x):
    return x * _sigmoid(x)


def _unit_lower_inverses(ms):
    n = ms[0].shape[0]
    eye = (_iota2((n, n), 0) == _iota2((n, n), 1)).astype(F32)
    xs = [eye + m for m in ms]
    ps = list(ms)
    for _ in range(int(math.log2(n)) - 1):
        ps = [_mm(p, p) for p in ps]
        xs = [x + _mm(x, p) for x, p in zip(xs, ps)]
    return xs


def _pair_blocks(x, split):
    first = _iota2(x.shape, 1) < split
    return jnp.concatenate([jnp.where(first, x, 0.0), jnp.where(first, 0.0, x)], axis=0)


def _unit_lower_inverse_pairs(ms):
    n = ms[0].shape[0]
    eye = ((_iota2((n, 2 * n), 1) & (n - 1)) == _iota2((n, 2 * n), 0)).astype(F32)
    xs = [eye + m for m in ms]
    ps = list(ms)
    bds = [_pair_blocks(p, n).astype(BF16) for p in ps]
    for _ in range(int(math.log2(n)) - 1):
        ps = [_mm(p, bd) for p, bd in zip(ps, bds)]
        bds = [_pair_blocks(p, n).astype(BF16) for p in ps]
        xs = [x + _mm(x, bd) for x, bd in zip(xs, bds)]
    return xs


def _head_sums(x, width, pieces):
    rows, cols = x.shape
    nb = cols // LANE
    shift = width.bit_length() - 1
    grp = (_iota2((LANE, LANE), 0) >> shift == _iota2((LANE, LANE), 1) >> shift).astype(F32)
    xs = jnp.concatenate([x[:, i * LANE:(i + 1) * LANE] for i in range(nb)], axis=0)
    s = _mm_const_rhs(xs, grp, pieces)
    return jnp.concatenate([s[i * rows:(i + 1) * rows] for i in range(nb)], axis=1)


def _shift_rows(cur, carry, k):
    rolled = pltpu.roll(cur, k, 0)
    head = jnp.where(_iota2((SUBLANE, 1), 0) < k, pltpu.roll(carry, k, 0), rolled[:SUBLANE])
    if cur.shape[0] == SUBLANE:
        return head
    return jnp.concatenate([head, rolled[SUBLANE:]], axis=0)


def _shift_rows_seqs(cur, carries, k):
    n = len(carries)
    rows = cur.shape[0] // n
    return jnp.concatenate([_shift_rows(cur[g * rows:(g + 1) * rows], carries[g], k) for g in range(n)], axis=0) \
        if n > 1 else _shift_rows(cur, carries[0], k)


def _seq_rows(shape, seq_rows):
    return _iota2(shape, 0) & (seq_rows - 1)


def _valid_rows(row0, rows, seq_len, n_pad):
    if seq_len & (seq_len - 1) == 0:
        t = (row0 + _iota2((rows, 1), 0)) & (seq_len - 1)
    else:
        t = lax.rem(row0, seq_len) + _iota2((rows, 1), 0)
        for _ in range(-(-rows // seq_len)):
            t = jnp.where(t >= seq_len, t - seq_len, t)
    return t >= n_pad


def _layer_norm_rows(h, g, b):
    mu = jnp.mean(h, axis=-1, keepdims=True)
    d = h - mu
    var = jnp.mean(d * d, axis=-1, keepdims=True)
    return d * lax.rsqrt(var + LN_EPS) * g + b


def _inproj_kernel(x_ref, w_ref, *out_refs, widths):
    z = _bdot(x_ref[...], w_ref[...])
    off = 0
    for o_ref, wd in zip(out_refs, widths):
        o_ref[...] = z[:, off:off + wd]
        off += wd


def _inproj(x, w, layer, widths):
    rows, d = x.shape
    n = w.shape[2]
    tm = _row_tile(rows, 512)
    return pl.pallas_call(
        functools.partial(_inproj_kernel, widths=widths),
        grid=(rows // tm,),
        in_specs=[pl.BlockSpec((tm, d), lambda i: (i, 0)), _resident((d, n), layer)],
        out_specs=[pl.BlockSpec((tm, wd), lambda i: (i, 0)) for wd in widths],
        out_shape=[jax.ShapeDtypeStruct((rows, wd), F32) for wd in widths],
        compiler_params=_params(1),
        name="inproj",
    )(x, w)


def _inproj_gdn_kernel(*refs, has_fix, full_pre):
    if has_fix:
        x_ref, w_ref, cw_ref, fix_ref, act_ref, rest_ref, pre_ref, carry_ref = refs
    else:
        x_ref, w_ref, cw_ref, act_ref, rest_ref, pre_ref, carry_ref = refs
    tm = x_ref.shape[0]
    width = act_ref.shape[1]

    @pl.when(pl.program_id(0) == 0)
    def _():
        carry_ref[...] = jnp.zeros_like(carry_ref)

    xb = x_ref[...].astype(BF16)
    step = 4 * LANE
    for lo in range(0, width, step):
        cs = slice(lo, lo + step)
        pre = jnp.dot(xb, w_ref[:, cs], preferred_element_type=F32)
        if has_fix:
            pre = pre + fix_ref[:, cs]
        carry = carry_ref[:, cs]
        conv = cw_ref[GDN_CONV - 1:GDN_CONV, cs] * pre
        for tap in range(GDN_CONV - 1):
            conv = conv + cw_ref[tap:tap + 1, cs] * _shift_rows(pre, carry, GDN_CONV - 1 - tap)
        carry_ref[:, cs] = pre[tm - SUBLANE:, :]
        act_ref[:, cs] = _silu(conv)
        if full_pre:
            pre_ref[:, cs] = pre
        else:
            pre_ref[0, :, cs] = pre[tm - SUBLANE:, :]
    rest_ref[...] = jnp.dot(xb, w_ref[:, width:], preferred_element_type=F32)


def _inproj_gdn(x, w, layer, conv_w, fix, tm, full_pre):
    rows, d = x.shape
    n = w.shape[2]
    width = conv_w.shape[1]
    nt = rows // tm
    in_specs = [pl.BlockSpec((tm, d), lambda i: (i, 0)), _resident((d, n), layer), _resident(conv_w.shape)]
    args = [x, w, conv_w]
    if fix is not None:
        in_specs.append(pl.BlockSpec((tm, width), lambda i: (i, 0)))
        args.append(fix)
    if full_pre:
        pre_spec = pl.BlockSpec((tm, width), lambda i: (i, 0))
        pre_shape = jax.ShapeDtypeStruct((rows, width), F32)
    else:
        pre_spec = pl.BlockSpec((1, SUBLANE, width), lambda i: (i, 0, 0))
        pre_shape = jax.ShapeDtypeStruct((nt, SUBLANE, width), F32)
    return pl.pallas_call(
        functools.partial(_inproj_gdn_kernel, has_fix=fix is not None, full_pre=full_pre),
        grid=(nt,),
        in_specs=in_specs,
        out_specs=[pl.BlockSpec((tm, width), lambda i: (i, 0)),
                   pl.BlockSpec((tm, n - width), lambda i: (i, 0)),
                   pre_spec],
        out_shape=[jax.ShapeDtypeStruct((rows, width), F32),
                   jax.ShapeDtypeStruct((rows, n - width), F32),
                   pre_shape],
        scratch_shapes=[pltpu.VMEM((SUBLANE, width), F32)],
        compiler_params=_params(1),
        name="inproj_gdn",
    )(*args)


def _tail_kernel(*refs, n_mixed, seq_len, n_pad, d_ff, has_fix, full_gate):
    o_refs, refs = refs[:n_mixed], refs[n_mixed:]
    if has_fix:
        (x_ref, wout_ref, g1_ref, b1_ref, wup_ref, wdn_ref, cw_ref, cb_ref, g2_ref, b2_ref, fix_ref,
         y_ref, gate_ref, carry_ref) = refs
    else:
        (x_ref, wout_ref, g1_ref, b1_ref, wup_ref, wdn_ref, cw_ref, cb_ref, g2_ref, b2_ref,
         y_ref, gate_ref, carry_ref) = refs
    i = pl.program_id(0)
    tm = x_ref.shape[0]

    @pl.when(i == 0)
    def _():
        carry_ref[...] = jnp.zeros_like(carry_ref)

    valid = _valid_rows(i * tm, tm, seq_len, n_pad)
    h = DEEPNORM_ALPHA * x_ref[...]
    off = 0
    for o_ref in o_refs:
        h = h + _bdot(o_ref[...], wout_ref[off:off + o_ref.shape[1], :])
        off += o_ref.shape[1]
    x = jnp.where(valid, _layer_norm_rows(h, g1_ref[...], b1_ref[...]), 0.0)

    hu = _bdot(x, wup_ref[...])
    gate, up = hu[:, :d_ff], hu[:, d_ff:]
    if has_fix:
        gate = gate + fix_ref[...]
    carry = carry_ref[...]
    conv = (cw_ref[0:1, :] * _shift_rows(gate, carry, 2) + cw_ref[1:2, :] * _shift_rows(gate, carry, 1)
            + cw_ref[2:3, :] * gate + cb_ref[...])
    acc = _bdot(_silu(conv) * up, wdn_ref[...])
    carry_ref[...] = gate[tm - SUBLANE:, :]
    if full_gate:
        gate_ref[...] = gate
    else:
        gate_ref[0] = gate[tm - SUBLANE:, :]
    y = _layer_norm_rows(DEEPNORM_ALPHA * x + acc, g2_ref[...], b2_ref[...])
    y_ref[...] = jnp.where(valid, y, 0.0)


def _layer_tail(mixed, x, w_out, mix_layer, ln1, w_up, w_down, layer, conv_w, conv_b, ln2, fix,
                seq_len, n_pad, tm, full_gate):
    rows, d = x.shape
    d_ff = w_down.shape[1]
    nt = rows // tm
    vec = lambda a: a.reshape(1, -1)
    row_block = lambda width: pl.BlockSpec((tm, width), lambda i: (i, 0))
    in_specs = ([row_block(o.shape[1]) for o in mixed]
                + [row_block(d), _resident((w_out.shape[1], d), mix_layer), _resident((1, d)), _resident((1, d)),
                   _resident((d, 2 * d_ff), layer), _resident((d_ff, d), layer),
                   _resident((FFN_CONV, d_ff)), _resident((1, d_ff)), _resident((1, d)), _resident((1, d))])
    args = list(mixed) + [x, w_out, vec(ln1[0]), vec(ln1[1]), w_up, w_down, conv_w, vec(conv_b),
                          vec(ln2[0]), vec(ln2[1])]
    if fix is not None:
        in_specs.append(row_block(d_ff))
        args.append(fix)
    if full_gate:
        gate_spec = row_block(d_ff)
        gate_shape = jax.ShapeDtypeStruct((rows, d_ff), F32)
    else:
        gate_spec = pl.BlockSpec((1, SUBLANE, d_ff), lambda i: (i, 0, 0))
        gate_shape = jax.ShapeDtypeStruct((nt, SUBLANE, d_ff), F32)
    return pl.pallas_call(
        functools.partial(_tail_kernel, n_mixed=len(mixed), seq_len=seq_len, n_pad=n_pad, d_ff=d_ff,
                          has_fix=fix is not None, full_gate=full_gate),
        grid=(nt,),
        in_specs=in_specs,
        out_specs=[row_block(d), gate_spec],
        out_shape=[jax.ShapeDtypeStruct((rows, d), F32), gate_shape],
        scratch_shapes=[pltpu.VMEM((SUBLANE, d_ff), F32)],
        compiler_params=_params(1),
        name="layer_tail",
    )(*args)


def _group(batch, preferred):
    g = preferred
    while batch % g:
        g //= 2
    return g


def _state_spec(layer, group, dims):
    return pl.BlockSpec((None, group) + dims, lambda bg, c: (layer, bg) + (0,) * len(dims))


def _new_state_spec(s_all, s_buf, layer, group, dims):
    if s_buf is not None:
        return _state_spec(layer, group, dims)
    assert layer == 0
    return pl.BlockSpec((s_all.shape[0], group) + dims, lambda bg, c: (0, bg) + (0,) * len(dims))


def _store_state(s_ref, g, h, value):
    if len(s_ref.shape) == 5:
        s_ref[0, g, h] = value
    else:
        s_ref[g, h] = value


def _clear_later_layers(s_ref):
    if len(s_ref.shape) == 5:
        s_ref[1:] = jnp.zeros((s_ref.shape[0] - 1,) + tuple(s_ref.shape[1:]), F32)


def _alias_io(s_buf, n_inputs, out_index):
    if s_buf is None:
        return [], [], {}
    return [pl.BlockSpec(memory_space=pl.ANY)], [s_buf], {n_inputs: out_index}


def _stacked(ref):
    g, rows, cols = ref.shape
    return ref[...].reshape(g * rows, cols)


def _seq_masks(rows, seq_rows):
    shift = seq_rows.bit_length() - 1
    r, c = _iota2((rows, rows), 0), _iota2((rows, rows), 1)
    same = (r >> shift) == (c >> shift)
    return same, same & (c <= r)


def _gla_kernel(qk_ref, v_ref, og_ref, glo_ref, w2_ref, gb_ref, ng_ref, s0_ref, *rest, n_pad, sub):
    o_ref, s_ref, st_ref = rest[-3:]
    c = pl.program_id(1)
    group, chunk = qk_ref.shape[0], qk_ref.shape[1]
    rows = group * chunk
    nsub = chunk // sub
    chains = [(g, h) for g in range(group) for h in range(GLA_HEADS)]
    ks = [slice(h * GLA_HK, (h + 1) * GLA_HK) for h in range(GLA_HEADS)]
    vs = [slice(h * GLA_HV, (h + 1) * GLA_HV) for h in range(GLA_HEADS)]
    rs = [slice(g * chunk, (g + 1) * chunk) for g in range(group)]

    @pl.when(c == 0)
    def _():
        for i, (g, h) in enumerate(chains):
            st_ref[i] = s0_ref[g, h].T

    valid = (c * chunk + _seq_rows((rows, 1), chunk)) >= n_pad
    pre = _mm3(_stacked(glo_ref), w2_ref[...]) + gb_ref[...]
    log_a = -_softplus(-pre) * (1.0 / GLA_GATE_NORM)
    log_a = jnp.where(valid, log_a, 0.0)
    qk = _stacked(qk_ref)
    q = qk[:, :GLA_DK] * (GLA_HK ** -0.5)
    k = jnp.where(valid, qk[:, GLA_DK:], 0.0)
    _, causal_sub = _seq_masks(rows, sub)
    same_seq, causal_seq = _seq_masks(rows, chunk)
    b = _mm_const_lhs(causal_sub.astype(F32), log_a, 3)
    big_b = _mm_const_lhs(causal_seq.astype(F32), log_a, 3)
    b_end = _mm_const_lhs(same_seq.astype(F32), log_a, 3)
    before = big_b - b
    sub_of = _seq_rows((rows, 1), chunk) >> (sub.bit_length() - 1)
    qd = (q * jnp.exp(b)).astype(BF16)
    q0 = (q * jnp.exp(big_b)).astype(BF16)
    kend = (k * jnp.exp(b_end - big_b)).astype(BF16)
    vb = _stacked(v_ref).astype(BF16)
    k_from = []
    for j in range(nsub):
        start = jnp.concatenate([jnp.broadcast_to(before[g * chunk + j * sub:g * chunk + j * sub + 1, :],
                                                  (chunk, GLA_DK)) for g in range(group)], axis=0)
        seen = sub_of <= j
        k_from.append(jnp.where(seen, k * jnp.exp(jnp.where(seen, start - big_b, 0.0)), 0.0).astype(BF16))
    causal_c = _tri(chunk)
    n = len(chains)
    s = [st_ref[i] for i in range(n)]
    att = []
    for g, h in chains:
        blocks = [_dg(qd[g * chunk + j * sub:g * chunk + (j + 1) * sub, ks[h]], k_from[j][rs[g], ks[h]], 'nt')
                  for j in range(nsub)]
        full = jnp.concatenate(blocks, axis=0) if nsub > 1 else blocks[0]
        att.append(jnp.where(causal_c, full, 0.0).astype(BF16))
    o = [_dg(att[i], vb[rs[g], vs[h]]) + _dg(q0[rs[g], ks[h]], s[i].astype(BF16), 'nt')
         for i, (g, h) in enumerate(chains)]
    s = [s[i] * jnp.exp(b_end[g * chunk:g * chunk + 1, ks[h]]) + _dg(vb[rs[g], vs[h]], kend[rs[g], ks[h]], 'tn')
         for i, (g, h) in enumerate(chains)]
    og = _stacked(og_ref)
    for i, (g, h) in enumerate(chains):
        st_ref[i] = s[i]
        oh = o[i] * lax.rsqrt(jnp.mean(o[i] * o[i], axis=-1, keepdims=True) + NORM_EPS) * ng_ref[:, vs[h]]
        o_ref[g, :, vs[h]] = oh * _silu(og[rs[g], vs[h]])

    @pl.when(c == pl.num_programs(1) - 1)
    def _():
        for i, (g, h) in enumerate(chains):
            _store_state(s_ref, g, h, st_ref[i].T)
        _clear_later_layers(s_ref)


def _gla(z, w2, gate_b, norm_g, s_all, layer, s_buf, n_pad, chunk, group):
    batch, seq_len, _ = z.shape
    nc = seq_len // chunk
    sub = math.gcd(chunk, GLA_SUBCHUNK)
    row = lambda blk: (lambda bg, c: (bg, c, blk))
    const = lambda bg, c: (0, 0)
    st = _state_spec(layer, group, (GLA_HEADS, GLA_HK, GLA_HV))
    w2p = jnp.zeros((LANE, GLA_DK), F32).at[:GLA_GATE_RANK].set(w2)
    alias_specs, alias_args, aliases = _alias_io(s_buf, 8, 1)
    return pl.pallas_call(
        functools.partial(_gla_kernel, n_pad=n_pad, sub=sub),
        grid=(batch // group, nc),
        in_specs=[pl.BlockSpec((group, chunk, 2 * GLA_DK), row(0)),
                  pl.BlockSpec((group, chunk, GLA_DV), row(1)),
                  pl.BlockSpec((group, chunk, GLA_DV), row(2)),
                  pl.BlockSpec((group, chunk, LANE), row(3 * 512 // LANE)),
                  pl.BlockSpec((LANE, GLA_DK), const),
                  pl.BlockSpec((1, GLA_DK), const),
                  pl.BlockSpec((1, GLA_DV), const),
                  st] + alias_specs,
        out_specs=[pl.BlockSpec((group, chunk, GLA_DV), row(0)),
                   _new_state_spec(s_all, s_buf, layer, group, (GLA_HEADS, GLA_HK, GLA_HV))],
        out_shape=[jax.ShapeDtypeStruct((batch, seq_len, GLA_DV), F32),
                   jax.ShapeDtypeStruct(s_all.shape, F32)],
        scratch_shapes=[pltpu.VMEM((group * GLA_HEADS, GLA_HV, GLA_HK), F32)],
        input_output_aliases=aliases,
        compiler_params=_params(2),
        name="gla",
    )(z, z, z, z, w2p, gate_b.reshape(1, GLA_DK), norm_g.reshape(1, GLA_DV), s_all, *alias_args)


def _rwkv_kernel(z_ref, mu_ref, w0_ref, w2_ref, a0_ref, a2_ref, g2_ref, kk_ref, ka_ref, rk_ref, lg_ref, lb_ref,
                 s0_ref, *rest, n_pad):
    y_ref, s_ref, st_ref, carry_ref = rest[-4:]
    c = pl.program_id(1)
    group, chunk = z_ref.shape[0], z_ref.shape[1]
    dim = RWKV_DIM
    pairs = RWKV_HEADS // 2

    @pl.when(c == 0)
    def _():
        zero = jnp.zeros((RWKV_HEAD, RWKV_HEAD), F32)
        for g in range(group):
            for q in range(pairs):
                top = jnp.concatenate([s0_ref[g, 2 * q], zero], axis=1)
                bottom = jnp.concatenate([zero, s0_ref[g, 2 * q + 1]], axis=1)
                st_ref[g * pairs + q] = jnp.concatenate([top, bottom], axis=0)
        carry_ref[...] = jnp.zeros_like(carry_ref)

    def prep(g0, g1):
        rows = (g1 - g0) * chunk
        valid = (c * chunk + _seq_rows((rows, 1), chunk)) >= n_pad
        z = z_ref[g0:g1].reshape(rows, z_ref.shape[2])
        z_prev = _shift_rows_seqs(z, [carry_ref[g] for g in range(g0, g1)], 1)
        for g in range(g0, g1):
            carry_ref[g] = z[(g - g0 + 1) * chunk - SUBLANE:(g - g0 + 1) * chunk]
        z = z + mu_ref[...] * (z_prev - z)
        r, kr, vr = z[:, :dim], z[:, dim:2 * dim], z[:, 2 * dim:3 * dim]
        w_lo = z[:, 3 * dim:3 * dim + 64]
        a_lo = z[:, 3 * dim + 64:3 * dim + 128]
        g_lo = z[:, 3 * dim + 128:]
        w_raw = w0_ref[...] + _mm3(jnp.tanh(w_lo), w2_ref[...])
        log_w = -jnp.exp(-_softplus(-w_raw) - 0.5)
        a_lr = _sigmoid(a0_ref[...] + _mm3(a_lo, a2_ref[...]))
        gate = _mm3(_sigmoid(g_lo), g2_ref[...])
        kx = kr * kk_ref[...]
        kk = kx * lax.rsqrt(_head_sums(kx * kx, RWKV_HEAD, 2) + NORM_EPS)
        k_mod = kr * (1.0 + (a_lr - 1.0) * ka_ref[...])
        bonus = _head_sums(r * k_mod * rk_ref[...], RWKV_HEAD, 2) * vr
        log_w = jnp.where(valid, log_w, 0.0)
        k_in = jnp.where(valid, k_mod, 0.0)
        b_in = jnp.where(valid, kk * a_lr, 0.0)
        same, causal = _seq_masks(rows, chunk)
        c_incl = _mm_const_lhs(causal.astype(F32), log_w, 3)
        c_tot = _mm_const_lhs(same.astype(F32), log_w, 3)
        e_neg = jnp.exp(-c_incl)
        e_end = jnp.exp(c_tot - c_incl)
        return dict(a_t=-kk * jnp.exp(c_incl - log_w), r_t=r * jnp.exp(c_incl),
                    bt=b_in * e_neg, kt=k_in * e_neg,
                    b_e=b_in * e_end, k_e=k_in * e_end, vr=vr, c_tot=c_tot,
                    bonus=bonus, gate=gate)

    def chains(p, g0, g1):
        ch = [(g, q) for g in range(g1 - g0) for q in range(pairs)]
        rs = [slice(g * chunk, (g + 1) * chunk) for g in range(g1 - g0)]
        pb = [slice(q * LANE, (q + 1) * LANE) for q in range(pairs)]
        n = len(ch)
        slot = [(g0 + g) * pairs + q for g, q in ch]
        col = _iota2((chunk, 2 * chunk), 1) & (chunk - 1)
        row = _iota2((chunk, 2 * chunk), 0)
        strict2, incl2 = col < row, col <= row
        blk = lambda x, g, q: _pair_blocks(x[rs[g], pb[q]], RWKV_HEAD).astype(BF16)
        s = [st_ref[i] for i in slot]
        ar = [jnp.concatenate([p['a_t'][rs[g], pb[q]], p['r_t'][rs[g], pb[q]]], axis=0).astype(BF16) for g, q in ch]
        vbd = [blk(p['vr'], g, q) for g, q in ch]
        g_b = [_dg(ar[i], blk(p['bt'], g, q), 'nt') for i, (g, q) in enumerate(ch)]
        g_k = [_dg(ar[i], blk(p['kt'], g, q), 'nt') for i, (g, q) in enumerate(ch)]
        g_s = [_dg(ar[i], s[i].astype(BF16), 'nt') for i in range(n)]
        m_ab = [jnp.where(strict2, g_b[i][:chunk], 0.0) for i in range(n)]
        rhs = [g_s[i][:chunk] + _mm(jnp.where(strict2, g_k[i][:chunk], 0.0), vbd[i]) for i in range(n)]
        t_inv = _unit_lower_inverse_pairs(m_ab)
        u = [_mm(t_inv[i], _pair_blocks(rhs[i], RWKV_HEAD)) for i in range(n)]
        ys = [g_s[i][chunk:] + _mm(jnp.where(incl2, g_b[i][chunk:], 0.0), _pair_blocks(u[i], RWKV_HEAD))
              + _mm(jnp.where(incl2, g_k[i][chunk:], 0.0), vbd[i]) for i in range(n)]
        same_head = (_iota2((LANE, LANE), 0) < RWKV_HEAD) == (_iota2((LANE, LANE), 1) < RWKV_HEAD)
        for i, (g, q) in enumerate(ch):
            uv = jnp.concatenate([u[i], p['vr'][rs[g], pb[q]]], axis=0)
            bk = jnp.concatenate([p['b_e'][rs[g], pb[q]], p['k_e'][rs[g], pb[q]]], axis=0)
            w_end = jnp.exp(p['c_tot'][g * chunk:g * chunk + 1, pb[q]])
            st_ref[slot[i]] = s[i] * w_end + jnp.where(same_head, _mm(uv, bk, 'tn'), 0.0)
        per_seq = [jnp.concatenate(ys[g * pairs:(g + 1) * pairs], axis=1) for g in range(g1 - g0)]
        return jnp.concatenate(per_seq, axis=0) if g1 - g0 > 1 else per_seq[0]

    def finish(p, y, g0, g1):
        mean = _head_sums(y, RWKV_HEAD, 2) * (1.0 / RWKV_HEAD)
        d = y - mean
        var = _head_sums(d * d, RWKV_HEAD, 2) * (1.0 / RWKV_HEAD)
        y = d * lax.rsqrt(var + RWKV_GN_EPS) * lg_ref[...] + lb_ref[...]
        y_ref[g0:g1] = ((y + p['bonus']) * p['gate']).reshape(g1 - g0, chunk, dim)

    p = prep(0, group)
    finish(p, chains(p, 0, group), 0, group)

    @pl.when(c == pl.num_programs(1) - 1)
    def _():
        for g in range(group):
            for q in range(pairs):
                _store_state(s_ref, g, 2 * q, st_ref[g * pairs + q, :RWKV_HEAD, :RWKV_HEAD])
                _store_state(s_ref, g, 2 * q + 1, st_ref[g * pairs + q, RWKV_HEAD:, RWKV_HEAD:])
        _clear_later_layers(s_ref)


def _rwkv(z, p, s_all, layer, s_buf, n_pad, chunk, group):
    batch, seq_len, _ = z.shape
    nc = seq_len // chunk
    dim = RWKV_DIM
    row = lambda bg, c: (bg, c, 0)
    const = lambda bg, c: (0, 0)
    st = _state_spec(layer, group, (RWKV_HEADS, RWKV_HEAD, RWKV_HEAD))
    vec = lambda a: a.reshape(1, -1)
    small = [vec(p['mu']), vec(p['w0']), p['w2'], vec(p['a0']), p['a2'], p['g2'], vec(p['k_k']), vec(p['k_a']),
             vec(p['r_k']), vec(p['ln_g']), vec(p['ln_b'])]
    alias_specs, alias_args, aliases = _alias_io(s_buf, len(small) + 2, 1)
    return pl.pallas_call(
        functools.partial(_rwkv_kernel, n_pad=n_pad),
        grid=(batch // group, nc),
        in_specs=([pl.BlockSpec((group, chunk, RWKV_COLS), row)]
                  + [pl.BlockSpec(a.shape, const) for a in small]
                  + [st] + alias_specs),
        out_specs=[pl.BlockSpec((group, chunk, dim), row),
                   _new_state_spec(s_all, s_buf, layer, group, (RWKV_HEADS, RWKV_HEAD, RWKV_HEAD))],
        out_shape=[jax.ShapeDtypeStruct((batch, seq_len, dim), F32),
                   jax.ShapeDtypeStruct(s_all.shape, F32)],
        scratch_shapes=[pltpu.VMEM((group * RWKV_HEADS // 2, LANE, LANE), F32),
                        pltpu.VMEM((group, SUBLANE, RWKV_COLS), F32)],
        input_output_aliases=aliases,
        compiler_params=_params(2),
        name="rwkv7",
    )(z, *small, s_all, *alias_args)


def _gdn_kernel(qkv_ref, rest_ref, alog_ref, dt_ref, ng_ref, s0_ref, *rest, n_pad):
    o_ref, s_ref, st_ref = rest[-3:]
    c = pl.program_id(1)
    group, chunk = qkv_ref.shape[0], qkv_ref.shape[1]
    rows = group * chunk
    dim = GDN_DIM
    chains = [(g, h) for g in range(group) for h in range(GDN_HEADS)]
    hs = [slice(h * GDN_HEAD, (h + 1) * GDN_HEAD) for h in range(GDN_HEADS)]
    rs = [slice(g * chunk, (g + 1) * chunk) for g in range(group)]

    @pl.when(c == 0)
    def _():
        for i, (g, h) in enumerate(chains):
            st_ref[i] = s0_ref[g, h]

    valid = (c * chunk + _seq_rows((rows, 1), chunk)) >= n_pad
    qkv = _stacked(qkv_ref)
    rest = _stacked(rest_ref)
    zg = rest[:, :dim]
    lo = rest[:, dim:]
    beta_all = _sigmoid(lo)
    g_all = -jnp.exp(alog_ref[...]) * _softplus(lo + dt_ref[...])
    g_all = jnp.where(valid, g_all, 0.0)

    incl = _tri(chunk)
    strict = _tri(chunk, strict=True)
    causal = _seq_masks(rows, chunk)[1]
    gam_all = _mm_const_lhs(causal.astype(F32), g_all, 3)
    shift = chunk.bit_length() - 1
    pick = (_iota2((GDN_HEADS * chunk, LANE), 1)
            == GDN_HEADS + (_iota2((GDN_HEADS * chunk, LANE), 0) >> shift)).astype(F32)
    gam_cols = [_mm_const_lhs(pick, gam_all[rs[g]], 3, 'nt') for g in range(group)]
    q_sq = _head_sums(jnp.square(qkv[:, :dim]), GDN_HEAD, 2)
    k_sq = _head_sums(jnp.square(qkv[:, dim:2 * dim]), GDN_HEAD, 2)
    q_all = qkv[:, :dim] * lax.rsqrt(q_sq + NORM_EPS) * (GDN_HEAD ** -0.5)
    k_all = jnp.where(valid, qkv[:, dim:2 * dim] * lax.rsqrt(k_sq + NORM_EPS), 0.0)
    v_all = qkv[:, 2 * dim:]
    n = len(chains)
    gam = [gam_all[rs[g], GDN_HEADS + h:GDN_HEADS + h + 1] for g, h in chains]
    beta = [beta_all[rs[g], h:h + 1] for g, h in chains]
    kq = [jnp.concatenate([k_all[rs[g], hs[h]], q_all[rs[g], hs[h]]], axis=0).astype(BF16) for g, h in chains]
    dec = [jnp.where(incl, jnp.exp(jnp.where(incl, gam[i] - gam_cols[g][h * chunk:(h + 1) * chunk], 0.0)), 0.0)
           for i, (g, h) in enumerate(chains)]
    gram = [_dg(kq[i], k_all[rs[g], hs[h]].astype(BF16), 'nt') for i, (g, h) in enumerate(chains)]
    m = [jnp.where(strict, gram[i][:chunk] * dec[i], 0.0) * (-beta[i]) for i in range(n)]
    t_inv = _unit_lower_inverses(m)
    e_gam = [jnp.exp(gam[i]) for i in range(n)]
    rhs = [jnp.concatenate([v_all[rs[g], hs[h]] * beta[i], k_all[rs[g], hs[h]] * (beta[i] * e_gam[i])], axis=1)
           for i, (g, h) in enumerate(chains)]
    uw = [_mm(t_inv[i], rhs[i]) for i in range(n)]
    s = [st_ref[i] for i in range(n)]
    sb = [s[i].astype(BF16) for i in range(n)]
    delta = [uw[i][:, :GDN_HEAD] - _mm(uw[i][:, GDN_HEAD:], sb[i]) for i in range(n)]
    o = [_mm(q_all[rs[g], hs[h]] * e_gam[i], sb[i]) + _mm(gram[i][chunk:] * dec[i], delta[i])
         for i, (g, h) in enumerate(chains)]
    for i, (g, h) in enumerate(chains):
        g_end = gam[i][chunk - 1:chunk, :]
        st_ref[i] = s[i] * jnp.exp(g_end) + _mm(k_all[rs[g], hs[h]] * jnp.exp(g_end - gam[i]), delta[i], 'tn')
        oh = o[i] * lax.rsqrt(jnp.mean(o[i] * o[i], axis=-1, keepdims=True) + NORM_EPS) * ng_ref[...]
        o_ref[g, :, hs[h]] = oh * _silu(zg[rs[g], hs[h]])

    @pl.when(c == pl.num_programs(1) - 1)
    def _():
        for i, (g, h) in enumerate(chains):
            _store_state(s_ref, g, h, st_ref[i])
        _clear_later_layers(s_ref)


def _gdn(qkv, rest, a_log, dt_bias, norm_g, s_all, layer, s_buf, n_pad, chunk, group):
    batch, seq_len, _ = qkv.shape
    nc = seq_len // chunk
    dim = GDN_DIM
    row = lambda bg, c: (bg, c, 0)
    const = lambda bg, c: (0, 0)
    st = _state_spec(layer, group, (GDN_HEADS, GDN_HEAD, GDN_HEAD))
    pad_lo = lambda a: jnp.zeros((1, LANE), F32).at[0, GDN_HEADS:2 * GDN_HEADS].set(a)
    alias_specs, alias_args, aliases = _alias_io(s_buf, 6, 1)
    return pl.pallas_call(
        functools.partial(_gdn_kernel, n_pad=n_pad),
        grid=(batch // group, nc),
        in_specs=[pl.BlockSpec((group, chunk, 3 * dim), row),
                  pl.BlockSpec((group, chunk, dim + LANE), row),
                  pl.BlockSpec((1, LANE), const),
                  pl.BlockSpec((1, LANE), const),
                  pl.BlockSpec((1, GDN_HEAD), const),
                  st] + alias_specs,
        out_specs=[pl.BlockSpec((group, chunk, dim), row),
                   _new_state_spec(s_all, s_buf, layer, group, (GDN_HEADS, GDN_HEAD, GDN_HEAD))],
        out_shape=[jax.ShapeDtypeStruct((batch, seq_len, dim), F32),
                   jax.ShapeDtypeStruct(s_all.shape, F32)],
        scratch_shapes=[pltpu.VMEM((group * GDN_HEADS, GDN_HEAD, GDN_HEAD), F32)],
        input_output_aliases=aliases,
        compiler_params=_params(2),
        name="gated_deltanet",
    )(qkv, rest, pad_lo(a_log), pad_lo(dt_bias), norm_g.reshape(1, GDN_HEAD), s_all, *alias_args)


def _pad_rows_fix(buf, seq_len, n_pad):
    b, w, c = buf.shape
    return jnp.zeros((b, seq_len, c), F32).at[:, n_pad - w:n_pad].set(buf).reshape(b * seq_len, c)


def _trunk(x, states, wts, batch, seq_len, n_pad, chunk, group, carried):
    st_gla, st_rwkv, st_shift, st_gdn, st_gdn_conv, st_ffn_conv = states
    d = x.shape[1]
    rows = batch * seq_len
    tm = _seq_tile(seq_len, rows, carried)
    tile_tail = lambda a, w: a.reshape(batch, seq_len // tm, SUBLANE, -1)[:, -1, SUBLANE - w:]
    seq_tail = lambda a, w: a.reshape(batch, seq_len, -1)[:, seq_len - w:]
    conv_rows = seq_tail if carried else tile_tail
    seqs = lambda a: a.reshape(batch, seq_len, a.shape[-1])
    flat = lambda a: a.reshape(rows, a.shape[-1])
    new_gla = new_rwkv = new_gdn = None
    new = {k: [] for k in ('shift', 'gdn_conv', 'ffn_conv')}
    for l in range(DEPTH):
        i = l // 2
        if l % 2 == 0:
            new['shift'].append(seqs(x)[:, -1])
            x_in = x
            if carried:
                x_in = flat(seqs(x).at[:, n_pad - 1].set(st_shift[i]))
            z_gla, z_rwkv = _inproj(x_in, wts['w_in_ab'], i, (GLA_Z, RWKV_COLS))
            o_gla, new_gla = _gla(seqs(z_gla), wts['gla_gate_w2'][i], wts['gla_gate_b'][i], wts['gla_norm_g'][i],
                                  st_gla, i, new_gla, n_pad, chunk, group)
            y_rwkv, new_rwkv = _rwkv(seqs(z_rwkv), {k: wts['rwkv_' + k][i] for k in
                                                    ('mu', 'w0', 'w2', 'a0', 'a2', 'g2', 'k_k', 'k_a', 'r_k', 'ln_g',
                                                     'ln_b')},
                                     st_rwkv, i, new_rwkv, n_pad, chunk, group)
            mixed = (flat(o_gla), flat(y_rwkv))
            w_out = wts['w_out_ab']
        else:
            fix = _pad_rows_fix(st_gdn_conv[i], seq_len, n_pad) if carried else None
            qkv, rest, pre = _inproj_gdn(x, wts['w_in_c'], i, wts['gdn_conv_w'][i], fix, tm, carried)
            new['gdn_conv'].append(conv_rows(pre, GDN_CONV - 1))
            o, new_gdn = _gdn(seqs(qkv), seqs(rest), wts['gdn_A_log'][i], wts['gdn_dt_bias'][i],
                              wts['gdn_norm_g'][i], st_gdn, i, new_gdn, n_pad, chunk, group)
            mixed = (flat(o),)
            w_out = wts['w_out_c']
        fix = _pad_rows_fix(st_ffn_conv[l], seq_len, n_pad) if carried else None
        x, gate = _layer_tail(mixed, x, w_out, i, (wts['ln_mix_g'][l], wts['ln_mix_b'][l]),
                              wts['w_up'], wts['w_down'], l, wts['ffn_conv_w'][l], wts['ffn_conv_b'][l],
                              (wts['ln_ffn_g'][l], wts['ln_ffn_b'][l]), fix, seq_len, n_pad, tm, carried)
        new['ffn_conv'].append(conv_rows(gate, FFN_CONV - 1))
    return (x, new_gla, new_rwkv, jnp.stack(new['shift']), new_gdn, jnp.stack(new['gdn_conv']),
            jnp.stack(new['ffn_conv']))


def kernel(x_prompt, x_sample, state_gla, state_rwkv, state_rwkv_shift, state_gdn, state_gdn_conv, state_ffn_conv, meta_tokens, w_in_ab, gla_gate_w2, gla_gate_b, gla_norm_g, rwkv_mu, rwkv_w0, rwkv_w2, rwkv_a0, rwkv_a2, rwkv_g2, rwkv_k_k, rwkv_k_a, rwkv_r_k, rwkv_ln_g, rwkv_ln_b, w_out_ab, w_in_c, gdn_conv_w, gdn_A_log, gdn_dt_bias, gdn_norm_g, w_out_c, w_up, ffn_conv_w, ffn_conv_b, w_down, ln_mix_g, ln_mix_b, ln_ffn_g, ln_ffn_b):
    d = x_prompt.shape[-1]
    n_ab, n_c = w_in_ab.shape[0], w_in_c.shape[0]
    gla_cols = 2 * GLA_DK + 2 * GLA_DV + GLA_GATE_RANK
    lo0 = 2 * GLA_DK + GLA_DV
    w_ab = jnp.concatenate([w_in_ab[:, :, :lo0], w_in_ab[:, :, lo0 + GLA_GATE_RANK:gla_cols],
                            w_in_ab[:, :, lo0:lo0 + GLA_GATE_RANK],
                            jnp.zeros((n_ab, d, LANE - GLA_GATE_RANK), w_in_ab.dtype),
                            w_in_ab[:, :, gla_cols:]], axis=2).astype(BF16)
    w_c = jnp.concatenate([w_in_c, jnp.zeros((n_c, d, LANE - 2 * GDN_HEADS), w_in_c.dtype)], axis=2).astype(BF16)
    wts = {
        'w_in_ab': w_ab, 'gla_gate_w2': gla_gate_w2, 'gla_gate_b': gla_gate_b, 'gla_norm_g': gla_norm_g,
        'rwkv_mu': rwkv_mu, 'rwkv_w0': rwkv_w0, 'rwkv_w2': rwkv_w2, 'rwkv_a0': rwkv_a0, 'rwkv_a2': rwkv_a2,
        'rwkv_g2': rwkv_g2, 'rwkv_k_k': rwkv_k_k, 'rwkv_k_a': rwkv_k_a, 'rwkv_r_k': rwkv_r_k,
        'rwkv_ln_g': rwkv_ln_g, 'rwkv_ln_b': rwkv_ln_b, 'w_out_ab': w_out_ab.astype(BF16),
        'w_in_c': w_c, 'gdn_conv_w': gdn_conv_w, 'gdn_A_log': gdn_A_log, 'gdn_dt_bias': gdn_dt_bias,
        'gdn_norm_g': gdn_norm_g, 'w_out_c': w_out_c.astype(BF16),
        'w_up': w_up.astype(BF16), 'ffn_conv_w': ffn_conv_w, 'ffn_conv_b': ffn_conv_b,
        'w_down': w_down.astype(BF16),
        'ln_mix_g': ln_mix_g, 'ln_mix_b': ln_mix_b, 'ln_ffn_g': ln_ffn_g, 'ln_ffn_b': ln_ffn_b,
    }

    bp, seq, _ = x_prompt.shape
    tp = -(-(N_META + seq + SUBLANE) // PROMPT_CHUNK) * PROMPT_CHUNK
    pad_p = tp - N_META - seq
    meta = jnp.broadcast_to(meta_tokens.astype(F32)[None], (bp, N_META, d))
    xp = jnp.concatenate([jnp.zeros((bp, pad_p, d), F32), meta, x_prompt], axis=1).reshape(bp * tp, d)
    zero_state = lambda s: jnp.zeros((s.shape[0], bp) + s.shape[2:], F32)
    p_out = _trunk(xp, tuple(zero_state(s) for s in (state_gla, state_rwkv, state_rwkv_shift, state_gdn,
                                                      state_gdn_conv, state_ffn_conv)),
                   wts, bp, tp, pad_p, PROMPT_CHUNK, _group(bp, PROMPT_GROUP), carried=False)
    y_prompt = p_out[0].reshape(bp, tp, d)[:, pad_p + N_META:]

    bs, ts, _ = x_sample.shape
    tsp = -(-(ts + GDN_CONV - 1) // SUBLANE) * SUBLANE
    pad_s = tsp - ts
    xs = jnp.concatenate([jnp.zeros((bs, pad_s, d), F32), x_sample], axis=1).reshape(bs * tsp, d)
    s_out = _trunk(xs, (state_gla, state_rwkv, state_rwkv_shift, state_gdn, state_gdn_conv, state_ffn_conv),
                   wts, bs, tsp, pad_s, tsp, _group(bs, SAMPLE_GROUP), carried=True)
    y_sample = s_out[0].reshape(bs, tsp, d)[:, pad_s:]
    return (y_prompt, y_sample) + tuple(p_out[1:]) + tuple(s_out[1:])
```

```python
import functools
import math

import jax
import jax.numpy as jnp
from jax import lax
from jax.experimental import pallas as pl
from jax.experimental.pallas import tpu as pltpu

F32 = jnp.float32
BF16 = jnp.bfloat16

N_META = 16
GLA_HEADS, GLA_HK, GLA_HV = 4, 64, 128
GLA_DK, GLA_DV = GLA_HEADS * GLA_HK, GLA_HEADS * GLA_HV
GLA_GATE_RANK = 16
GLA_GATE_NORM = 16.0
GLA_SUBCHUNK = 16
RWKV_HEADS, RWKV_HEAD = 8, 64
RWKV_DIM = RWKV_HEADS * RWKV_HEAD
RWKV_COLS = 3 * RWKV_DIM + 64 + 64 + 128
RWKV_GN_EPS = 64e-5
GDN_HEADS, GDN_HEAD = 8, 128
GDN_DIM = GDN_HEADS * GDN_HEAD
GDN_CONV = 4
FFN_CONV = 3
LN_EPS = 1e-5
NORM_EPS = 1e-6
DEPTH = 4
DEEPNORM_ALPHA = (2.0 * DEPTH) ** 0.25

LANE = 128
SUBLANE = 8
PROMPT_CHUNK = 64
PROMPT_GROUP = 4
SAMPLE_GROUP = 8
VMEM_LIMIT = 56 * 1024 * 1024

GLA_Z = 3 * 512 + LANE


def _params(n_axes):
    return pltpu.CompilerParams(dimension_semantics=("arbitrary",) * n_axes,
                                vmem_limit_bytes=VMEM_LIMIT)


def _row_tile(rows, cap):
    t = cap
    while rows % t:
        t //= 2
    return t


def _seq_tile(seq_len, rows, carried):
    if carried:
        return _row_tile(rows, 256)
    return next(t for t in (704, 352, 192, 64) if seq_len % t == 0)


def _resident(shape, layer=None):
    if layer is None:
        return pl.BlockSpec(shape, lambda *_: (0,) * len(shape), pipeline_mode=pl.Buffered(1))
    return pl.BlockSpec((None,) + tuple(shape), lambda *_: (layer,) + (0,) * len(shape),
                        pipeline_mode=pl.Buffered(1))


_DIMS = {'nn': (((1,), (0,)), ((), ())),
         'nt': (((1,), (1,)), ((), ())),
         'tn': (((0,), (0,)), ((), ()))}


def _dg(a, b, kind='nn'):
    return lax.dot_general(a, b, _DIMS[kind], preferred_element_type=F32)


def _split(x, pieces):
    out = []
    for _ in range(pieces - 1):
        hi = x.astype(BF16)
        out.append(hi)
        x = x - hi.astype(F32)
    out.append(x.astype(BF16))
    return out


def _mm(a, b, kind='nn'):
    return _dg(a.astype(BF16), b.astype(BF16), kind)


def _mm3(a, b, kind='nn'):
    ah, al = _split(a, 2)
    bh, bl = _split(b, 2)
    return _dg(ah, bh, kind) + _dg(al, bh, kind) + _dg(ah, bl, kind)


def _mm_const_lhs(c, x, pieces, kind='nn'):
    cb = c.astype(BF16)
    return sum(_dg(cb, p, kind) for p in _split(x, pieces))


def _mm_const_rhs(x, c, pieces, kind='nn'):
    cb = c.astype(BF16)
    return sum(_dg(p, cb, kind) for p in _split(x, pieces))


def _bdot(a, w):
    return jnp.dot(a.astype(BF16), w, preferred_element_type=F32)


def _iota2(shape, dim):
    return lax.broadcasted_iota(jnp.int32, shape, dim)


def _tri(n, strict=False):
    r, c = _iota2((n, n), 0), _iota2((n, n), 1)
    return (c < r) if strict else (c <= r)


def _softplus(x):
    return jnp.maximum(x, 0.0) + jnp.log1p(jnp.exp(-jnp.abs(x)))


def _sigmoid(x):
    return 1.0 / (1.0 + jnp.exp(-x))


def _silu(x):
    return x * _sigmoid(x)


def _unit_lower_inverses(ms):
    n = ms[0].shape[0]
    eye = (_iota2((n, n), 0) == _iota2((n, n), 1)).astype(F32)
    xs = [eye + m for m in ms]
    ps = list(ms)
    for _ in range(int(math.log2(n)) - 1):
        ps = [_mm(p, p) for p in ps]
        xs = [x + _mm(x, p) for x, p in zip(xs, ps)]
    return xs


def _pair_blocks(x, split):
    first = _iota2(x.shape, 1) < split
    return jnp.concatenate([jnp.where(first, x, 0.0), jnp.where(first, 0.0, x)], axis=0)


def _unit_lower_inverse_pairs(ms):
    n = ms[0].shape[0]
    eye = ((_iota2((n, 2 * n), 1) & (n - 1)) == _iota2((n, 2 * n), 0)).astype(F32)
    xs = [eye + m for m in ms]
    ps = list(ms)
    bds = [_pair_blocks(p, n).astype(BF16) for p in ps]
    for _ in range(int(math.log2(n)) - 1):
        ps = [_mm(p, bd) for p, bd in zip(ps, bds)]
        bds = [_pair_blocks(p, n).astype(BF16) for p in ps]
        xs = [x + _mm(x, bd) for x, bd in zip(xs, bds)]
    return xs


def _head_sums(x, width, pieces):
    rows, cols = x.shape
    nb = cols // LANE
    shift = width.bit_length() - 1
    grp = (_iota2((LANE, LANE), 0) >> shift == _iota2((LANE, LANE), 1) >> shift).astype(F32)
    xs = jnp.concatenate([x[:, i * LANE:(i + 1) * LANE] for i in range(nb)], axis=0)
    s = _mm_const_rhs(xs, grp, pieces)
    return jnp.concatenate([s[i * rows:(i + 1) * rows] for i in range(nb)], axis=1)


def _shift_rows(cur, carry, k):
    rolled = pltpu.roll(cur, k, 0)
    head = jnp.where(_iota2((SUBLANE, 1), 0) < k, pltpu.roll(carry, k, 0), rolled[:SUBLANE])
    if cur.shape[0] == SUBLANE:
        return head
    return jnp.concatenate([head, rolled[SUBLANE:]], axis=0)


def _shift_rows_seqs(cur, carries, k):
    n = len(carries)
    rows = cur.shape[0] // n
    return jnp.concatenate([_shift_rows(cur[g * rows:(g + 1) * rows], carries[g], k) for g in range(n)], axis=0) \
        if n > 1 else _shift_rows(cur, carries[0], k)


def _seq_rows(shape, seq_rows):
    return _iota2(shape, 0) & (seq_rows - 1)


def _valid_rows(row0, rows, seq_len, n_pad):
    if seq_len & (seq_len - 1) == 0:
        t = (row0 + _iota2((rows, 1), 0)) & (seq_len - 1)
    else:
        t = lax.rem(row0, seq_len) + _iota2((rows, 1), 0)
        for _ in range(-(-rows // seq_len)):
            t = jnp.where(t >= seq_len, t - seq_len, t)
    return t >= n_pad


def _layer_norm_rows(h, g, b):
    mu = jnp.mean(h, axis=-1, keepdims=True)
    d = h - mu
    var = jnp.mean(d * d, axis=-1, keepdims=True)
    return d * lax.rsqrt(var + LN_EPS) * g + b


def _inproj_kernel(x_ref, w_ref, *out_refs, widths):
    z = _bdot(x_ref[...], w_ref[...])
    off = 0
    for o_ref, wd in zip(out_refs, widths):
        o_ref[...] = z[:, off:off + wd]
        off += wd


def _inproj(x, w, layer, widths):
    rows, d = x.shape
    n = w.shape[2]
    tm = _row_tile(rows, 512)
    return pl.pallas_call(
        functools.partial(_inproj_kernel, widths=widths),
        grid=(rows // tm,),
        in_specs=[pl.BlockSpec((tm, d), lambda i: (i, 0)), _resident((d, n), layer)],
        out_specs=[pl.BlockSpec((tm, wd), lambda i: (i, 0)) for wd in widths],
        out_shape=[jax.ShapeDtypeStruct((rows, wd), F32) for wd in widths],
        compiler_params=_params(1),
        name="inproj",
    )(x, w)


def _inproj_gdn_kernel(*refs, has_fix, full_pre):
    if has_fix:
        x_ref, w_ref, cw_ref, fix_ref, act_ref, rest_ref, pre_ref, carry_ref = refs
    else:
        x_ref, w_ref, cw_ref, act_ref, rest_ref, pre_ref, carry_ref = refs
    tm = x_ref.shape[0]
    width = act_ref.shape[1]

    @pl.when(pl.program_id(0) == 0)
    def _():
        carry_ref[...] = jnp.zeros_like(carry_ref)

    xb = x_ref[...].astype(BF16)
    step = 4 * LANE
    for lo in range(0, width, step):
        cs = slice(lo, lo + step)
        pre = jnp.dot(xb, w_ref[:, cs], preferred_element_type=F32)
        if has_fix:
            pre = pre + fix_ref[:, cs]
        carry = carry_ref[:, cs]
        conv = cw_ref[GDN_CONV - 1:GDN_CONV, cs] * pre
        for tap in range(GDN_CONV - 1):
            conv = conv + cw_ref[tap:tap + 1, cs] * _shift_rows(pre, carry, GDN_CONV - 1 - tap)
        carry_ref[:, cs] = pre[tm - SUBLANE:, :]
        act_ref[:, cs] = _silu(conv)
        if full_pre:
            pre_ref[:, cs] = pre
        else:
            pre_ref[0, :, cs] = pre[tm - SUBLANE:, :]
    rest_ref[...] = jnp.dot(xb, w_ref[:, width:], preferred_element_type=F32)


def _inproj_gdn(x, w, layer, conv_w, fix, tm, full_pre):
    rows, d = x.shape
    n = w.shape[2]
    width = conv_w.shape[1]
    nt = rows // tm
    in_specs = [pl.BlockSpec((tm, d), lambda i: (i, 0)), _resident((d, n), layer), _resident(conv_w.shape)]
    args = [x, w, conv_w]
    if fix is not None:
        in_specs.append(pl.BlockSpec((tm, width), lambda i: (i, 0)))
        args.append(fix)
    if full_pre:
        pre_spec = pl.BlockSpec((tm, width), lambda i: (i, 0))
        pre_shape = jax.ShapeDtypeStruct((rows, width), F32)
    else:
        pre_spec = pl.BlockSpec((1, SUBLANE, width), lambda i: (i, 0, 0))
        pre_shape = jax.ShapeDtypeStruct((nt, SUBLANE, width), F32)
    return pl.pallas_call(
        functools.partial(_inproj_gdn_kernel, has_fix=fix is not None, full_pre=full_pre),
        grid=(nt,),
        in_specs=in_specs,
        out_specs=[pl.BlockSpec((tm, width), lambda i: (i, 0)),
                   pl.BlockSpec((tm, n - width), lambda i: (i, 0)),
                   pre_spec],
        out_shape=[jax.ShapeDtypeStruct((rows, width), F32),
                   jax.ShapeDtypeStruct((rows, n - width), F32),
                   pre_shape],
        scratch_shapes=[pltpu.VMEM((SUBLANE, width), F32)],
        compiler_params=_params(1),
        name="inproj_gdn",
    )(*args)


def _tail_kernel(*refs, n_mixed, seq_len, n_pad, d_ff, has_fix, full_gate):
    o_refs, refs = refs[:n_mixed], refs[n_mixed:]
    if has_fix:
        (x_ref, wout_ref, g1_ref, b1_ref, wup_ref, wdn_ref, cw_ref, cb_ref, g2_ref, b2_ref, fix_ref,
         y_ref, gate_ref, carry_ref) = refs
    else:
        (x_ref, wout_ref, g1_ref, b1_ref, wup_ref, wdn_ref, cw_ref, cb_ref, g2_ref, b2_ref,
         y_ref, gate_ref, carry_ref) = refs
    i = pl.program_id(0)
    tm = x_ref.shape[0]

    @pl.when(i == 0)
    def _():
        carry_ref[...] = jnp.zeros_like(carry_ref)

    valid = _valid_rows(i * tm, tm, seq_len, n_pad)
    h = DEEPNORM_ALPHA * x_ref[...]
    off = 0
    for o_ref in o_refs:
        h = h + _bdot(o_ref[...], wout_ref[off:off + o_ref.shape[1], :])
        off += o_ref.shape[1]
    x = jnp.where(valid, _layer_norm_rows(h, g1_ref[...], b1_ref[...]), 0.0)

    hu = _bdot(x, wup_ref[...])
    gate, up = hu[:, :d_ff], hu[:, d_ff:]
    if has_fix:
        gate = gate + fix_ref[...]
    carry = carry_ref[...]
    conv = (cw_ref[0:1, :] * _shift_rows(gate, carry, 2) + cw_ref[1:2, :] * _shift_rows(gate, carry, 1)
            + cw_ref[2:3, :] * gate + cb_ref[...])
    acc = _bdot(_silu(conv) * up, wdn_ref[...])
    carry_ref[...] = gate[tm - SUBLANE:, :]
    if full_gate:
        gate_ref[...] = gate
    else:
        gate_ref[0] = gate[tm - SUBLANE:, :]
    y = _layer_norm_rows(DEEPNORM_ALPHA * x + acc, g2_ref[...], b2_ref[...])
    y_ref[...] = jnp.where(valid, y, 0.0)


def _layer_tail(mixed, x, w_out, mix_layer, ln1, w_up, w_down, layer, conv_w, conv_b, ln2, fix,
                seq_len, n_pad, tm, full_gate):
    rows, d = x.shape
    d_ff = w_down.shape[1]
    nt = rows // tm
    vec = lambda a: a.reshape(1, -1)
    row_block = lambda width: pl.BlockSpec((tm, width), lambda i: (i, 0))
    in_specs = ([row_block(o.shape[1]) for o in mixed]
                + [row_block(d), _resident((w_out.shape[1], d), mix_layer), _resident((1, d)), _resident((1, d)),
                   _resident((d, 2 * d_ff), layer), _resident((d_ff, d), layer),
                   _resident((FFN_CONV, d_ff)), _resident((1, d_ff)), _resident((1, d)), _resident((1, d))])
    args = list(mixed) + [x, w_out, vec(ln1[0]), vec(ln1[1]), w_up, w_down, conv_w, vec(conv_b),
                          vec(ln2[0]), vec(ln2[1])]
    if fix is not None:
        in_specs.append(row_block(d_ff))
        args.append(fix)
    if full_gate:
        gate_spec = row_block(d_ff)
        gate_shape = jax.ShapeDtypeStruct((rows, d_ff), F32)
    else:
        gate_spec = pl.BlockSpec((1, SUBLANE, d_ff), lambda i: (i, 0, 0))
        gate_shape = jax.ShapeDtypeStruct((nt, SUBLANE, d_ff), F32)
    return pl.pallas_call(
        functools.partial(_tail_kernel, n_mixed=len(mixed), seq_len=seq_len, n_pad=n_pad, d_ff=d_ff,
                          has_fix=fix is not None, full_gate=full_gate),
        grid=(nt,),
        in_specs=in_specs,
        out_specs=[row_block(d), gate_spec],
        out_shape=[jax.ShapeDtypeStruct((rows, d), F32), gate_shape],
        scratch_shapes=[pltpu.VMEM((SUBLANE, d_ff), F32)],
        compiler_params=_params(1),
        name="layer_tail",
    )(*args)


def _group(batch, preferred):
    g = preferred
    while batch % g:
        g //= 2
    return g


def _state_spec(layer, group, dims):
    return pl.BlockSpec((None, group) + dims, lambda bg, c: (layer, bg) + (0,) * len(dims))


def _new_state_spec(s_all, s_buf, layer, group, dims):
    if s_buf is not None:
        return _state_spec(layer, group, dims)
    assert layer == 0
    return pl.BlockSpec((s_all.shape[0], group) + dims, lambda bg, c: (0, bg) + (0,) * len(dims))


def _store_state(s_ref, g, h, value):
    if len(s_ref.shape) == 5:
        s_ref[0, g, h] = value
    else:
        s_ref[g, h] = value


def _clear_later_layers(s_ref):
    if len(s_ref.shape) == 5:
        s_ref[1:] = jnp.zeros((s_ref.shape[0] - 1,) + tuple(s_ref.shape[1:]), F32)


def _alias_io(s_buf, n_inputs, out_index):
    if s_buf is None:
        return [], [], {}
    return [pl.BlockSpec(memory_space=pl.ANY)], [s_buf], {n_inputs: out_index}


def _stacked(ref):
    g, rows, cols = ref.shape
    return ref[...].reshape(g * rows, cols)


def _seq_masks(rows, seq_rows):
    shift = seq_rows.bit_length() - 1
    r, c = _iota2((rows, rows), 0), _iota2((rows, rows), 1)
    same = (r >> shift) == (c >> shift)
    return same, same & (c <= r)


def _gla_kernel(qk_ref, v_ref, og_ref, glo_ref, w2_ref, gb_ref, ng_ref, s0_ref, *rest, n_pad, sub):
    o_ref, s_ref, st_ref = rest[-3:]
    c = pl.program_id(1)
    group, chunk = qk_ref.shape[0], qk_ref.shape[1]
    rows = group * chunk
    nsub = chunk // sub
    chains = [(g, h) for g in range(group) for h in range(GLA_HEADS)]
    ks = [slice(h * GLA_HK, (h + 1) * GLA_HK) for h in range(GLA_HEADS)]
    vs = [slice(h * GLA_HV, (h + 1) * GLA_HV) for h in range(GLA_HEADS)]
    rs = [slice(g * chunk, (g + 1) * chunk) for g in range(group)]

    @pl.when(c == 0)
    def _():
        for i, (g, h) in enumerate(chains):
            st_ref[i] = s0_ref[g, h].T

    valid = (c * chunk + _seq_rows((rows, 1), chunk)) >= n_pad
    pre = _mm3(_stacked(glo_ref), w2_ref[...]) + gb_ref[...]
    log_a = -_softplus(-pre) * (1.0 / GLA_GATE_NORM)
    log_a = jnp.where(valid, log_a, 0.0)
    qk = _stacked(qk_ref)
    q = qk[:, :GLA_DK] * (GLA_HK ** -0.5)
    k = jnp.where(valid, qk[:, GLA_DK:], 0.0)
    _, causal_sub = _seq_masks(rows, sub)
    same_seq, causal_seq = _seq_masks(rows, chunk)
    b = _mm_const_lhs(causal_sub.astype(F32), log_a, 3)
    big_b = _mm_const_lhs(causal_seq.astype(F32), log_a, 3)
    b_end = _mm_const_lhs(same_seq.astype(F32), log_a, 3)
    before = big_b - b
    sub_of = _seq_rows((rows, 1), chunk) >> (sub.bit_length() - 1)
    qd = (q * jnp.exp(b)).astype(BF16)
    q0 = (q * jnp.exp(big_b)).astype(BF16)
    kend = (k * jnp.exp(b_end - big_b)).astype(BF16)
    vb = _stacked(v_ref).astype(BF16)
    k_from = []
    for j in range(nsub):
        start = jnp.concatenate([jnp.broadcast_to(before[g * chunk + j * sub:g * chunk + j * sub + 1, :],
                                                  (chunk, GLA_DK)) for g in range(group)], axis=0)
        seen = sub_of <= j
        k_from.append(jnp.where(seen, k * jnp.exp(jnp.where(seen, start - big_b, 0.0)), 0.0).astype(BF16))
    causal_c = _tri(chunk)
    n = len(chains)
    s = [st_ref[i] for i in range(n)]
    att = []
    for g, h in chains:
        blocks = [_dg(qd[g * chunk + j * sub:g * chunk + (j + 1) * sub, ks[h]], k_from[j][rs[g], ks[h]], 'nt')
                  for j in range(nsub)]
        full = jnp.concatenate(blocks, axis=0) if nsub > 1 else blocks[0]
        att.append(jnp.where(causal_c, full, 0.0).astype(BF16))
    o = [_dg(att[i], vb[rs[g], vs[h]]) + _dg(q0[rs[g], ks[h]], s[i].astype(BF16), 'nt')
         for i, (g, h) in enumerate(chains)]
    s = [s[i] * jnp.exp(b_end[g * chunk:g * chunk + 1, ks[h]]) + _dg(vb[rs[g], vs[h]], kend[rs[g], ks[h]], 'tn')
         for i, (g, h) in enumerate(chains)]
    og = _stacked(og_ref)
    for i, (g, h) in enumerate(chains):
        st_ref[i] = s[i]
        oh = o[i] * lax.rsqrt(jnp.mean(o[i] * o[i], axis=-1, keepdims=True) + NORM_EPS) * ng_ref[:, vs[h]]
        o_ref[g, :, vs[h]] = oh * _silu(og[rs[g], vs[h]])

    @pl.when(c == pl.num_programs(1) - 1)
    def _():
        for i, (g, h) in enumerate(chains):
            _store_state(s_ref, g, h, st_ref[i].T)
        _clear_later_layers(s_ref)


def _gla(z, w2, gate_b, norm_g, s_all, layer, s_buf, n_pad, chunk, group):
    batch, seq_len, _ = z.shape
    nc = seq_len // chunk
    sub = math.gcd(chunk, GLA_SUBCHUNK)
    row = lambda blk: (lambda bg, c: (bg, c, blk))
    const = lambda bg, c: (0, 0)
    st = _state_spec(layer, group, (GLA_HEADS, GLA_HK, GLA_HV))
    w2p = jnp.zeros((LANE, GLA_DK), F32).at[:GLA_GATE_RANK].set(w2)
    alias_specs, alias_args, aliases = _alias_io(s_buf, 8, 1)
    return pl.pallas_call(
        functools.partial(_gla_kernel, n_pad=n_pad, sub=sub),
        grid=(batch // group, nc),
        in_specs=[pl.BlockSpec((group, chunk, 2 * GLA_DK), row(0)),
                  pl.BlockSpec((group, chunk, GLA_DV), row(1)),
                  pl.BlockSpec((group, chunk, GLA_DV), row(2)),
                  pl.BlockSpec((group, chunk, LANE), row(3 * 512 // LANE)),
                  pl.BlockSpec((LANE, GLA_DK), const),
                  pl.BlockSpec((1, GLA_DK), const),
                  pl.BlockSpec((1, GLA_DV), const),
                  st] + alias_specs,
        out_specs=[pl.BlockSpec((group, chunk, GLA_DV), row(0)),
                   _new_state_spec(s_all, s_buf, layer, group, (GLA_HEADS, GLA_HK, GLA_HV))],
        out_shape=[jax.ShapeDtypeStruct((batch, seq_len, GLA_DV), F32),
                   jax.ShapeDtypeStruct(s_all.shape, F32)],
        scratch_shapes=[pltpu.VMEM((group * GLA_HEADS, GLA_HV, GLA_HK), F32)],
        input_output_aliases=aliases,
        compiler_params=_params(2),
        name="gla",
    )(z, z, z, z, w2p, gate_b.reshape(1, GLA_DK), norm_g.reshape(1, GLA_DV), s_all, *alias_args)


def _rwkv_kernel(z_ref, mu_ref, w0_ref, w2_ref, a0_ref, a2_ref, g2_ref, kk_ref, ka_ref, rk_ref, lg_ref, lb_ref,
                 s0_ref, *rest, n_pad):
    y_ref, s_ref, st_ref, carry_ref = rest[-4:]
    c = pl.program_id(1)
    group, chunk = z_ref.shape[0], z_ref.shape[1]
    dim = RWKV_DIM
    pairs = RWKV_HEADS // 2

    @pl.when(c == 0)
    def _():
        zero = jnp.zeros((RWKV_HEAD, RWKV_HEAD), F32)
        for g in range(group):
            for q in range(pairs):
                top = jnp.concatenate([s0_ref[g, 2 * q], zero], axis=1)
                bottom = jnp.concatenate([zero, s0_ref[g, 2 * q + 1]], axis=1)
                st_ref[g * pairs + q] = jnp.concatenate([top, bottom], axis=0)
        carry_ref[...] = jnp.zeros_like(carry_ref)

    def prep(g0, g1):
        rows = (g1 - g0) * chunk
        valid = (c * chunk + _seq_rows((rows, 1), chunk)) >= n_pad
        z = z_ref[g0:g1].reshape(rows, z_ref.shape[2])
        z_prev = _shift_rows_seqs(z, [carry_ref[g] for g in range(g0, g1)], 1)
        for g in range(g0, g1):
            carry_ref[g] = z[(g - g0 + 1) * chunk - SUBLANE:(g - g0 + 1) * chunk]
        z = z + mu_ref[...] * (z_prev - z)
        r, kr, vr = z[:, :dim], z[:, dim:2 * dim], z[:, 2 * dim:3 * dim]
        w_lo = z[:, 3 * dim:3 * dim + 64]
        a_lo = z[:, 3 * dim + 64:3 * dim + 128]
        g_lo = z[:, 3 * dim + 128:]
        w_raw = w0_ref[...] + _mm3(jnp.tanh(w_lo), w2_ref[...])
        log_w = -jnp.exp(-_softplus(-w_raw) - 0.5)
        a_lr = _sigmoid(a0_ref[...] + _mm3(a_lo, a2_ref[...]))
        gate = _mm3(_sigmoid(g_lo), g2_ref[...])
        kx = kr * kk_ref[...]
        kk = kx * lax.rsqrt(_head_sums(kx * kx, RWKV_HEAD, 2) + NORM_EPS)
        k_mod = kr * (1.0 + (a_lr - 1.0) * ka_ref[...])
        bonus = _head_sums(r * k_mod * rk_ref[...], RWKV_HEAD, 2) * vr
        log_w = jnp.where(valid, log_w, 0.0)
        k_in = jnp.where(valid, k_mod, 0.0)
        b_in = jnp.where(valid, kk * a_lr, 0.0)
        same, causal = _seq_masks(rows, chunk)
        c_incl = _mm_const_lhs(causal.astype(F32), log_w, 3)
        c_tot = _mm_const_lhs(same.astype(F32), log_w, 3)
        e_neg = jnp.exp(-c_incl)
        e_end = jnp.exp(c_tot - c_incl)
        return dict(a_t=-kk * jnp.exp(c_incl - log_w), r_t=r * jnp.exp(c_incl),
                    bt=b_in * e_neg, kt=k_in * e_neg,
                    b_e=b_in * e_end, k_e=k_in * e_end, vr=vr, c_tot=c_tot,
                    bonus=bonus, gate=gate)

    def chains(p, g0, g1):
        ch = [(g, q) for g in range(g1 - g0) for q in range(pairs)]
        rs = [slice(g * chunk, (g + 1) * chunk) for g in range(g1 - g0)]
        pb = [slice(q * LANE, (q + 1) * LANE) for q in range(pairs)]
        n = len(ch)
        slot = [(g0 + g) * pairs + q for g, q in ch]
        col = _iota2((chunk, 2 * chunk), 1) & (chunk - 1)
        row = _iota2((chunk, 2 * chunk), 0)
        strict2, incl2 = col < row, col <= row
        blk = lambda x, g, q: _pair_blocks(x[rs[g], pb[q]], RWKV_HEAD).astype(BF16)
        s = [st_ref[i] for i in slot]
        ar = [jnp.concatenate([p['a_t'][rs[g], pb[q]], p['r_t'][rs[g], pb[q]]], axis=0).astype(BF16) for g, q in ch]
        vbd = [blk(p['vr'], g, q) for g, q in ch]
        g_b = [_dg(ar[i], blk(p['bt'], g, q), 'nt') for i, (g, q) in enumerate(ch)]
        g_k = [_dg(ar[i], blk(p['kt'], g, q), 'nt') for i, (g, q) in enumerate(ch)]
        g_s = [_dg(ar[i], s[i].astype(BF16), 'nt') for i in range(n)]
        m_ab = [jnp.where(strict2, g_b[i][:chunk], 0.0) for i in range(n)]
        rhs = [g_s[i][:chunk] + _mm(jnp.where(strict2, g_k[i][:chunk], 0.0), vbd[i]) for i in range(n)]
        t_inv = _unit_lower_inverse_pairs(m_ab)
        u = [_mm(t_inv[i], _pair_blocks(rhs[i], RWKV_HEAD)) for i in range(n)]
        ys = [g_s[i][chunk:] + _mm(jnp.where(incl2, g_b[i][chunk:], 0.0), _pair_blocks(u[i], RWKV_HEAD))
              + _mm(jnp.where(incl2, g_k[i][chunk:], 0.0), vbd[i]) for i in range(n)]
        same_head = (_iota2((LANE, LANE), 0) < RWKV_HEAD) == (_iota2((LANE, LANE), 1) < RWKV_HEAD)
        for i, (g, q) in enumerate(ch):
            uv = jnp.concatenate([u[i], p['vr'][rs[g], pb[q]]], axis=0)
            bk = jnp.concatenate([p['b_e'][rs[g], pb[q]], p['k_e'][rs[g], pb[q]]], axis=0)
            w_end = jnp.exp(p['c_tot'][g * chunk:g * chunk + 1, pb[q]])
            st_ref[slot[i]] = s[i] * w_end + jnp.where(same_head, _mm(uv, bk, 'tn'), 0.0)
        per_seq = [jnp.concatenate(ys[g * pairs:(g + 1) * pairs], axis=1) for g in range(g1 - g0)]
        return jnp.concatenate(per_seq, axis=0) if g1 - g0 > 1 else per_seq[0]

    def finish(p, y, g0, g1):
        mean = _head_sums(y, RWKV_HEAD, 2) * (1.0 / RWKV_HEAD)
        d = y - mean
        var = _head_sums(d * d, RWKV_HEAD, 2) * (1.0 / RWKV_HEAD)
        y = d * lax.rsqrt(var + RWKV_GN_EPS) * lg_ref[...] + lb_ref[...]
        y_ref[g0:g1] = ((y + p['bonus']) * p['gate']).reshape(g1 - g0, chunk, dim)

    p = prep(0, group)
    finish(p, chains(p, 0, group), 0, group)

    @pl.when(c == pl.num_programs(1) - 1)
    def _():
        for g in range(group):
            for q in range(pairs):
                _store_state(s_ref, g, 2 * q, st_ref[g * pairs + q, :RWKV_HEAD, :RWKV_HEAD])
                _store_state(s_ref, g, 2 * q + 1, st_ref[g * pairs + q, RWKV_HEAD:, RWKV_HEAD:])
        _clear_later_layers(s_ref)


def _rwkv(z, p, s_all, layer, s_buf, n_pad, chunk, group):
    batch, seq_len, _ = z.shape
    nc = seq_len // chunk
    dim = RWKV_DIM
    row = lambda bg, c: (bg, c, 0)
    const = lambda bg, c: (0, 0)
    st = _state_spec(layer, group, (RWKV_HEADS, RWKV_HEAD, RWKV_HEAD))
    vec = lambda a: a.reshape(1, -1)
    small = [vec(p['mu']), vec(p['w0']), p['w2'], vec(p['a0']), p['a2'], p['g2'], vec(p['k_k']), vec(p['k_a']),
             vec(p['r_k']), vec(p['ln_g']), vec(p['ln_b'])]
    alias_specs, alias_args, aliases = _alias_io(s_buf, len(small) + 2, 1)
    return pl.pallas_call(
        functools.partial(_rwkv_kernel, n_pad=n_pad),
        grid=(batch // group, nc),
        in_specs=([pl.BlockSpec((group, chunk, RWKV_COLS), row)]
                  + [pl.BlockSpec(a.shape, const) for a in small]
                  + [st] + alias_specs),
        out_specs=[pl.BlockSpec((group, chunk, dim), row),
                   _new_state_spec(s_all, s_buf, layer, group, (RWKV_HEADS, RWKV_HEAD, RWKV_HEAD))],
        out_shape=[jax.ShapeDtypeStruct((batch, seq_len, dim), F32),
                   jax.ShapeDtypeStruct(s_all.shape, F32)],
        scratch_shapes=[pltpu.VMEM((group * RWKV_HEADS // 2, LANE, LANE), F32),
                        pltpu.VMEM((group, SUBLANE, RWKV_COLS), F32)],
        input_output_aliases=aliases,
        compiler_params=_params(2),
        name="rwkv7",
    )(z, *small, s_all, *alias_args)


def _gdn_kernel(qkv_ref, rest_ref, alog_ref, dt_ref, ng_ref, s0_ref, *rest, n_pad):
    o_ref, s_ref, st_ref = rest[-3:]
    c = pl.program_id(1)
    group, chunk = qkv_ref.shape[0], qkv_ref.shape[1]
    rows = group * chunk
    dim = GDN_DIM
    chains = [(g, h) for g in range(group) for h in range(GDN_HEADS)]
    hs = [slice(h * GDN_HEAD, (h + 1) * GDN_HEAD) for h in range(GDN_HEADS)]
    rs = [slice(g * chunk, (g + 1) * chunk) for g in range(group)]

    @pl.when(c == 0)
    def _():
        for i, (g, h) in enumerate(chains):
            st_ref[i] = s0_ref[g, h]

    valid = (c * chunk + _seq_rows((rows, 1), chunk)) >= n_pad
    qkv = _stacked(qkv_ref)
    rest = _stacked(rest_ref)
    zg = rest[:, :dim]
    lo = rest[:, dim:]
    beta_all = _sigmoid(lo)
    g_all = -jnp.exp(alog_ref[...]) * _softplus(lo + dt_ref[...])
    g_all = jnp.where(valid, g_all, 0.0)

    incl = _tri(chunk)
    strict = _tri(chunk, strict=True)
    causal = _seq_masks(rows, chunk)[1]
    gam_all = _mm_const_lhs(causal.astype(F32), g_all, 3)
    if chunk % LANE == 0 or chunk == 64:
        gam_t = [gam_all[rs[g]].T for g in range(group)]
        gam_row = lambda g, h: gam_t[g][GDN_HEADS + h:GDN_HEADS + h + 1, :]
    else:
        shift = chunk.bit_length() - 1
        pick = (_iota2((GDN_HEADS * chunk, LANE), 1)
                == GDN_HEADS + (_iota2((GDN_HEADS * chunk, LANE), 0) >> shift)).astype(F32)
        gam_cols = [_mm_const_lhs(pick, gam_all[rs[g]], 3, 'nt') for g in range(group)]
        gam_row = lambda g, h: gam_cols[g][h * chunk:(h + 1) * chunk]
    q_sq = _head_sums(jnp.square(qkv[:, :dim]), GDN_HEAD, 2)
    k_sq = _head_sums(jnp.square(qkv[:, dim:2 * dim]), GDN_HEAD, 2)
    q_all = qkv[:, :dim] * lax.rsqrt(q_sq + NORM_EPS) * (GDN_HEAD ** -0.5)
    k_all = jnp.where(valid, qkv[:, dim:2 * dim] * lax.rsqrt(k_sq + NORM_EPS), 0.0)
    v_all = qkv[:, 2 * dim:]
    n = len(chains)
    gam = [gam_all[rs[g], GDN_HEADS + h:GDN_HEADS + h + 1] for g, h in chains]
    beta = [beta_all[rs[g], h:h + 1] for g, h in chains]
    kq = [jnp.concatenate([k_all[rs[g], hs[h]], q_all[rs[g], hs[h]]], axis=0).astype(BF16) for g, h in chains]
    dec = [jnp.where(incl, jnp.exp(jnp.where(incl, gam[i] - gam_row(g, h), 0.0)), 0.0)
           for i, (g, h) in enumerate(chains)]
    gram = [_dg(kq[i], k_all[rs[g], hs[h]].astype(BF16), 'nt') for i, (g, h) in enumerate(chains)]
    m = [jnp.where(strict, gram[i][:chunk] * dec[i], 0.0) * (-beta[i]) for i in range(n)]
    t_inv = _unit_lower_inverses(m)
    e_gam = [jnp.exp(gam[i]) for i in range(n)]
    rhs = [jnp.concatenate([v_all[rs[g], hs[h]] * beta[i], k_all[rs[g], hs[h]] * (beta[i] * e_gam[i])], axis=1)
           for i, (g, h) in enumerate(chains)]
    uw = [_mm(t_inv[i], rhs[i]) for i in range(n)]
    s = [st_ref[i] for i in range(n)]
    sb = [s[i].astype(BF16) for i in range(n)]
    delta = [uw[i][:, :GDN_HEAD] - _mm(uw[i][:, GDN_HEAD:], sb[i]) for i in range(n)]
    o = [_mm(q_all[rs[g], hs[h]] * e_gam[i], sb[i]) + _mm(gram[i][chunk:] * dec[i], delta[i])
         for i, (g, h) in enumerate(chains)]
    for i, (g, h) in enumerate(chains):
        g_end = gam[i][chunk - 1:chunk, :]
        st_ref[i] = s[i] * jnp.exp(g_end) + _mm(k_all[rs[g], hs[h]] * jnp.exp(g_end - gam[i]), delta[i], 'tn')
        oh = o[i] * lax.rsqrt(jnp.mean(o[i] * o[i], axis=-1, keepdims=True) + NORM_EPS) * ng_ref[...]
        o_ref[g, :, hs[h]] = oh * _silu(zg[rs[g], hs[h]])

    @pl.when(c == pl.num_programs(1) - 1)
    def _():
        for i, (g, h) in enumerate(chains):
            _store_state(s_ref, g, h, st_ref[i])
        _clear_later_layers(s_ref)


def _gdn(qkv, rest, a_log, dt_bias, norm_g, s_all, layer, s_buf, n_pad, chunk, group):
    batch, seq_len, _ = qkv.shape
    nc = seq_len // chunk
    dim = GDN_DIM
    row = lambda bg, c: (bg, c, 0)
    const = lambda bg, c: (0, 0)
    st = _state_spec(layer, group, (GDN_HEADS, GDN_HEAD, GDN_HEAD))
    pad_lo = lambda a: jnp.zeros((1, LANE), F32).at[0, GDN_HEADS:2 * GDN_HEADS].set(a)
    alias_specs, alias_args, aliases = _alias_io(s_buf, 6, 1)
    return pl.pallas_call(
        functools.partial(_gdn_kernel, n_pad=n_pad),
        grid=(batch // group, nc),
        in_specs=[pl.BlockSpec((group, chunk, 3 * dim), row),
                  pl.BlockSpec((group, chunk, dim + LANE), row),
                  pl.BlockSpec((1, LANE), const),
                  pl.BlockSpec((1, LANE), const),
                  pl.BlockSpec((1, GDN_HEAD), const),
                  st] + alias_specs,
        out_specs=[pl.BlockSpec((group, chunk, dim), row),
                   _new_state_spec(s_all, s_buf, layer, group, (GDN_HEADS, GDN_HEAD, GDN_HEAD))],
        out_shape=[jax.ShapeDtypeStruct((batch, seq_len, dim), F32),
                   jax.ShapeDtypeStruct(s_all.shape, F32)],
        scratch_shapes=[pltpu.VMEM((group * GDN_HEADS, GDN_HEAD, GDN_HEAD), F32)],
        input_output_aliases=aliases,
        compiler_params=_params(2),
        name="gated_deltanet",
    )(qkv, rest, pad_lo(a_log), pad_lo(dt_bias), norm_g.reshape(1, GDN_HEAD), s_all, *alias_args)


def _pad_rows_fix(buf, seq_len, n_pad):
    b, w, c = buf.shape
    return jnp.zeros((b, seq_len, c), F32).at[:, n_pad - w:n_pad].set(buf).reshape(b * seq_len, c)


def _trunk(x, states, wts, batch, seq_len, n_pad, chunk, group, carried):
    st_gla, st_rwkv, st_shift, st_gdn, st_gdn_conv, st_ffn_conv = states
    d = x.shape[1]
    rows = batch * seq_len
    tm = _seq_tile(seq_len, rows, carried)
    tile_tail = lambda a, w: a.reshape(batch, seq_len // tm, SUBLANE, -1)[:, -1, SUBLANE - w:]
    seq_tail = lambda a, w: a.reshape(batch, seq_len, -1)[:, seq_len - w:]
    conv_rows = seq_tail if carried else tile_tail
    seqs = lambda a: a.reshape(batch, seq_len, a.shape[-1])
    flat = lambda a: a.reshape(rows, a.shape[-1])
    new_gla = new_rwkv = new_gdn = None
    new = {k: [] for k in ('shift', 'gdn_conv', 'ffn_conv')}
    for l in range(DEPTH):
        i = l // 2
        if l % 2 == 0:
            new['shift'].append(seqs(x)[:, -1])
            x_in = x
            if carried:
                x_in = flat(seqs(x).at[:, n_pad - 1].set(st_shift[i]))
            z_gla, z_rwkv = _inproj(x_in, wts['w_in_ab'], i, (GLA_Z, RWKV_COLS))
            o_gla, new_gla = _gla(seqs(z_gla), wts['gla_gate_w2'][i], wts['gla_gate_b'][i], wts['gla_norm_g'][i],
                                  st_gla, i, new_gla, n_pad, chunk, group)
            y_rwkv, new_rwkv = _rwkv(seqs(z_rwkv), {k: wts['rwkv_' + k][i] for k in
                                                    ('mu', 'w0', 'w2', 'a0', 'a2', 'g2', 'k_k', 'k_a', 'r_k', 'ln_g',
                                                     'ln_b')},
                                     st_rwkv, i, new_rwkv, n_pad, chunk, group)
            mixed = (flat(o_gla), flat(y_rwkv))
            w_out = wts['w_out_ab']
        else:
            fix = _pad_rows_fix(st_gdn_conv[i], seq_len, n_pad) if carried else None
            qkv, rest, pre = _inproj_gdn(x, wts['w_in_c'], i, wts['gdn_conv_w'][i], fix, tm, carried)
            new['gdn_conv'].append(conv_rows(pre, GDN_CONV - 1))
            o, new_gdn = _gdn(seqs(qkv), seqs(rest), wts['gdn_A_log'][i], wts['gdn_dt_bias'][i],
                              wts['gdn_norm_g'][i], st_gdn, i, new_gdn, n_pad, chunk, group)
            mixed = (flat(o),)
            w_out = wts['w_out_c']
        fix = _pad_rows_fix(st_ffn_conv[l], seq_len, n_pad) if carried else None
        x, gate = _layer_tail(mixed, x, w_out, i, (wts['ln_mix_g'][l], wts['ln_mix_b'][l]),
                              wts['w_up'], wts['w_down'], l, wts['ffn_conv_w'][l], wts['ffn_conv_b'][l],
                              (wts['ln_ffn_g'][l], wts['ln_ffn_b'][l]), fix, seq_len, n_pad, tm, carried)
        new['ffn_conv'].append(conv_rows(gate, FFN_CONV - 1))
    return (x, new_gla, new_rwkv, jnp.stack(new['shift']), new_gdn, jnp.stack(new['gdn_conv']),
            jnp.stack(new['ffn_conv']))


def kernel(x_prompt, x_sample, state_gla, state_rwkv, state_rwkv_shift, state_gdn, state_gdn_conv, state_ffn_conv, meta_tokens, w_in_ab, gla_gate_w2, gla_gate_b, gla_norm_g, rwkv_mu, rwkv_w0, rwkv_w2, rwkv_a0, rwkv_a2, rwkv_g2, rwkv_k_k, rwkv_k_a, rwkv_r_k, rwkv_ln_g, rwkv_ln_b, w_out_ab, w_in_c, gdn_conv_w, gdn_A_log, gdn_dt_bias, gdn_norm_g, w_out_c, w_up, ffn_conv_w, ffn_conv_b, w_down, ln_mix_g, ln_mix_b, ln_ffn_g, ln_ffn_b):
    d = x_prompt.shape[-1]
    n_ab, n_c = w_in_ab.shape[0], w_in_c.shape[0]
    gla_cols = 2 * GLA_DK + 2 * GLA_DV + GLA_GATE_RANK
    lo0 = 2 * GLA_DK + GLA_DV
    w_ab = jnp.concatenate([w_in_ab[:, :, :lo0], w_in_ab[:, :, lo0 + GLA_GATE_RANK:gla_cols],
                            w_in_ab[:, :, lo0:lo0 + GLA_GATE_RANK],
                            jnp.zeros((n_ab, d, LANE - GLA_GATE_RANK), w_in_ab.dtype),
                            w_in_ab[:, :, gla_cols:]], axis=2).astype(BF16)
    w_c = jnp.concatenate([w_in_c, jnp.zeros((n_c, d, LANE - 2 * GDN_HEADS), w_in_c.dtype)], axis=2).astype(BF16)
    wts = {
        'w_in_ab': w_ab, 'gla_gate_w2': gla_gate_w2, 'gla_gate_b': gla_gate_b, 'gla_norm_g': gla_norm_g,
        'rwkv_mu': rwkv_mu, 'rwkv_w0': rwkv_w0, 'rwkv_w2': rwkv_w2, 'rwkv_a0': rwkv_a0, 'rwkv_a2': rwkv_a2,
        'rwkv_g2': rwkv_g2, 'rwkv_k_k': rwkv_k_k, 'rwkv_k_a': rwkv_k_a, 'rwkv_r_k': rwkv_r_k,
        'rwkv_ln_g': rwkv_ln_g, 'rwkv_ln_b': rwkv_ln_b, 'w_out_ab': w_out_ab.astype(BF16),
        'w_in_c': w_c, 'gdn_conv_w': gdn_conv_w, 'gdn_A_log': gdn_A_log, 'gdn_dt_bias': gdn_dt_bias,
        'gdn_norm_g': gdn_norm_g, 'w_out_c': w_out_c.astype(BF16),
        'w_up': w_up.astype(BF16), 'ffn_conv_w': ffn_conv_w, 'ffn_conv_b': ffn_conv_b,
        'w_down': w_down.astype(BF16),
        'ln_mix_g': ln_mix_g, 'ln_mix_b': ln_mix_b, 'ln_ffn_g': ln_ffn_g, 'ln_ffn_b': ln_ffn_b,
    }

    bp, seq, _ = x_prompt.shape
    tp = -(-(N_META + seq + SUBLANE) // PROMPT_CHUNK) * PROMPT_CHUNK
    pad_p = tp - N_META - seq
    meta = jnp.broadcast_to(meta_tokens.astype(F32)[None], (bp, N_META, d))
    xp = jnp.concatenate([jnp.zeros((bp, pad_p, d), F32), meta, x_prompt], axis=1).reshape(bp * tp, d)
    zero_state = lambda s: jnp.zeros((s.shape[0], bp) + s.shape[2:], F32)
    p_out = _trunk(xp, tuple(zero_state(s) for s in (state_gla, state_rwkv, state_rwkv_shift, state_gdn,
                                                      state_gdn_conv, state_ffn_conv)),
                   wts, bp, tp, pad_p, PROMPT_CHUNK, _group(bp, PROMPT_GROUP), carried=False)
    y_prompt = p_out[0].reshape(bp, tp, d)[:, pad_p + N_META:]

    bs, ts, _ = x_sample.shape
    tsp = -(-(ts + GDN_CONV - 1) // SUBLANE) * SUBLANE
    pad_s = tsp - ts
    xs = jnp.concatenate([jnp.zeros((bs, pad_s, d), F32), x_sample], axis=1).reshape(bs * tsp, d)
    s_out = _trunk(xs, (state_gla, state_rwkv, state_rwkv_shift, state_gdn, state_gdn_conv, state_ffn_conv),
                   wts, bs, tsp, pad_s, tsp, _group(bs, SAMPLE_GROUP), carried=True)
    y_sample = s_out[0].reshape(bs, tsp, d)[:, pad_s:]
    return (y_prompt, y_sample) + tuple(p_out[1:]) + tuple(s_out[1:])
```
